```python
import math
import jax, jax.numpy as jnp
from jax import lax
import numpy as np

D_MODEL = 2048
BATCH = 1
SEQ = 8192
DEPTH = 4

ATTN_WIDTH = D_MODEL // 2
ATTN_HEAD_DIM = 128
N_ATTN_HEADS = ATTN_WIDTH // ATTN_HEAD_DIM
N_IDX_HEADS = 16
IDX_HEAD_DIM = 64
IDX_Q_WIDTH = N_IDX_HEADS * IDX_HEAD_DIM
TOPK_MAX = 256
Q_BLOCK = 128
N_BUCKETS = 32
MAX_DISTANCE = 128
NEG_INF = -1e30
POOL_WINDOWS = (2, 4, 8, 16)
POOL_GROUP_WIDTH = D_MODEL // 8
POOL_WIDTH = len(POOL_WINDOWS) * POOL_GROUP_WIDTH
EVEN_SPLITS = (ATTN_WIDTH, 2 * ATTN_WIDTH, 3 * ATTN_WIDTH,
               3 * ATTN_WIDTH + IDX_Q_WIDTH,
               3 * ATTN_WIDTH + IDX_Q_WIDTH + IDX_HEAD_DIM,
               3 * ATTN_WIDTH + IDX_Q_WIDTH + IDX_HEAD_DIM + N_IDX_HEADS)
EVEN_IN_WIDTH = EVEN_SPLITS[-1] + POOL_WIDTH
EVEN_OUT_WIDTH = ATTN_WIDTH + POOL_WIDTH
GDN_DK = 128
GDN_DV = 128
GDN_NK = D_MODEL // 128
GDN_NV = 2 * GDN_NK
GDN_QK_WIDTH = GDN_NK * GDN_DK
GDN_V_WIDTH = GDN_NV * GDN_DV
CONV_K = 4
CONV_WIDTH = 2 * GDN_QK_WIDTH + GDN_V_WIDTH
CHUNK = 64
ODD_IN_WIDTH = CONV_WIDTH + GDN_V_WIDTH + 2 * GDN_NV
D_FF = -(-8 * D_MODEL // (3 * 256)) * 256
RMS_EPS = 1e-6

kernel_name = "hybrid_dsa_pool_gdn_trunk"


def rms_norm(x, g):
    xf = x.astype(jnp.float32)
    y = xf * lax.rsqrt(jnp.mean(xf * xf, axis=-1, keepdims=True) + RMS_EPS)
    return (y * g.astype(jnp.float32)).astype(x.dtype)


def l2_norm(x):
    xf = x.astype(jnp.float32)
    return xf * lax.rsqrt(jnp.sum(xf * xf, axis=-1, keepdims=True) + RMS_EPS)


def t5_bucket(dist):
    max_exact = N_BUCKETS // 2
    d = jnp.maximum(dist, 0)
    log_ratio = jnp.log(jnp.maximum(d, 1).astype(jnp.float32) / max_exact) / math.log(MAX_DISTANCE / max_exact)
    large = jnp.minimum(max_exact + (log_ratio * (N_BUCKETS - max_exact)).astype(jnp.int32), N_BUCKETS - 1)
    return jnp.where(d < max_exact, d, large)


def dsa_attention(q, k, v, q_idx, k_idx, w_idx, rel_bias):
    B, S, H, Dh = q.shape
    topk = min(TOPK_MAX, S // 4)
    n_blk = S // Q_BLOCK

    def to_blocks(a):
        return a.reshape((B, n_blk, Q_BLOCK) + a.shape[2:]).swapaxes(0, 1)

    q_pos = jnp.arange(S, dtype=jnp.int32).reshape(n_blk, Q_BLOCK)
    key_pos = jnp.arange(S, dtype=jnp.int32)
    k_idx_f = k_idx.astype(jnp.float32)
    idx_scale = IDX_HEAD_DIM ** -0.5
    head_w_scale = N_IDX_HEADS ** -0.5
    attn_scale = Dh ** -0.5
    gather = jax.vmap(lambda a, i: a[i])

    def block(args):
        qb, qib, wb, tb = args
        dots = jnp.einsum('bqhd,bsd->bqhs', qib.astype(jnp.float32), k_idx_f) * idx_scale
        score = jnp.einsum('bqh,bqhs->bqs', wb.astype(jnp.float32) * head_w_scale, jax.nn.relu(dots))
        causal = key_pos[None, :] <= tb[:, None]
        score = jnp.where(causal[None], score, NEG_INF)
        _, sel = lax.top_k(score, topk)
        k_sel = gather(k, sel)
        v_sel = gather(v, sel)
        dist = tb[None, :, None] - sel
        bias = rel_bias[t5_bucket(dist)].astype(jnp.float32)
        logits = jnp.einsum('bqhd,bqjhd->bhqj', qb, k_sel).astype(jnp.float32) * attn_scale
        logits = logits + bias.transpose(0, 3, 1, 2)
        logits = jnp.where((dist >= 0)[:, None], logits, NEG_INF)
        p = jax.nn.softmax(logits, axis=-1).astype(v.dtype)
        return jnp.einsum('bhqj,bqjhd->bqhd', p, v_sel)

    out = lax.map(block, (to_blocks(q), to_blocks(q_idx), to_blocks(w_idx), q_pos))
    return out.swapaxes(0, 1).reshape(B, S, H * Dh)


def multiscale_pool(xp, pool_w, pool_scale):
    B, S, _ = xp.shape
    xf = xp.astype(jnp.float32)
    cs = jnp.concatenate([jnp.zeros((B, 1, POOL_WIDTH), jnp.float32), jnp.cumsum(xf, axis=1)], axis=1)
    t = jnp.arange(S, dtype=jnp.int32)
    groups = []
    for gi, w in enumerate(POOL_WINDOWS):
        sl = slice(gi * POOL_GROUP_WIDTH, (gi + 1) * POOL_GROUP_WIDTH)
        lo = jnp.maximum(t + 1 - w, 0)
        win_sum = cs[:, 1:, sl] - cs[:, lo, sl]
        count = (t + 1 - lo).astype(jnp.float32)
        groups.append(win_sum / count[None, :, None] - xf[..., sl])
    pooled = jnp.stack(groups, axis=2).astype(xp.dtype)
    mixed = jnp.einsum('bsgc,gcd->bsgd', pooled, pool_w).reshape(B, S, POOL_WIDTH)
    return mixed * pool_scale


def even_mixer(h, w_in, w_o, pool_w, pool_scale, rel_bias):
    B, S, _ = h.shape
    p = h @ w_in
    q, k, v, qi, ki, wi, xp = jnp.split(p, list(EVEN_SPLITS), axis=-1)
    q = q.reshape(B, S, N_ATTN_HEADS, ATTN_HEAD_DIM)
    k = k.reshape(B, S, N_ATTN_HEADS, ATTN_HEAD_DIM)
    v = v.reshape(B, S, N_ATTN_HEADS, ATTN_HEAD_DIM)
    qi = qi.reshape(B, S, N_IDX_HEADS, IDX_HEAD_DIM)
    attn_out = dsa_attention(q, k, v, qi, ki, wi, rel_bias)
    pool_out = multiscale_pool(xp, pool_w, pool_scale)
    return jnp.concatenate([attn_out, pool_out], axis=-1) @ w_o


def causal_depthwise_conv(x, w):
    C = x.shape[-1]
    return lax.conv_general_dilated(x, w[:, None, :], window_strides=(1,), padding=[(CONV_K - 1, 0)],
                                    dimension_numbers=('NWC', 'WIO', 'NWC'), feature_group_count=C)


def gated_delta_rule(q, k, v, g, beta):
    B, L, H, Dk = q.shape
    Dv = v.shape[-1]
    N = L // CHUNK

    def chunks(a):
        return a.reshape(B, N, CHUNK, H, a.shape[-1]).transpose(1, 0, 3, 2, 4)

    qc = chunks(q * Dk ** -0.5)
    kc = chunks(k)
    vc = chunks(v)
    bc = beta.reshape(B, N, CHUNK, H).transpose(1, 0, 3, 2)
    gc = jnp.cumsum(g.reshape(B, N, CHUNK, H).transpose(1, 0, 3, 2), axis=-1)
    tril = jnp.tril(jnp.ones((CHUNK, CHUNK), bool))
    strict = jnp.tril(jnp.ones((CHUNK, CHUNK), bool), -1)
    diff = gc[..., :, None] - gc[..., None, :]
    decay = jnp.where(tril, jnp.exp(jnp.where(tril, diff, 0.0)), 0.0)
    kb = kc * bc[..., None]
    vb = vc * bc[..., None]
    m = jnp.where(strict, jnp.einsum('nbhid,nbhjd->nbhij', kb, kc) * decay, 0.0)
    a_mat = m + jnp.eye(CHUNK, dtype=jnp.float32)
    rhs = jnp.concatenate([vb, kb * jnp.exp(gc)[..., None]], axis=-1)
    sol = lax.linalg.triangular_solve(a_mat, rhs, left_side=True, lower=True, unit_diagonal=True)
    u, w = sol[..., :Dv], sol[..., Dv:]
    intra = jnp.where(tril, jnp.einsum('nbhid,nbhjd->nbhij', qc, kc) * decay, 0.0)
    g_last = gc[..., -1]
    q_dec = qc * jnp.exp(gc)[..., None]
    k_dec = kc * jnp.exp(g_last[..., None] - gc)[..., None]
    d_last = jnp.exp(g_last)

    def step(state, xs):
        q_i, k_i, u_i, w_i, intra_i, dl_i = xs
        v_new = u_i - jnp.einsum('bhcd,bhde->bhce', w_i, state)
        out = jnp.einsum('bhcd,bhde->bhce', q_i, state) + jnp.einsum('bhij,bhje->bhie', intra_i, v_new)
        state = state * dl_i[..., None, None] + jnp.einsum('bhcd,bhce->bhde', k_i, v_new)
        return state, out

    state0 = jnp.zeros((B, H, Dk, Dv), jnp.float32)
    _, out = lax.scan(step, state0, (q_dec, k_dec, u, w, intra, d_last))
    return out.transpose(1, 0, 3, 2, 4).reshape(B, L, H, Dv)


def odd_mixer(h, w_in, conv_w, a_log, dt_bias, norm_w, w_o):
    B, S, _ = h.shape
    p = h @ w_in
    qkv = jax.nn.silu(causal_depthwise_conv(p[..., :CONV_WIDTH], conv_w))
    z = p[..., CONV_WIDTH:CONV_WIDTH + GDN_V_WIDTH]
    b = p[..., CONV_WIDTH + GDN_V_WIDTH:CONV_WIDTH + GDN_V_WIDTH + GDN_NV]
    a = p[..., CONV_WIDTH + GDN_V_WIDTH + GDN_NV:]
    q, k, v = jnp.split(qkv, [GDN_QK_WIDTH, 2 * GDN_QK_WIDTH], axis=-1)
    rep = GDN_NV // GDN_NK
    q = jnp.repeat(l2_norm(q.reshape(B, S, GDN_NK, GDN_DK)), rep, axis=2)
    k = jnp.repeat(l2_norm(k.reshape(B, S, GDN_NK, GDN_DK)), rep, axis=2)
    v = v.reshape(B, S, GDN_NV, GDN_DV).astype(jnp.float32)
    beta = jax.nn.sigmoid(b.astype(jnp.float32))
    g = -jnp.exp(a_log.astype(jnp.float32)) * jax.nn.softplus(a.astype(jnp.float32) + dt_bias.astype(jnp.float32))
    o = gated_delta_rule(q, k, v, g, beta).astype(h.dtype)
    o = rms_norm(o, norm_w) * jax.nn.silu(z.reshape(B, S, GDN_NV, GDN_DV))
    return o.reshape(B, S, GDN_V_WIDTH) @ w_o


def swiglu(h, w_gate, w_up, w_down):
    return (jax.nn.silu(h @ w_gate) * (h @ w_up)) @ w_down


def setup_inputs(seed: int = 0) -> dict:
    key = jax.random.key(seed)
    ks = jax.random.split(key, 20)
    n_even = (DEPTH + 1) // 2
    n_odd = DEPTH // 2
    f32 = jnp.float32

    def nrm(k, shape, fan_in):
        return jax.random.normal(k, shape, f32) * fan_in ** -0.5

    def gain(k, shape):
        return 1.0 + 0.05 * jax.random.normal(k, shape, f32)

    dt = jnp.exp(jax.random.uniform(ks[9], (n_odd, GDN_NV), f32, math.log(1e-3), math.log(1e-1)))
    return {
        "x": jax.random.normal(ks[0], (BATCH, SEQ, D_MODEL), f32),
        "rel_bias": 0.5 * jax.random.normal(ks[1], (N_BUCKETS, N_ATTN_HEADS), f32),
        "even_w_in": nrm(ks[2], (n_even, D_MODEL, EVEN_IN_WIDTH), D_MODEL),
        "even_w_o": nrm(ks[3], (n_even, EVEN_OUT_WIDTH, D_MODEL), EVEN_OUT_WIDTH),
        "pool_w": nrm(ks[4], (n_even, len(POOL_WINDOWS), POOL_GROUP_WIDTH, POOL_GROUP_WIDTH), POOL_GROUP_WIDTH),
        "pool_scale": gain(ks[5], (n_even, POOL_WIDTH)),
        "odd_w_in": nrm(ks[6], (n_odd, D_MODEL, ODD_IN_WIDTH), D_MODEL),
        "conv_w": nrm(ks[7], (n_odd, CONV_K, CONV_WIDTH), CONV_K),
        "a_log": jnp.log(jax.random.uniform(ks[8], (n_odd, GDN_NV), f32, 1.0, 16.0)),
        "dt_bias": dt + jnp.log(-jnp.expm1(-dt)),
        "gdn_norm_w": gain(ks[10], (n_odd, GDN_DV)),
        "odd_w_o": nrm(ks[11], (n_odd, GDN_V_WIDTH, D_MODEL), GDN_V_WIDTH),
        "ffn_w_gate": nrm(ks[12], (DEPTH, D_MODEL, D_FF), D_MODEL),
        "ffn_w_up": nrm(ks[13], (DEPTH, D_MODEL, D_FF), D_MODEL),
        "ffn_w_down": nrm(ks[14], (DEPTH, D_FF, D_MODEL), D_FF),
        "mix_pre_g": gain(ks[15], (DEPTH, D_MODEL)),
        "mix_post_g": gain(ks[16], (DEPTH, D_MODEL)),
        "ffn_pre_g": gain(ks[17], (DEPTH, D_MODEL)),
        "ffn_post_g": gain(ks[18], (DEPTH, D_MODEL)),
    }


def reference(x, rel_bias, even_w_in, even_w_o, pool_w, pool_scale, odd_w_in, conv_w, a_log, dt_bias,
              gdn_norm_w, odd_w_o, ffn_w_gate, ffn_w_up, ffn_w_down, mix_pre_g, mix_post_g, ffn_pre_g, ffn_post_g):
    for layer in range(DEPTH):
        i = layer // 2
        hn = rms_norm(x, mix_pre_g[layer])
        if layer % 2 == 0:
            m = even_mixer(hn, even_w_in[i], even_w_o[i], pool_w[i], pool_scale[i], rel_bias)
        else:
            m = odd_mixer(hn, odd_w_in[i], conv_w[i], a_log[i], dt_bias[i], gdn_norm_w[i], odd_w_o[i])
        x = x + rms_norm(m, mix_post_g[layer])
        hn = rms_norm(x, ffn_pre_g[layer])
        x = x + rms_norm(swiglu(hn, ffn_w_gate[layer], ffn_w_up[layer], ffn_w_down[layer]), ffn_post_g[layer])
    return x
```

```python
import functools
import math

import jax
import jax.numpy as jnp
from jax import lax
from jax.experimental import pallas as pl
from jax.experimental.pallas import tpu as pltpu

F32 = jnp.float32
BF16 = jnp.bfloat16
I32 = jnp.int32

RMS_EPS = 1e-6
NEG_INF = -1e30
INT_MIN = -2 ** 31

VMEM_LIMIT_BYTES = 56 * 1024 * 1024

ATTN_HEAD_DIM = 128
N_IDX_HEADS = 16
IDX_HEAD_DIM = 64
TOPK_MAX = 256
N_BUCKETS = 32
MAX_DISTANCE = 128
POOL_WINDOWS = (2, 4, 8, 16)
GDN_DK = 128
GDN_DV = 128
CONV_K = 4
CHUNK = 64


def _params(*sem):
    return pltpu.CompilerParams(dimension_semantics=sem, vmem_limit_bytes=VMEM_LIMIT_BYTES)

def _dot(a, b):
    return jnp.dot(a, b, preferred_element_type=F32)


def _dot_nt(a, b):
    return lax.dot_general(a, b, (((1,), (1,)), ((), ())), preferred_element_type=F32)


def _rms(x, g):
    ms = jnp.mean(x * x, axis=-1, keepdims=True)
    return x * lax.rsqrt(ms + RMS_EPS) * g


def _norm_matmul_kernel(x_ref, g_ref, w_ref, o_ref, hn_ref):
    @pl.when(pl.program_id(1) == 0)
    def _():
        hn_ref[...] = _rms(x_ref[...], g_ref[...]).astype(BF16)

    o_ref[...] = _dot(hn_ref[...], w_ref[...]).astype(o_ref.dtype)


def norm_matmul(x, g, w, out_dtype, tm, tn):
    m, k = x.shape
    n = w.shape[1]
    assert m % tm == 0 and n % tn == 0
    return pl.pallas_call(
        _norm_matmul_kernel,
        grid=(m // tm, n // tn),
        in_specs=[
            pl.BlockSpec((tm, k), lambda i, j: (i, 0)),
            pl.BlockSpec((1, k), lambda i, j: (0, 0)),
            pl.BlockSpec((k, tn), lambda i, j: (0, j)),
        ],
        out_specs=pl.BlockSpec((tm, tn), lambda i, j: (i, j)),
        out_shape=jax.ShapeDtypeStruct((m, n), out_dtype),
        scratch_shapes=[pltpu.VMEM((tm, k), BF16)],
        compiler_params=_params("parallel", "arbitrary"),
        name="norm_matmul",
    )(x, g.reshape(1, k), w)


def _norm_swiglu_kernel(x_ref, g_ref, wg_ref, wu_ref, o_ref, hn_ref):
    @pl.when(pl.program_id(1) == 0)
    def _():
        hn_ref[...] = _rms(x_ref[...], g_ref[...]).astype(BF16)

    hn = hn_ref[...]
    a = _dot(hn, wg_ref[...])
    b = _dot(hn, wu_ref[...])
    o_ref[...] = (a * jax.nn.sigmoid(a) * b).astype(o_ref.dtype)


def norm_swiglu(x, g, wg, wu, tm, tn):
    m, k = x.shape
    n = wg.shape[1]
    assert m % tm == 0 and n % tn == 0
    return pl.pallas_call(
        _norm_swiglu_kernel,
        grid=(m // tm, n // tn),
        in_specs=[
            pl.BlockSpec((tm, k), lambda i, j: (i, 0)),
            pl.BlockSpec((1, k), lambda i, j: (0, 0)),
            pl.BlockSpec((k, tn), lambda i, j: (0, j)),
            pl.BlockSpec((k, tn), lambda i, j: (0, j)),
        ],
        out_specs=pl.BlockSpec((tm, tn), lambda i, j: (i, j)),
        out_shape=jax.ShapeDtypeStruct((m, n), BF16),
        scratch_shapes=[pltpu.VMEM((tm, k), BF16)],
        compiler_params=_params("parallel", "arbitrary"),
        name="norm_swiglu",
    )(x, g.reshape(1, k), wg, wu)


def _mm_norm_res_kernel(*refs, n_seg, nk):
    a_refs = refs[:n_seg]
    w_refs = refs[n_seg:2 * n_seg]
    g_ref, r_ref, o_ref, acc_ref = refs[2 * n_seg:]
    kk = pl.program_id(1)

    @pl.when(kk == 0)
    def _():
        acc_ref[...] = jnp.zeros_like(acc_ref)

    part = _dot(a_refs[0][...], w_refs[0][...])
    for s in range(1, n_seg):
        part = part + _dot(a_refs[s][...], w_refs[s][...])
    acc_ref[...] += part

    @pl.when(kk == nk - 1)
    def _():
        o_ref[...] = r_ref[...] + _rms(acc_ref[...], g_ref[...])


def mm_norm_res(a_list, w, g, res, tm, tk):
    n_seg = len(a_list)
    m, ka = a_list[0].shape
    n = w.shape[1]
    assert w.shape[0] == n_seg * ka and m % tm == 0 and ka % tk == 0
    nk = ka // tk
    a_specs = [pl.BlockSpec((tm, tk), lambda i, k: (i, k)) for _ in range(n_seg)]
    w_specs = [pl.BlockSpec((tk, n), functools.partial(lambda i, k, s: (k + s * nk, 0), s=s))
               for s in range(n_seg)]
    return pl.pallas_call(
        functools.partial(_mm_norm_res_kernel, n_seg=n_seg, nk=nk),
        grid=(m // tm, nk),
        in_specs=a_specs + w_specs + [
            pl.BlockSpec((1, n), lambda i, k: (0, 0)),
            pl.BlockSpec((tm, n), lambda i, k: (i, 0)),
        ],
        out_specs=pl.BlockSpec((tm, n), lambda i, k: (i, 0)),
        out_shape=jax.ShapeDtypeStruct((m, n), F32),
        scratch_shapes=[pltpu.VMEM((tm, n), F32)],
        compiler_params=_params("parallel", "arbitrary"),
        name="mm_norm_res",
    )(*a_list, *([w] * n_seg), g.reshape(1, n), res)


IDX_TQ = 128
IDX_TK = 512


def _indexer_kernel(qi_ref, aux_ref, kt_ref, mask_ref, key_ref, qh_ref, wb_ref, *, topk, w_scale):
    tq, tk = IDX_TQ, IDX_TK
    s_len = mask_ref.shape[1]
    qb = pl.program_id(0)
    n_chunks = ((qb + 1) * tq - 1) // tk + 1

    lane = lax.broadcasted_iota(I32, (tq, 128), 1)
    aux = aux_ref[...]
    for h in range(N_IDX_HEADS):
        pair = qi_ref[:, (h // 2) * 128:(h // 2 + 1) * 128]
        keep = (lane < IDX_HEAD_DIM) if h % 2 == 0 else (lane >= IDX_HEAD_DIM)
        qh_ref[h] = jnp.where(keep, pair, jnp.zeros_like(pair))
        wcol = aux[:, IDX_HEAD_DIM + h:IDX_HEAD_DIM + h + 1] * w_scale
        wb_ref[h] = jnp.broadcast_to(wcol, (tq, 128))

    q_pos = qb * tq + lax.broadcasted_iota(I32, (tq, 128), 0)

    def score_chunk(c, carry):
        col0 = pl.multiple_of(c * tk, tk)
        kt = kt_ref[:, pl.ds(col0, tk)]
        accs = [jnp.zeros((tq, 128), F32) for _ in range(tk // 128)]
        for h in range(N_IDX_HEADS):
            d = _dot(qh_ref[h], kt)
            wb = wb_ref[h]
            for j in range(tk // 128):
                accs[j] = accs[j] + wb * jnp.maximum(d[:, j * 128:(j + 1) * 128], 0.0)
        for j in range(tk // 128):
            bits = pltpu.bitcast(accs[j], I32)
            skey = bits ^ ((bits >> 31) & 0x7FFFFFFF)
            k_pos = col0 + j * 128 + lane
            key_ref[:, pl.ds(pl.multiple_of(col0 + j * 128, 128), 128)] = jnp.where(k_pos <= q_pos, skey, INT_MIN)
        return carry

    lax.fori_loop(0, n_chunks, score_chunk, 0)

    def count_ge(cand):
        cand_b = jnp.broadcast_to(cand, (tq, 128))

        def body(c, cnt):
            col0 = pl.multiple_of(c * tk, tk)
            blk = key_ref[:, pl.ds(col0, tk)]
            for j in range(tk // 128):
                cnt = cnt + jnp.where(blk[:, j * 128:(j + 1) * 128] >= cand_b, 1, 0)
            return cnt

        cnt = lax.fori_loop(0, n_chunks, body, jnp.zeros((tq, 128), I32))
        return jnp.sum(cnt, axis=-1, keepdims=True)

    def bit_step(i, tau_u):
        bit = jnp.left_shift(jnp.int32(1), 31 - i)
        cand_u = tau_u | bit
        cnt = count_ge(cand_u ^ INT_MIN)
        return jnp.where(cnt >= topk, cand_u, tau_u)

    tau_u = lax.fori_loop(0, 32, bit_step, jnp.zeros((tq, 1), I32))
    tau = jnp.broadcast_to(jnp.maximum(tau_u ^ INT_MIN, INT_MIN + 1), (tq, 128))

    neg = jnp.full((tq, tk), NEG_INF, F32).astype(BF16)

    def write_chunk(c, carry):
        col0 = pl.multiple_of(c * tk, tk)

        @pl.when(c < n_chunks)
        def _():
            blk = key_ref[:, pl.ds(col0, tk)]
            parts = [jnp.where(blk[:, j * 128:(j + 1) * 128] >= tau, 0.0, NEG_INF) for j in range(tk // 128)]
            mask_ref[:, pl.ds(col0, tk)] = jnp.concatenate(parts, axis=-1).astype(BF16)

        @pl.when(c >= n_chunks)
        def _():
            mask_ref[:, pl.ds(col0, tk)] = neg

        return carry

    lax.fori_loop(0, s_len // tk, write_chunk, 0)


def indexer_mask(p_main, p_aux, kt2, topk):
    s_len = p_main.shape[0]
    tq = IDX_TQ
    w_scale = (N_IDX_HEADS ** -0.5) * (IDX_HEAD_DIM ** -0.5)
    return pl.pallas_call(
        functools.partial(_indexer_kernel, topk=topk, w_scale=w_scale),
        grid=(s_len // tq,),
        in_specs=[
            pl.BlockSpec((tq, 1024), lambda i: (i, 3)),
            pl.BlockSpec((tq, 128), lambda i: (i, 8)),
            pl.BlockSpec((128, s_len), lambda i: (0, 0)),
        ],
        out_specs=pl.BlockSpec((tq, s_len), lambda i: (i, 0)),
        out_shape=jax.ShapeDtypeStruct((s_len, s_len), BF16),
        scratch_shapes=[
            pltpu.VMEM((tq, s_len), I32),
            pltpu.VMEM((N_IDX_HEADS, tq, 128), BF16),
            pltpu.VMEM((N_IDX_HEADS, tq, 128), F32),
        ],
        compiler_params=_params("parallel"),
        name="indexer_mask",
    )(p_main, p_aux, kt2)


ATT_T = 256


def _attention_kernel(qb_tab, kb_tab, bsel_tab, q_ref, k_ref, v_ref, mask_ref, bias_ref, o_ref,
                      qs_ref, m_ref, l_ref, acc_ref, *, n_heads, scale):
    step = pl.program_id(0)
    qb = qb_tab[step]
    kb = kb_tab[step]
    dh = ATTN_HEAD_DIM

    @pl.when(kb == 0)
    def _():
        qs_ref[...] = (q_ref[...].astype(F32) * scale).astype(BF16)
        m_ref[...] = jnp.full_like(m_ref, NEG_INF)
        l_ref[...] = jnp.zeros_like(l_ref)
        acc_ref[...] = jnp.zeros_like(acc_ref)

    maskf = mask_ref[...].astype(F32)
    for h in range(n_heads):
        sl = slice(h * dh, (h + 1) * dh)
        s = _dot_nt(qs_ref[:, sl], k_ref[:, sl]) + bias_ref[0, h] + maskf
        m_prev = m_ref[h]
        m_new = jnp.maximum(m_prev, jnp.max(s, axis=-1, keepdims=True))
        alpha = jnp.exp(m_prev - m_new)
        p = jnp.exp(s - jnp.concatenate([m_new, m_new], axis=-1))
        l_ref[h] = alpha * l_ref[h] + jnp.sum(p, axis=-1, keepdims=True)
        acc_ref[h] = alpha * acc_ref[h] + _dot(p.astype(BF16), v_ref[:, sl])
        m_ref[h] = m_new

    @pl.when(kb == qb)
    def _():
        for h in range(n_heads):
            o_ref[:, h * dh:(h + 1) * dh] = (acc_ref[h] / l_ref[h]).astype(o_ref.dtype)


def masked_attention(p_main, mask, bias_tiles, n_heads):
    s_len = p_main.shape[0]
    t = ATT_T
    nb = s_len // t
    pairs = [(q, k) for q in range(nb) for k in range(q + 1)]
    qb_tab = jnp.array([p[0] for p in pairs], I32)
    kb_tab = jnp.array([p[1] for p in pairs], I32)
    bsel_tab = jnp.array([0 if q == k else (1 if q == k + 1 else 2) for q, k in pairs], I32)
    width = n_heads * ATTN_HEAD_DIM
    grid_spec = pltpu.PrefetchScalarGridSpec(
        num_scalar_prefetch=3,
        grid=(len(pairs),),
        in_specs=[
            pl.BlockSpec((t, width), lambda s, qt, kt, bt: (qt[s], 0)),
            pl.BlockSpec((t, width), lambda s, qt, kt, bt: (kt[s], 1)),
            pl.BlockSpec((t, width), lambda s, qt, kt, bt: (kt[s], 2)),
            pl.BlockSpec((t, t), lambda s, qt, kt, bt: (qt[s], kt[s])),
            pl.BlockSpec((1, n_heads, t, t), lambda s, qt, kt, bt: (bt[s], 0, 0, 0)),
        ],
        out_specs=pl.BlockSpec((t, width), lambda s, qt, kt, bt: (qt[s], 0)),
        scratch_shapes=[
            pltpu.VMEM((t, width), BF16),
            pltpu.VMEM((n_heads, t, 128), F32),
            pltpu.VMEM((n_heads, t, 128), F32),
            pltpu.VMEM((n_heads, t, ATTN_HEAD_DIM), F32),
        ],
    )
    return pl.pallas_call(
        functools.partial(_attention_kernel, n_heads=n_heads, scale=ATTN_HEAD_DIM ** -0.5),
        grid_spec=grid_spec,
        out_shape=jax.ShapeDtypeStruct((s_len, width), BF16),
        compiler_params=_params("arbitrary"),
        name="masked_attention",
    )(qb_tab, kb_tab, bsel_tab, p_main, p_main, p_main, mask, bias_tiles)


def _t5_bucket(dist):
    max_exact = N_BUCKETS // 2
    d = jnp.maximum(dist, 0)
    log_ratio = jnp.log(jnp.maximum(d, 1).astype(F32) / max_exact) / math.log(MAX_DISTANCE / max_exact)
    large = jnp.minimum(max_exact + (log_ratio * (N_BUCKETS - max_exact)).astype(I32), N_BUCKETS - 1)
    return jnp.where(d < max_exact, d, large)


def _bias_tiles(rel_bias, t):
    i = jnp.arange(t, dtype=I32)[:, None]
    j = jnp.arange(t, dtype=I32)[None, :]
    dists = jnp.stack([i - j, i - j + t, jnp.full((t, t), 2 * t, I32)])
    return rel_bias[_t5_bucket(dists)].astype(F32).transpose(0, 3, 1, 2)


POOL_TR = 512
POOL_G = 256
HALO = 16


def _pool_kernel(x_ref, prev_ref, w_ref, sc_ref, o_ref):
    i = pl.program_id(0)
    g = pl.program_id(1)
    tr = POOL_TR
    cur = x_ref[...]
    prev = jnp.where(i > 0, prev_ref[...], 0.0)
    ext = jnp.concatenate([prev, cur], axis=0)
    sums = []
    run = ext
    for shift in (1, 2, 4, 8):
        run = run + pltpu.roll(run, shift, 0)
        sums.append(run[HALO:])
    win_sum = jnp.where(g == 0, sums[0], jnp.where(g == 1, sums[1], jnp.where(g == 2, sums[2], sums[3])))
    win = jnp.left_shift(jnp.int32(2), g)
    t_pos = i * tr + lax.broadcasted_iota(I32, (tr, 1), 0)
    count = jnp.minimum(t_pos + 1, win).astype(F32)
    pooled = win_sum / count - cur
    o_ref[...] = (_dot(pooled.astype(BF16), w_ref[0]) * sc_ref[...]).astype(o_ref.dtype)


def multiscale_pool(p_aux, pool_w, pool_scale):
    s_len = p_aux.shape[0]
    tr, gw = POOL_TR, POOL_G
    ng = len(POOL_WINDOWS)
    return pl.pallas_call(
        _pool_kernel,
        grid=(s_len // tr, ng),
        in_specs=[
            pl.BlockSpec((tr, gw), lambda i, g: (i, g)),
            pl.BlockSpec((HALO, gw), lambda i, g: (jnp.maximum(i * (tr // HALO) - 1, 0), g)),
            pl.BlockSpec((1, gw, gw), lambda i, g: (g, 0, 0)),
            pl.BlockSpec((1, gw), lambda i, g: (0, g)),
        ],
        out_specs=pl.BlockSpec((tr, gw), lambda i, g: (i, g)),
        out_shape=jax.ShapeDtypeStruct((s_len, ng * gw), BF16),
        compiler_params=_params("parallel", "parallel"),
        name="multiscale_pool",
    )(p_aux, p_aux, pool_w, pool_scale.reshape(1, ng * gw))


CONV_TR = 512
CONV_TC = 512


def _conv_kernel(x_ref, prev_ref, w_ref, o_ref, *, n_norm_blocks):
    i = pl.program_id(0)
    j = pl.program_id(1)
    cur = x_ref[...].astype(F32)
    prev = jnp.where(i > 0, prev_ref[...].astype(F32), 0.0)
    ext = jnp.concatenate([prev, cur], axis=0)
    w = w_ref[...]
    y = ext * w[CONV_K - 1:CONV_K, :]
    for tap in range(CONV_K - 1):
        y = y + pltpu.roll(ext, CONV_K - 1 - tap, 0) * w[tap:tap + 1, :]
    y = y[HALO:]
    y = y * jax.nn.sigmoid(y)

    @pl.when(j < n_norm_blocks)
    def _():
        for h in range(CONV_TC // 128):
            yh = y[:, h * 128:(h + 1) * 128]
            ss = jnp.sum(yh * yh, axis=-1, keepdims=True)
            o_ref[:, h * 128:(h + 1) * 128] = yh * lax.rsqrt(ss + RMS_EPS)

    @pl.when(j >= n_norm_blocks)
    def _():
        o_ref[...] = y


def conv_silu_norm(p_main, conv_w, conv_width, qk_width):
    s_len = p_main.shape[0]
    tr, tc = CONV_TR, CONV_TC
    return pl.pallas_call(
        functools.partial(_conv_kernel, n_norm_blocks=2 * qk_width // tc),
        grid=(s_len // tr, conv_width // tc),
        in_specs=[
            pl.BlockSpec((tr, tc), lambda i, j: (i, j)),
            pl.BlockSpec((HALO, tc), lambda i, j: (jnp.maximum(i * (tr // HALO) - 1, 0), j)),
            pl.BlockSpec((CONV_K, tc), lambda i, j: (0, j)),
        ],
        out_specs=pl.BlockSpec((tr, tc), lambda i, j: (i, j)),
        out_shape=jax.ShapeDtypeStruct((s_len, conv_width), F32),
        compiler_params=_params("parallel", "parallel"),
        name="conv_silu_norm",
    )(p_main, p_main, conv_w)


def _split3(x):
    hi = x.astype(BF16)
    r1 = x - hi.astype(F32)
    mid = r1.astype(BF16)
    lo = (r1 - mid.astype(F32)).astype(BF16)
    return hi, mid, lo


def _dot_exact_lhs(a_bf16, x):
    hi, mid, lo = _split3(x)
    return _dot(a_bf16, lo) + _dot(a_bf16, mid) + _dot(a_bf16, hi)


def _dot_exact_rhs(x, b_bf16):
    hi, mid, lo = _split3(x)
    return _dot(lo, b_bf16) + _dot(mid, b_bf16) + _dot(hi, b_bf16)


def _softplus(x):
    return jnp.maximum(x, 0.0) + jnp.log(1.0 + jnp.exp(-jnp.abs(x)))


def _gdn_prep_kernel(q_ref, k_ref, v_ref, ba_ref, alog_ref, dtb_ref,
                     u_ref, w_ref, qd_ref, kdt_ref, intra_ref, dl_ref, *, n_vheads, rep):
    c = CHUNK
    dk = GDN_DK
    scale = dk ** -0.5
    ba = ba_ref[...]
    beta_all = jax.nn.sigmoid(ba)
    g_all = -jnp.exp(alog_ref[...]) * _softplus(ba + dtb_ref[...])
    row = lax.broadcasted_iota(I32, (c, c), 0)
    col = lax.broadcasted_iota(I32, (c, c), 1)
    tril = row >= col
    strict = row > col
    eye = jnp.where(row == col, 1.0, 0.0).astype(F32)
    lower = jnp.where(tril, 1.0, 0.0).astype(BF16)
    upper = jnp.where(row <= col, 1.0, 0.0).astype(BF16)
    gc_all = _dot_exact_lhs(lower, g_all)
    gct_all = _dot_exact_rhs(g_all.T, upper)

    for hv in range(n_vheads):
        kq = hv // rep
        kh = k_ref[:, kq * dk:(kq + 1) * dk]
        qh = q_ref[:, kq * dk:(kq + 1) * dk] * scale
        vh = v_ref[:, hv * GDN_DV:(hv + 1) * GDN_DV]
        beta = beta_all[:, hv:hv + 1]
        gcol = gc_all[:, n_vheads + hv:n_vheads + hv + 1]
        grow = gct_all[n_vheads + hv:n_vheads + hv + 1, :]
        decay = jnp.where(tril, jnp.exp(jnp.where(tril, gcol - grow, 0.0)), 0.0)
        kb = kh * beta
        vb = vh * beta
        kh16 = kh.astype(BF16)
        a = jnp.where(strict, -(_dot_nt(kb.astype(BF16), kh16) * decay), 0.0)
        t_inv = eye + a
        pw = a
        for _ in range(int(math.log2(c)) - 1):
            pw16 = pw.astype(BF16)
            pw = _dot(pw16, pw16)
            t_inv = t_inv + _dot(t_inv.astype(BF16), pw.astype(BF16))
        eg = jnp.exp(gcol)
        rhs = jnp.concatenate([vb, kb * eg], axis=-1).astype(BF16)
        sol = _dot(t_inv.astype(BF16), rhs)
        g_last = gcol[c - 1:c, :]
        sl = slice(hv * GDN_DV, (hv + 1) * GDN_DV)
        u_ref[:, sl] = sol[:, :GDN_DV]
        w_ref[:, sl] = sol[:, GDN_DV:].astype(BF16)
        qd_ref[:, sl] = (qh * eg).astype(BF16)
        kdt_ref[0, hv] = (kh * jnp.exp(g_last - gcol)).T.astype(BF16)
        intra_ref[0, hv] = jnp.where(tril, _dot_nt(qh.astype(BF16), kh16) * decay, 0.0).astype(BF16)
        dl_ref[0, hv] = jnp.broadcast_to(jnp.exp(g_last), (1, 128))


def gdn_prepare(qkv, ba, a_log, dt_bias, n_kheads, n_vheads):
    s_len = qkv.shape[0]
    c = CHUNK
    n = s_len // c
    qk_w = n_kheads * GDN_DK
    v_w = n_vheads * GDN_DV
    assert 2 * n_vheads <= 128
    pad = jnp.zeros((128 - 2 * n_vheads,), F32)
    alog_row = jnp.concatenate([jnp.zeros((n_vheads,), F32), a_log, pad]).reshape(1, 128)
    dtb_row = jnp.concatenate([jnp.zeros((n_vheads,), F32), dt_bias, pad]).reshape(1, 128)
    qb, vb = qk_w // qk_w, (2 * qk_w) // v_w
    return pl.pallas_call(
        functools.partial(_gdn_prep_kernel, n_vheads=n_vheads, rep=n_vheads // n_kheads),
        grid=(n,),
        in_specs=[
            pl.BlockSpec((c, qk_w), lambda i: (i, 0)),
            pl.BlockSpec((c, qk_w), lambda i: (i, qb)),
            pl.BlockSpec((c, v_w), lambda i: (i, vb)),
            pl.BlockSpec((c, 128), lambda i: (i, 0)),
            pl.BlockSpec((1, 128), lambda i: (0, 0)),
            pl.BlockSpec((1, 128), lambda i: (0, 0)),
        ],
        out_specs=[
            pl.BlockSpec((c, v_w), lambda i: (i, 0)),
            pl.BlockSpec((c, v_w), lambda i: (i, 0)),
            pl.BlockSpec((c, v_w), lambda i: (i, 0)),
            pl.BlockSpec((1, n_vheads, GDN_DK, c), lambda i: (i, 0, 0, 0)),
            pl.BlockSpec((1, n_vheads, c, c), lambda i: (i, 0, 0, 0)),
            pl.BlockSpec((1, n_vheads, 1, 128), lambda i: (i, 0, 0, 0)),
        ],
        out_shape=[
            jax.ShapeDtypeStruct((s_len, v_w), F32),
            jax.ShapeDtypeStruct((s_len, v_w), BF16),
            jax.ShapeDtypeStruct((s_len, v_w), BF16),
            jax.ShapeDtypeStruct((n, n_vheads, GDN_DK, c), BF16),
            jax.ShapeDtypeStruct((n, n_vheads, c, c), BF16),
            jax.ShapeDtypeStruct((n, n_vheads, 1, 128), F32),
        ],
        compiler_params=_params("parallel"),
        name="gdn_prepare",
    )(qkv, qkv, qkv, ba, alog_row, dtb_row)


SCAN_HG = 8


def _gdn_scan_kernel(u_ref, w_ref, qd_ref, kdt_ref, intra_ref, dl_ref, z_ref, nw_ref, o_ref, state_ref):
    @pl.when(pl.program_id(1) == 0)
    def _():
        state_ref[...] = jnp.zeros_like(state_ref)

    nw = nw_ref[...]
    for j in range(SCAN_HG):
        sl = slice(j * GDN_DV, (j + 1) * GDN_DV)
        state = state_ref[j]
        s16 = state.astype(BF16)
        v_new = u_ref[:, sl] - _dot(w_ref[:, sl], s16)
        v16 = v_new.astype(BF16)
        out = _dot(qd_ref[:, sl], s16) + _dot(intra_ref[0, j], v16)
        state_ref[j] = state * dl_ref[0, j] + _dot(kdt_ref[0, j], v16)
        z = z_ref[:, sl].astype(F32)
        o_ref[:, sl] = (_rms(out, nw) * (z * jax.nn.sigmoid(z))).astype(o_ref.dtype)


def gdn_scan(u, w, qd, kdt, intra, dl, p_main, norm_w, z_col_block):
    s_len, v_w = u.shape
    c = CHUNK
    n = s_len // c
    hg = SCAN_HG
    gw = hg * GDN_DV
    return pl.pallas_call(
        _gdn_scan_kernel,
        grid=(v_w // gw, n),
        in_specs=[
            pl.BlockSpec((c, gw), lambda g, i: (i, g)),
            pl.BlockSpec((c, gw), lambda g, i: (i, g)),
            pl.BlockSpec((c, gw), lambda g, i: (i, g)),
            pl.BlockSpec((1, hg, GDN_DK, c), lambda g, i: (i, g, 0, 0)),
            pl.BlockSpec((1, hg, c, c), lambda g, i: (i, g, 0, 0)),
            pl.BlockSpec((1, hg, 1, 128), lambda g, i: (i, g, 0, 0)),
            pl.BlockSpec((c, gw), lambda g, i: (i, z_col_block + g)),
            pl.BlockSpec((1, GDN_DV), lambda g, i: (0, 0)),
        ],
        out_specs=pl.BlockSpec((c, gw), lambda g, i: (i, g)),
        out_shape=jax.ShapeDtypeStruct((s_len, v_w), BF16),
        scratch_shapes=[pltpu.VMEM((hg, GDN_DK, GDN_DV), F32)],
        compiler_params=_params("parallel", "arbitrary"),
        name="gdn_scan",
    )(u, w, qd, kdt, intra, dl, p_main, norm_w.reshape(1, GDN_DV))


PROJ_TM = 1024
PROJ_TN = 1024
OUT_TM = 512
OUT_TK = 512
FFN_TN = 512


def _even_mixer(x, g_pre, g_post, w_in, w_o, pool_w, pool_scale, bias_tiles):
    s_len, d = x.shape
    attn_w = d // 2
    n_heads = attn_w // ATTN_HEAD_DIM
    idx_w = N_IDX_HEADS * IDX_HEAD_DIM
    pool_width = len(POOL_WINDOWS) * POOL_G
    main_w = 3 * attn_w + idx_w
    assert main_w == 4096 and pool_width == 1024
    small = IDX_HEAD_DIM + N_IDX_HEADS
    w_main = w_in[:, :main_w].astype(BF16)
    w_aux = jnp.concatenate([w_in[:, main_w + small:], w_in[:, main_w:main_w + small],
                             jnp.zeros((d, 128 - small), F32)], axis=1).astype(BF16)
    p_main = norm_matmul(x, g_pre, w_main, BF16, PROJ_TM, PROJ_TN)
    p_aux = norm_matmul(x, g_pre, w_aux, F32, PROJ_TM, 384)
    k_idx_t = p_aux[:, pool_width:pool_width + IDX_HEAD_DIM].astype(BF16).T
    kt2 = jnp.concatenate([k_idx_t, k_idx_t], axis=0)
    mask = indexer_mask(p_main, p_aux, kt2, min(TOPK_MAX, s_len // 4))
    attn = masked_attention(p_main, mask, bias_tiles, n_heads)
    pooled = multiscale_pool(p_aux, pool_w.astype(BF16), pool_scale)
    return mm_norm_res([attn, pooled], w_o.astype(BF16), g_post, x, OUT_TM, OUT_TK)


def _odd_mixer(x, g_pre, g_post, w_in, conv_w, a_log, dt_bias, norm_w, w_o):
    s_len, d = x.shape
    n_kheads = d // 128
    n_vheads = 2 * n_kheads
    qk_w = n_kheads * GDN_DK
    v_w = n_vheads * GDN_DV
    conv_width = 2 * qk_w + v_w
    main_w = conv_width + v_w
    w_main = w_in[:, :main_w].astype(BF16)
    w_ba = jnp.concatenate([w_in[:, main_w:], jnp.zeros((d, 128 - 2 * n_vheads), F32)], axis=1).astype(BF16)
    p_main = norm_matmul(x, g_pre, w_main, BF16, PROJ_TM, PROJ_TN)
    ba = norm_matmul(x, g_pre, w_ba, F32, PROJ_TM, 128)
    qkv = conv_silu_norm(p_main, conv_w, conv_width, qk_w)
    u, w, qd, kdt, intra, dl = gdn_prepare(qkv, ba, a_log, dt_bias, n_kheads, n_vheads)
    o = gdn_scan(u, w, qd, kdt, intra, dl, p_main, norm_w, conv_width // (SCAN_HG * GDN_DV))
    return mm_norm_res([o], w_o.astype(BF16), g_post, x, OUT_TM, OUT_TK)


def _ffn(x, g_pre, g_post, w_gate, w_up, w_down):
    act = norm_swiglu(x, g_pre, w_gate.astype(BF16), w_up.astype(BF16), PROJ_TM, FFN_TN)
    return mm_norm_res([act], w_down.astype(BF16), g_post, x, OUT_TM, OUT_TK)


def kernel(x, rel_bias, even_w_in, even_w_o, pool_w, pool_scale, odd_w_in, conv_w, a_log, dt_bias,
           gdn_norm_w, odd_w_o, ffn_w_gate, ffn_w_up, ffn_w_down, mix_pre_g, mix_post_g, ffn_pre_g, ffn_post_g):
    b, s_len, d = x.shape
    depth = ffn_w_gate.shape[0]
    bias_tiles = _bias_tiles(rel_bias, ATT_T)
    outs = []
    for bi in range(b):
        h = x[bi]
        for layer in range(depth):
            i = layer // 2
            if layer % 2 == 0:
                h = _even_mixer(h, mix_pre_g[layer], mix_post_g[layer], even_w_in[i], even_w_o[i],
                                pool_w[i], pool_scale[i], bias_tiles)
            else:
                h = _odd_mixer(h, mix_pre_g[layer], mix_post_g[layer], odd_w_in[i], conv_w[i], a_log[i],
                               dt_bias[i], gdn_norm_w[i], odd_w_o[i])
            h = _ffn(h, ffn_pre_g[layer], ffn_post_g[layer], ffn_w_gate[layer], ffn_w_up[layer], ffn_w_down[layer])
        outs.append(h)
    return jnp.stack(outs)
```

```python
import functools
import math

import jax
import jax.numpy as jnp
from jax import lax
from jax.experimental import pallas as pl
from jax.experimental.pallas import tpu as pltpu

F32 = jnp.float32
BF16 = jnp.bfloat16
I32 = jnp.int32

RMS_EPS = 1e-6
NEG_INF = -1e30
INT_MIN = -2 ** 31

VMEM_LIMIT_BYTES = 56 * 1024 * 1024

ATTN_HEAD_DIM = 128
N_IDX_HEADS = 16
IDX_HEAD_DIM = 64
TOPK_MAX = 256
N_BUCKETS = 32
MAX_DISTANCE = 128
POOL_WINDOWS = (2, 4, 8, 16)
GDN_DK = 128
GDN_DV = 128
CONV_K = 4
CHUNK = 64


def _params(*sem):
    return pltpu.CompilerParams(dimension_semantics=sem, vmem_limit_bytes=VMEM_LIMIT_BYTES)

def _dot(a, b):
    return jnp.dot(a, b, preferred_element_type=F32)


def _dot_nt(a, b):
    return lax.dot_general(a, b, (((1,), (1,)), ((), ())), preferred_element_type=F32)


def _rms(x, g):
    ms = jnp.mean(x * x, axis=-1, keepdims=True)
    return x * lax.rsqrt(ms + RMS_EPS) * g


def _norm_matmul_kernel(x_ref, g_ref, w_ref, o_ref, hn_ref):
    @pl.when(pl.program_id(1) == 0)
    def _():
        hn_ref[...] = _rms(x_ref[...], g_ref[...]).astype(BF16)

    o_ref[...] = _dot(hn_ref[...], w_ref[...]).astype(o_ref.dtype)


def norm_matmul(x, g, w, out_dtype, tm, tn):
    m, k = x.shape
    n = w.shape[1]
    assert m % tm == 0 and n % tn == 0
    return pl.pallas_call(
        _norm_matmul_kernel,
        grid=(m // tm, n // tn),
        in_specs=[
            pl.BlockSpec((tm, k), lambda i, j: (i, 0)),
            pl.BlockSpec((1, k), lambda i, j: (0, 0)),
            pl.BlockSpec((k, tn), lambda i, j: (0, j)),
        ],
        out_specs=pl.BlockSpec((tm, tn), lambda i, j: (i, j)),
        out_shape=jax.ShapeDtypeStruct((m, n), out_dtype),
        scratch_shapes=[pltpu.VMEM((tm, k), BF16)],
        compiler_params=_params("parallel", "arbitrary"),
        name="norm_matmul",
    )(x, g.reshape(1, k), w)


def _norm_swiglu_kernel(x_ref, g_ref, wg_ref, wu_ref, o_ref, hn_ref):
    @pl.when(pl.program_id(1) == 0)
    def _():
        hn_ref[...] = _rms(x_ref[...], g_ref[...]).astype(BF16)

    hn = hn_ref[...]
    a = _dot(hn, wg_ref[...])
    b = _dot(hn, wu_ref[...])
    o_ref[...] = (a * jax.nn.sigmoid(a) * b).astype(o_ref.dtype)


def norm_swiglu(x, g, wg, wu, tm, tn):
    m, k = x.shape
    n = wg.shape[1]
    assert m % tm == 0 and n % tn == 0
    return pl.pallas_call(
        _norm_swiglu_kernel,
        grid=(m // tm, n // tn),
        in_specs=[
            pl.BlockSpec((tm, k), lambda i, j: (i, 0)),
            pl.BlockSpec((1, k), lambda i, j: (0, 0)),
            pl.BlockSpec((k, tn), lambda i, j: (0, j)),
            pl.BlockSpec((k, tn), lambda i, j: (0, j)),
        ],
        out_specs=pl.BlockSpec((tm, tn), lambda i, j: (i, j)),
        out_shape=jax.ShapeDtypeStruct((m, n), BF16),
        scratch_shapes=[pltpu.VMEM((tm, k), BF16)],
        compiler_params=_params("parallel", "arbitrary"),
        name="norm_swiglu",
    )(x, g.reshape(1, k), wg, wu)


def _mm_norm_res_kernel(*refs, n_seg, nk):
    a_refs = refs[:n_seg]
    w_refs = refs[n_seg:2 * n_seg]
    g_ref, r_ref, o_ref, acc_ref = refs[2 * n_seg:]
    kk = pl.program_id(1)

    @pl.when(kk == 0)
    def _():
        acc_ref[...] = jnp.zeros_like(acc_ref)

    part = _dot(a_refs[0][...], w_refs[0][...])
    for s in range(1, n_seg):
        part = part + _dot(a_refs[s][...], w_refs[s][...])
    acc_ref[...] += part

    @pl.when(kk == nk - 1)
    def _():
        o_ref[...] = r_ref[...] + _rms(acc_ref[...], g_ref[...])


def mm_norm_res(a_list, w, g, res, tm, tk):
    n_seg = len(a_list)
    m, ka = a_list[0].shape
    n = w.shape[1]
    assert w.shape[0] == n_seg * ka and m % tm == 0 and ka % tk == 0
    nk = ka // tk
    a_specs = [pl.BlockSpec((tm, tk), lambda i, k: (i, k)) for _ in range(n_seg)]
    w_specs = [pl.BlockSpec((tk, n), functools.partial(lambda i, k, s: (k + s * nk, 0), s=s))
               for s in range(n_seg)]
    return pl.pallas_call(
        functools.partial(_mm_norm_res_kernel, n_seg=n_seg, nk=nk),
        grid=(m // tm, nk),
        in_specs=a_specs + w_specs + [
            pl.BlockSpec((1, n), lambda i, k: (0, 0)),
            pl.BlockSpec((tm, n), lambda i, k: (i, 0)),
        ],
        out_specs=pl.BlockSpec((tm, n), lambda i, k: (i, 0)),
        out_shape=jax.ShapeDtypeStruct((m, n), F32),
        scratch_shapes=[pltpu.VMEM((tm, n), F32)],
        compiler_params=_params("parallel", "arbitrary"),
        name="mm_norm_res",
    )(*a_list, *([w] * n_seg), g.reshape(1, n), res)


IDX_TQ = 128
IDX_TK = 512


def _indexer_kernel(qi_ref, aux_ref, kt_ref, mask_ref, key_ref, qh_ref, wb_ref, *, topk, w_scale):
    tq, tk = IDX_TQ, IDX_TK
    s_len = mask_ref.shape[1]
    qb = pl.program_id(0)
    n_chunks = ((qb + 1) * tq - 1) // tk + 1

    lane = lax.broadcasted_iota(I32, (tq, 128), 1)
    aux = aux_ref[...]
    for h in range(N_IDX_HEADS):
        pair = qi_ref[:, (h // 2) * 128:(h // 2 + 1) * 128]
        keep = (lane < IDX_HEAD_DIM) if h % 2 == 0 else (lane >= IDX_HEAD_DIM)
        qh_ref[h] = jnp.where(keep, pair, jnp.zeros_like(pair))
        wcol = aux[:, IDX_HEAD_DIM + h:IDX_HEAD_DIM + h + 1] * w_scale
        wb_ref[h] = jnp.broadcast_to(wcol, (tq, 128))

    q_pos = qb * tq + lax.broadcasted_iota(I32, (tq, 128), 0)

    def score_chunk(c, carry):
        col0 = pl.multiple_of(c * tk, tk)
        kt = kt_ref[:, pl.ds(col0, tk)]
        accs = [jnp.zeros((tq, 128), F32) for _ in range(tk // 128)]
        for h in range(N_IDX_HEADS):
            d = _dot(qh_ref[h], kt)
            wb = wb_ref[h]
            for j in range(tk // 128):
                accs[j] = accs[j] + wb * jnp.maximum(d[:, j * 128:(j + 1) * 128], 0.0)
        for j in range(tk // 128):
            bits = pltpu.bitcast(accs[j], I32)
            skey = bits ^ ((bits >> 31) & 0x7FFFFFFF)
            k_pos = col0 + j * 128 + lane
            key_ref[:, pl.ds(pl.multiple_of(col0 + j * 128, 128), 128)] = jnp.where(k_pos <= q_pos, skey, INT_MIN)
        return carry

    lax.fori_loop(0, n_chunks, score_chunk, 0)

    def count_ge(cand):
        cand_b = jnp.broadcast_to(cand, (tq, 128))

        def body(c, cnt):
            col0 = pl.multiple_of(c * tk, tk)
            blk = key_ref[:, pl.ds(col0, tk)]
            for j in range(tk // 128):
                cnt = cnt + jnp.where(blk[:, j * 128:(j + 1) * 128] >= cand_b, 1, 0)
            return cnt

        cnt = lax.fori_loop(0, n_chunks, body, jnp.zeros((tq, 128), I32))
        return jnp.sum(cnt, axis=-1, keepdims=True)

    def bit_step(i, tau_u):
        bit = jnp.left_shift(jnp.int32(1), 31 - i)
        cand_u = tau_u | bit
        cnt = count_ge(cand_u ^ INT_MIN)
        return jnp.where(cnt >= topk, cand_u, tau_u)

    tau_u = lax.fori_loop(0, 32, bit_step, jnp.zeros((tq, 1), I32))
    tau = jnp.broadcast_to(jnp.maximum(tau_u ^ INT_MIN, INT_MIN + 1), (tq, 128))

    neg = jnp.full((tq, tk), NEG_INF, F32).astype(BF16)

    def write_chunk(c, carry):
        col0 = pl.multiple_of(c * tk, tk)

        @pl.when(c < n_chunks)
        def _():
            blk = key_ref[:, pl.ds(col0, tk)]
            parts = [jnp.where(blk[:, j * 128:(j + 1) * 128] >= tau, 0.0, NEG_INF) for j in range(tk // 128)]
            mask_ref[:, pl.ds(col0, tk)] = jnp.concatenate(parts, axis=-1).astype(BF16)

        @pl.when(c >= n_chunks)
        def _():
            mask_ref[:, pl.ds(col0, tk)] = neg

        return carry

    lax.fori_loop(0, s_len // tk, write_chunk, 0)


def indexer_mask(p_main, p_aux, kt2, topk):
    s_len = p_main.shape[0]
    tq = IDX_TQ
    w_scale = (N_IDX_HEADS ** -0.5) * (IDX_HEAD_DIM ** -0.5)
    return pl.pallas_call(
        functools.partial(_indexer_kernel, topk=topk, w_scale=w_scale),
        grid=(s_len // tq,),
        in_specs=[
            pl.BlockSpec((tq, 1024), lambda i: (i, 3)),
            pl.BlockSpec((tq, 128), lambda i: (i, 8)),
            pl.BlockSpec((128, s_len), lambda i: (0, 0)),
        ],
        out_specs=pl.BlockSpec((tq, s_len), lambda i: (i, 0)),
        out_shape=jax.ShapeDtypeStruct((s_len, s_len), BF16),
        scratch_shapes=[
            pltpu.VMEM((tq, s_len), I32),
            pltpu.VMEM((N_IDX_HEADS, tq, 128), BF16),
            pltpu.VMEM((N_IDX_HEADS, tq, 128), F32),
        ],
        compiler_params=_params("parallel"),
        name="indexer_mask",
    )(p_main, p_aux, kt2)


ATT_T = 256


def _attention_kernel(qb_tab, kb_tab, bsel_tab, q_ref, k_ref, v_ref, mask_ref, bias_ref, o_ref,
                      qs_ref, m_ref, l_ref, acc_ref, *, n_heads, scale):
    step = pl.program_id(0)
    qb = qb_tab[step]
    kb = kb_tab[step]
    dh = ATTN_HEAD_DIM

    @pl.when(kb == 0)
    def _():
        qs_ref[...] = (q_ref[...].astype(F32) * scale).astype(BF16)
        m_ref[...] = jnp.full_like(m_ref, NEG_INF)
        l_ref[...] = jnp.zeros_like(l_ref)
        acc_ref[...] = jnp.zeros_like(acc_ref)

    maskf = mask_ref[...].astype(F32)
    for h in range(n_heads):
        sl = slice(h * dh, (h + 1) * dh)
        s = _dot_nt(qs_ref[:, sl], k_ref[:, sl]) + bias_ref[0, h] + maskf
        m_prev = m_ref[h]
        m_new = jnp.maximum(m_prev, jnp.max(s, axis=-1, keepdims=True))
        alpha = jnp.exp(m_prev - m_new)
        p = jnp.exp(s - jnp.concatenate([m_new, m_new], axis=-1))
        l_ref[h] = alpha * l_ref[h] + jnp.sum(p, axis=-1, keepdims=True)
        acc_ref[h] = alpha * acc_ref[h] + _dot(p.astype(BF16), v_ref[:, sl])
        m_ref[h] = m_new

    @pl.when(kb == qb)
    def _():
        for h in range(n_heads):
            o_ref[:, h * dh:(h + 1) * dh] = (acc_ref[h] / l_ref[h]).astype(o_ref.dtype)


def masked_attention(p_main, mask, bias_tiles, n_heads):
    s_len = p_main.shape[0]
    t = ATT_T
    nb = s_len // t
    pairs = [(q, k) for q in range(nb) for k in range(q + 1)]
    qb_tab = jnp.array([p[0] for p in pairs], I32)
    kb_tab = jnp.array([p[1] for p in pairs], I32)
    bsel_tab = jnp.array([0 if q == k else (1 if q == k + 1 else 2) for q, k in pairs], I32)
    width = n_heads * ATTN_HEAD_DIM
    grid_spec = pltpu.PrefetchScalarGridSpec(
        num_scalar_prefetch=3,
        grid=(len(pairs),),
        in_specs=[
            pl.BlockSpec((t, width), lambda s, qt, kt, bt: (qt[s], 0)),
            pl.BlockSpec((t, width), lambda s, qt, kt, bt: (kt[s], 1)),
            pl.BlockSpec((t, width), lambda s, qt, kt, bt: (kt[s], 2)),
            pl.BlockSpec((t, t), lambda s, qt, kt, bt: (qt[s], kt[s])),
            pl.BlockSpec((1, n_heads, t, t), lambda s, qt, kt, bt: (bt[s], 0, 0, 0)),
        ],
        out_specs=pl.BlockSpec((t, width), lambda s, qt, kt, bt: (qt[s], 0)),
        scratch_shapes=[
            pltpu.VMEM((t, width), BF16),
            pltpu.VMEM((n_heads, t, 128), F32),
            pltpu.VMEM((n_heads, t, 128), F32),
            pltpu.VMEM((n_heads, t, ATTN_HEAD_DIM), F32),
        ],
    )
    return pl.pallas_call(
        functools.partial(_attention_kernel, n_heads=n_heads, scale=ATTN_HEAD_DIM ** -0.5),
        grid_spec=grid_spec,
        out_shape=jax.ShapeDtypeStruct((s_len, width), BF16),
        compiler_params=_params("arbitrary"),
        name="masked_attention",
    )(qb_tab, kb_tab, bsel_tab, p_main, p_main, p_main, mask, bias_tiles)


def _t5_bucket(dist):
    max_exact = N_BUCKETS // 2
    d = jnp.maximum(dist, 0)
    log_ratio = jnp.log(jnp.maximum(d, 1).astype(F32) / max_exact) / math.log(MAX_DISTANCE / max_exact)
    large = jnp.minimum(max_exact + (log_ratio * (N_BUCKETS - max_exact)).astype(I32), N_BUCKETS - 1)
    return jnp.where(d < max_exact, d, large)


def _bias_tiles(rel_bias, t):
    i = jnp.arange(t, dtype=I32)[:, None]
    j = jnp.arange(t, dtype=I32)[None, :]
    dists = jnp.stack([i - j, i - j + t, jnp.full((t, t), 2 * t, I32)])
    bucket = _t5_bucket(dists)[:, None]
    tiles = jnp.zeros((3, rel_bias.shape[1], t, t), F32)
    for b in range(N_BUCKETS):
        tiles = jnp.where(bucket == b, rel_bias[b].astype(F32)[None, :, None, None], tiles)
    return tiles


POOL_TR = 512
POOL_G = 256
HALO = 16


def _pool_kernel(x_ref, prev_ref, w_ref, sc_ref, o_ref):
    i = pl.program_id(0)
    g = pl.program_id(1)
    tr = POOL_TR
    cur = x_ref[...]
    prev = jnp.where(i > 0, prev_ref[...], 0.0)
    ext = jnp.concatenate([prev, cur], axis=0)
    sums = []
    run = ext
    for shift in (1, 2, 4, 8):
        run = run + pltpu.roll(run, shift, 0)
        sums.append(run[HALO:])
    win_sum = jnp.where(g == 0, sums[0], jnp.where(g == 1, sums[1], jnp.where(g == 2, sums[2], sums[3])))
    win = jnp.left_shift(jnp.int32(2), g)
    t_pos = i * tr + lax.broadcasted_iota(I32, (tr, 1), 0)
    count = jnp.minimum(t_pos + 1, win).astype(F32)
    pooled = win_sum / count - cur
    o_ref[...] = (_dot(pooled.astype(BF16), w_ref[0]) * sc_ref[...]).astype(o_ref.dtype)


def multiscale_pool(p_aux, pool_w, pool_scale):
    s_len = p_aux.shape[0]
    tr, gw = POOL_TR, POOL_G
    ng = len(POOL_WINDOWS)
    return pl.pallas_call(
        _pool_kernel,
        grid=(s_len // tr, ng),
        in_specs=[
            pl.BlockSpec((tr, gw), lambda i, g: (i, g)),
            pl.BlockSpec((HALO, gw), lambda i, g: (jnp.maximum(i * (tr // HALO) - 1, 0), g)),
            pl.BlockSpec((1, gw, gw), lambda i, g: (g, 0, 0)),
            pl.BlockSpec((1, gw), lambda i, g: (0, g)),
        ],
        out_specs=pl.BlockSpec((tr, gw), lambda i, g: (i, g)),
        out_shape=jax.ShapeDtypeStruct((s_len, ng * gw), BF16),
        compiler_params=_params("parallel", "parallel"),
        name="multiscale_pool",
    )(p_aux, p_aux, pool_w, pool_scale.reshape(1, ng * gw))


CONV_TR = 512
CONV_TC = 512


def _conv_kernel(x_ref, prev_ref, w_ref, o_ref, *, n_norm_blocks):
    i = pl.program_id(0)
    j = pl.program_id(1)
    cur = x_ref[...].astype(F32)
    prev = jnp.where(i > 0, prev_ref[...].astype(F32), 0.0)
    ext = jnp.concatenate([prev, cur], axis=0)
    w = w_ref[...]
    y = ext * w[CONV_K - 1:CONV_K, :]
    for tap in range(CONV_K - 1):
        y = y + pltpu.roll(ext, CONV_K - 1 - tap, 0) * w[tap:tap + 1, :]
    y = y[HALO:]
    y = y * jax.nn.sigmoid(y)

    @pl.when(j < n_norm_blocks)
    def _():
        for h in range(CONV_TC // 128):
            yh = y[:, h * 128:(h + 1) * 128]
            ss = jnp.sum(yh * yh, axis=-1, keepdims=True)
            o_ref[:, h * 128:(h + 1) * 128] = yh * lax.rsqrt(ss + RMS_EPS)

    @pl.when(j >= n_norm_blocks)
    def _():
        o_ref[...] = y


def conv_silu_norm(p_main, conv_w, conv_width, qk_width):
    s_len = p_main.shape[0]
    tr, tc = CONV_TR, CONV_TC
    return pl.pallas_call(
        functools.partial(_conv_kernel, n_norm_blocks=2 * qk_width // tc),
        grid=(s_len // tr, conv_width // tc),
        in_specs=[
            pl.BlockSpec((tr, tc), lambda i, j: (i, j)),
            pl.BlockSpec((HALO, tc), lambda i, j: (jnp.maximum(i * (tr // HALO) - 1, 0), j)),
            pl.BlockSpec((CONV_K, tc), lambda i, j: (0, j)),
        ],
        out_specs=pl.BlockSpec((tr, tc), lambda i, j: (i, j)),
        out_shape=jax.ShapeDtypeStruct((s_len, conv_width), F32),
        compiler_params=_params("parallel", "parallel"),
        name="conv_silu_norm",
    )(p_main, p_main, conv_w)


def _split3(x):
    hi = x.astype(BF16)
    r1 = x - hi.astype(F32)
    mid = r1.astype(BF16)
    lo = (r1 - mid.astype(F32)).astype(BF16)
    return hi, mid, lo


def _dot_exact_lhs(a_bf16, x):
    hi, mid, lo = _split3(x)
    return _dot(a_bf16, lo) + _dot(a_bf16, mid) + _dot(a_bf16, hi)


def _dot_exact_rhs(x, b_bf16):
    hi, mid, lo = _split3(x)
    return _dot(lo, b_bf16) + _dot(mid, b_bf16) + _dot(hi, b_bf16)


def _softplus(x):
    return jnp.maximum(x, 0.0) + jnp.log(1.0 + jnp.exp(-jnp.abs(x)))


PREP_GROUP = 8


def _gdn_prep_kernel(q_ref, k_ref, v_ref, ba_ref, alog_ref, dtb_ref,
                     u_ref, w_ref, qd_ref, kdt_ref, intra_ref, dl_ref, *, n_vheads, rep):
    c = CHUNK
    dk = GDN_DK
    scale = dk ** -0.5
    ba = ba_ref[...]
    beta_all = jax.nn.sigmoid(ba)
    g_all = -jnp.exp(alog_ref[...]) * _softplus(ba + dtb_ref[...])
    row = lax.broadcasted_iota(I32, (c, c), 0)
    col = lax.broadcasted_iota(I32, (c, c), 1)
    tril = row >= col
    strict = row > col
    eye = jnp.where(row == col, 1.0, 0.0).astype(F32)
    lower = jnp.where(tril, 1.0, 0.0).astype(BF16)
    upper = jnp.where(row <= col, 1.0, 0.0).astype(BF16)
    gc_all = _dot_exact_lhs(lower, g_all)
    gct_all = _dot_exact_rhs(g_all.T, upper)

    for h0 in range(0, n_vheads, PREP_GROUP):
        group = range(h0, min(h0 + PREP_GROUP, n_vheads))
        pws, t_invs, rhss = [], [], []
        for hv in group:
            kq = hv // rep
            kh = k_ref[:, kq * dk:(kq + 1) * dk]
            qh = q_ref[:, kq * dk:(kq + 1) * dk] * scale
            vh = v_ref[:, hv * GDN_DV:(hv + 1) * GDN_DV]
            beta = beta_all[:, hv:hv + 1]
            gcol = gc_all[:, n_vheads + hv:n_vheads + hv + 1]
            grow = gct_all[n_vheads + hv:n_vheads + hv + 1, :]
            decay = jnp.where(tril, jnp.exp(jnp.where(tril, gcol - grow, 0.0)), 0.0)
            kb = kh * beta
            kh16 = kh.astype(BF16)
            a = jnp.where(strict, -(_dot_nt(kb.astype(BF16), kh16) * decay), 0.0)
            eg = jnp.exp(gcol)
            g_last = gcol[c - 1:c, :]
            sl = slice(hv * GDN_DV, (hv + 1) * GDN_DV)
            qd_ref[:, sl] = (qh * eg).astype(BF16)
            kdt_ref[0, hv] = (kh * jnp.exp(g_last - gcol)).T.astype(BF16)
            intra_ref[0, hv] = jnp.where(tril, _dot_nt(qh.astype(BF16), kh16) * decay, 0.0).astype(BF16)
            dl_ref[0, hv] = jnp.broadcast_to(jnp.exp(g_last), (1, 128))
            pws.append(a)
            t_invs.append(eye + a)
            rhss.append(jnp.concatenate([vh * beta, kb * eg], axis=-1).astype(BF16))
        for _ in range(int(math.log2(c)) - 1):
            pws = [_dot(p.astype(BF16), p.astype(BF16)) for p in pws]
            t_invs = [t + _dot(t.astype(BF16), p.astype(BF16)) for t, p in zip(t_invs, pws)]
        for hv, t_inv, rhs in zip(group, t_invs, rhss):
            sol = _dot(t_inv.astype(BF16), rhs)
            sl = slice(hv * GDN_DV, (hv + 1) * GDN_DV)
            u_ref[:, sl] = sol[:, :GDN_DV]
            w_ref[:, sl] = sol[:, GDN_DV:].astype(BF16)


def gdn_prepare(qkv, ba, a_log, dt_bias, n_kheads, n_vheads):
    s_len = qkv.shape[0]
    c = CHUNK
    n = s_len // c
    qk_w = n_kheads * GDN_DK
    v_w = n_vheads * GDN_DV
    assert 2 * n_vheads <= 128
    pad = jnp.zeros((128 - 2 * n_vheads,), F32)
    alog_row = jnp.concatenate([jnp.zeros((n_vheads,), F32), a_log, pad]).reshape(1, 128)
    dtb_row = jnp.concatenate([jnp.zeros((n_vheads,), F32), dt_bias, pad]).reshape(1, 128)
    qb, vb = qk_w // qk_w, (2 * qk_w) // v_w
    return pl.pallas_call(
        functools.partial(_gdn_prep_kernel, n_vheads=n_vheads, rep=n_vheads // n_kheads),
        grid=(n,),
        in_specs=[
            pl.BlockSpec((c, qk_w), lambda i: (i, 0)),
            pl.BlockSpec((c, qk_w), lambda i: (i, qb)),
            pl.BlockSpec((c, v_w), lambda i: (i, vb)),
            pl.BlockSpec((c, 128), lambda i: (i, 0)),
            pl.BlockSpec((1, 128), lambda i: (0, 0)),
            pl.BlockSpec((1, 128), lambda i: (0, 0)),
        ],
        out_specs=[
            pl.BlockSpec((c, v_w), lambda i: (i, 0)),
            pl.BlockSpec((c, v_w), lambda i: (i, 0)),
            pl.BlockSpec((c, v_w), lambda i: (i, 0)),
            pl.BlockSpec((1, n_vheads, GDN_DK, c), lambda i: (i, 0, 0, 0)),
            pl.BlockSpec((1, n_vheads, c, c), lambda i: (i, 0, 0, 0)),
            pl.BlockSpec((1, n_vheads, 1, 128), lambda i: (i, 0, 0, 0)),
        ],
        out_shape=[
            jax.ShapeDtypeStruct((s_len, v_w), F32),
            jax.ShapeDtypeStruct((s_len, v_w), BF16),
            jax.ShapeDtypeStruct((s_len, v_w), BF16),
            jax.ShapeDtypeStruct((n, n_vheads, GDN_DK, c), BF16),
            jax.ShapeDtypeStruct((n, n_vheads, c, c), BF16),
            jax.ShapeDtypeStruct((n, n_vheads, 1, 128), F32),
        ],
        compiler_params=_params("parallel"),
        name="gdn_prepare",
    )(qkv, qkv, qkv, ba, alog_row, dtb_row)


SCAN_HG = 8


def _gdn_scan_kernel(u_ref, w_ref, qd_ref, kdt_ref, intra_ref, dl_ref, z_ref, nw_ref, o_ref, state_ref):
    @pl.when(pl.program_id(1) == 0)
    def _():
        state_ref[...] = jnp.zeros_like(state_ref)

    nw = nw_ref[...]
    heads = range(SCAN_HG)
    sls = [slice(j * GDN_DV, (j + 1) * GDN_DV) for j in heads]
    states = [state_ref[j] for j in heads]
    s16 = [s.astype(BF16) for s in states]
    ws = [_dot(w_ref[:, sls[j]], s16[j]) for j in heads]
    qs = [_dot(qd_ref[:, sls[j]], s16[j]) for j in heads]
    v16 = [(u_ref[:, sls[j]] - ws[j]).astype(BF16) for j in heads]
    upd = [_dot(kdt_ref[0, j], v16[j]) for j in heads]
    outs = [qs[j] + _dot(intra_ref[0, j], v16[j]) for j in heads]
    for j in heads:
        state_ref[j] = states[j] * dl_ref[0, j] + upd[j]
    for j in heads:
        z = z_ref[:, sls[j]].astype(F32)
        o_ref[:, sls[j]] = (_rms(outs[j], nw) * (z * jax.nn.sigmoid(z))).astype(o_ref.dtype)


def gdn_scan(u, w, qd, kdt, intra, dl, p_main, norm_w, z_col_block):
    s_len, v_w = u.shape
    c = CHUNK
    n = s_len // c
    hg = SCAN_HG
    gw = hg * GDN_DV
    return pl.pallas_call(
        _gdn_scan_kernel,
        grid=(v_w // gw, n),
        in_specs=[
            pl.BlockSpec((c, gw), lambda g, i: (i, g)),
            pl.BlockSpec((c, gw), lambda g, i: (i, g)),
            pl.BlockSpec((c, gw), lambda g, i: (i, g)),
            pl.BlockSpec((1, hg, GDN_DK, c), lambda g, i: (i, g, 0, 0)),
            pl.BlockSpec((1, hg, c, c), lambda g, i: (i, g, 0, 0)),
            pl.BlockSpec((1, hg, 1, 128), lambda g, i: (i, g, 0, 0)),
            pl.BlockSpec((c, gw), lambda g, i: (i, z_col_block + g)),
            pl.BlockSpec((1, GDN_DV), lambda g, i: (0, 0)),
        ],
        out_specs=pl.BlockSpec((c, gw), lambda g, i: (i, g)),
        out_shape=jax.ShapeDtypeStruct((s_len, v_w), BF16),
        scratch_shapes=[pltpu.VMEM((hg, GDN_DK, GDN_DV), F32)],
        compiler_params=_params("parallel", "arbitrary"),
        name="gdn_scan",
    )(u, w, qd, kdt, intra, dl, p_main, norm_w.reshape(1, GDN_DV))


PROJ_TM = 1024
PROJ_TN = 1024
OUT_TM = 512
OUT_TK = 512
FFN_TN = 512


def _even_mixer(x, g_pre, g_post, w_in, w_o, pool_w, pool_scale, bias_tiles):
    s_len, d = x.shape
    attn_w = d // 2
    n_heads = attn_w // ATTN_HEAD_DIM
    idx_w = N_IDX_HEADS * IDX_HEAD_DIM
    pool_width = len(POOL_WINDOWS) * POOL_G
    main_w = 3 * attn_w + idx_w
    assert main_w == 4096 and pool_width == 1024
    small = IDX_HEAD_DIM + N_IDX_HEADS
    w_main = w_in[:, :main_w].astype(BF16)
    w_aux = jnp.concatenate([w_in[:, main_w + small:], w_in[:, main_w:main_w + small],
                             jnp.zeros((d, 128 - small), F32)], axis=1).astype(BF16)
    p_main = norm_matmul(x, g_pre, w_main, BF16, PROJ_TM, PROJ_TN)
    p_aux = norm_matmul(x, g_pre, w_aux, F32, PROJ_TM, 384)
    k_idx_t = p_aux[:, pool_width:pool_width + IDX_HEAD_DIM].astype(BF16).T
    kt2 = jnp.concatenate([k_idx_t, k_idx_t], axis=0)
    mask = indexer_mask(p_main, p_aux, kt2, min(TOPK_MAX, s_len // 4))
    attn = masked_attention(p_main, mask, bias_tiles, n_heads)
    pooled = multiscale_pool(p_aux, pool_w.astype(BF16), pool_scale)
    return mm_norm_res([attn, pooled], w_o.astype(BF16), g_post, x, OUT_TM, OUT_TK)


def _odd_mixer(x, g_pre, g_post, w_in, conv_w, a_log, dt_bias, norm_w, w_o):
    s_len, d = x.shape
    n_kheads = d // 128
    n_vheads = 2 * n_kheads
    qk_w = n_kheads * GDN_DK
    v_w = n_vheads * GDN_DV
    conv_width = 2 * qk_w + v_w
    main_w = conv_width + v_w
    w_main = w_in[:, :main_w].astype(BF16)
    w_ba = jnp.concatenate([w_in[:, main_w:], jnp.zeros((d, 128 - 2 * n_vheads), F32)], axis=1).astype(BF16)
    p_main = norm_matmul(x, g_pre, w_main, BF16, PROJ_TM, PROJ_TN)
    ba = norm_matmul(x, g_pre, w_ba, F32, PROJ_TM, 128)
    qkv = conv_silu_norm(p_main, conv_w, conv_width, qk_w)
    u, w, qd, kdt, intra, dl = gdn_prepare(qkv, ba, a_log, dt_bias, n_kheads, n_vheads)
    o = gdn_scan(u, w, qd, kdt, intra, dl, p_main, norm_w, conv_width // (SCAN_HG * GDN_DV))
    return mm_norm_res([o], w_o.astype(BF16), g_post, x, OUT_TM, OUT_TK)


def _ffn(x, g_pre, g_post, w_gate, w_up, w_down):
    act = norm_swiglu(x, g_pre, w_gate.astype(BF16), w_up.astype(BF16), PROJ_TM, FFN_TN)
    return mm_norm_res([act], w_down.astype(BF16), g_post, x, OUT_TM, OUT_TK)


def kernel(x, rel_bias, even_w_in, even_w_o, pool_w, pool_scale, odd_w_in, conv_w, a_log, dt_bias,
           gdn_norm_w, odd_w_o, ffn_w_gate, ffn_w_up, ffn_w_down, mix_pre_g, mix_post_g, ffn_pre_g, ffn_post_g):
    b, s_len, d = x.shape
    depth = ffn_w_gate.shape[0]
    bias_tiles = _bias_tiles(rel_bias, ATT_T)
    outs = []
    for bi in range(b):
        h = x[bi]
        for layer in range(depth):
            i = layer // 2
            if layer % 2 == 0:
                h = _even_mixer(h, mix_pre_g[layer], mix_post_g[layer], even_w_in[i], even_w_o[i],
                                pool_w[i], pool_scale[i], bias_tiles)
            else:
                h = _odd_mixer(h, mix_pre_g[layer], mix_post_g[layer], odd_w_in[i], conv_w[i], a_log[i],
                               dt_bias[i], gdn_norm_w[i], odd_w_o[i])
            h = _ffn(h, ffn_pre_g[layer], ffn_post_g[layer], ffn_w_gate[layer], ffn_w_up[layer], ffn_w_down[layer])
        outs.append(h)
    return jnp.stack(outs)
```

```python
import functools
import math

import jax
import jax.numpy as jnp
from jax import lax
from jax.experimental import pallas as pl
from jax.experimental.pallas import tpu as pltpu

F32 = jnp.float32
BF16 = jnp.bfloat16
I32 = jnp.int32
I16 = jnp.int16

RMS_EPS = 1e-6
NEG_INF = -1e30
INT_MIN = -2 ** 31
LOG2E = 1.4426950408889634

VMEM_LIMIT_BYTES = 56 * 1024 * 1024

ATTN_HEAD_DIM = 128
N_IDX_HEADS = 16
IDX_HEAD_DIM = 64
TOPK_MAX = 256
N_BUCKETS = 32
MAX_DISTANCE = 128
POOL_WINDOWS = (2, 4, 8, 16)
GDN_DK = 128
GDN_DV = 128
CONV_K = 4
CHUNK = 64


def _params(*sem):
    return pltpu.CompilerParams(dimension_semantics=sem, vmem_limit_bytes=VMEM_LIMIT_BYTES)


def _dot(a, b):
    return jnp.dot(a, b, preferred_element_type=F32)


def _dot_nt(a, b):
    return lax.dot_general(a, b, (((1,), (1,)), ((), ())), preferred_element_type=F32)


def _rms(x, g):
    ms = jnp.mean(x * x, axis=-1, keepdims=True)
    return x * lax.rsqrt(ms + RMS_EPS) * g


def _norm_matmul_kernel(x_ref, g_ref, w_ref, o_ref, hn_ref):
    @pl.when(pl.program_id(1) == 0)
    def _():
        hn_ref[...] = _rms(x_ref[...], g_ref[...]).astype(BF16)

    o_ref[...] = _dot(hn_ref[...], w_ref[...]).astype(o_ref.dtype)


def norm_matmul(x, g, w, out_dtype, tm, tn):
    m, k = x.shape
    n = w.shape[1]
    assert m % tm == 0 and n % tn == 0
    return pl.pallas_call(
        _norm_matmul_kernel,
        grid=(m // tm, n // tn),
        in_specs=[
            pl.BlockSpec((tm, k), lambda i, j: (i, 0)),
            pl.BlockSpec((1, k), lambda i, j: (0, 0)),
            pl.BlockSpec((k, tn), lambda i, j: (0, j)),
        ],
        out_specs=pl.BlockSpec((tm, tn), lambda i, j: (i, j)),
        out_shape=jax.ShapeDtypeStruct((m, n), out_dtype),
        scratch_shapes=[pltpu.VMEM((tm, k), BF16)],
        compiler_params=_params("parallel", "arbitrary"),
        name="norm_matmul",
    )(x, g.reshape(1, k), w)


def _norm_swiglu_kernel(x_ref, g_ref, wg_ref, wu_ref, o_ref, hn_ref):
    @pl.when(pl.program_id(1) == 0)
    def _():
        hn_ref[...] = _rms(x_ref[...], g_ref[...]).astype(BF16)

    hn = hn_ref[...]
    a = _dot(hn, wg_ref[...])
    b = _dot(hn, wu_ref[...])
    o_ref[...] = (a * jax.nn.sigmoid(a) * b).astype(o_ref.dtype)


def norm_swiglu(x, g, wg, wu, tm, tn):
    m, k = x.shape
    n = wg.shape[1]
    assert m % tm == 0 and n % tn == 0
    return pl.pallas_call(
        _norm_swiglu_kernel,
        grid=(m // tm, n // tn),
        in_specs=[
            pl.BlockSpec((tm, k), lambda i, j: (i, 0)),
            pl.BlockSpec((1, k), lambda i, j: (0, 0)),
            pl.BlockSpec((k, tn), lambda i, j: (0, j)),
            pl.BlockSpec((k, tn), lambda i, j: (0, j)),
        ],
        out_specs=pl.BlockSpec((tm, tn), lambda i, j: (i, j)),
        out_shape=jax.ShapeDtypeStruct((m, n), BF16),
        scratch_shapes=[pltpu.VMEM((tm, k), BF16)],
        compiler_params=_params("parallel", "arbitrary"),
        name="norm_swiglu",
    )(x, g.reshape(1, k), wg, wu)


def _mm_norm_res_kernel(*refs, n_seg, nk):
    a_refs = refs[:n_seg]
    w_refs = refs[n_seg:2 * n_seg]
    g_ref, r_ref, o_ref, acc_ref = refs[2 * n_seg:]
    kk = pl.program_id(1)

    @pl.when(kk == 0)
    def _():
        acc_ref[...] = jnp.zeros_like(acc_ref)

    part = _dot(a_refs[0][...], w_refs[0][...])
    for s in range(1, n_seg):
        part = part + _dot(a_refs[s][...], w_refs[s][...])
    acc_ref[...] += part

    @pl.when(kk == nk - 1)
    def _():
        o_ref[...] = r_ref[...] + _rms(acc_ref[...], g_ref[...])


def mm_norm_res(a_list, w, g, res, tm, tk):
    n_seg = len(a_list)
    m, ka = a_list[0].shape
    n = w.shape[1]
    assert w.shape[0] == n_seg * ka and m % tm == 0 and ka % tk == 0
    nk = ka // tk
    a_specs = [pl.BlockSpec((tm, tk), lambda i, k: (i, k)) for _ in range(n_seg)]
    w_specs = [pl.BlockSpec((tk, n), functools.partial(lambda i, k, s: (k + s * nk, 0), s=s))
               for s in range(n_seg)]
    return pl.pallas_call(
        functools.partial(_mm_norm_res_kernel, n_seg=n_seg, nk=nk),
        grid=(m // tm, nk),
        in_specs=a_specs + w_specs + [
            pl.BlockSpec((1, n), lambda i, k: (0, 0)),
            pl.BlockSpec((tm, n), lambda i, k: (i, 0)),
        ],
        out_specs=pl.BlockSpec((tm, n), lambda i, k: (i, 0)),
        out_shape=jax.ShapeDtypeStruct((m, n), F32),
        scratch_shapes=[pltpu.VMEM((tm, n), F32)],
        compiler_params=_params("parallel", "arbitrary"),
        name="mm_norm_res",
    )(*a_list, *([w] * n_seg), g.reshape(1, n), res)


IDX_TQ = 256
IDX_TK = 512
IDX_ROWS = 128


def _indexer_kernel(qi_ref, aux_ref, kt_ref, mask_ref, key_ref, half_ref, qh_ref, wb_ref, *, topk, w_scale):
    tq, tk, tr = IDX_TQ, IDX_TK, IDX_ROWS
    nj = tk // 128
    s_len = mask_ref.shape[1]
    qb = pl.program_id(0)
    n_chunks = ((qb + 1) * tq - 1) // tk + 1

    lane = lax.broadcasted_iota(I32, (tq, 128), 1)
    aux = aux_ref[...]
    for h in range(N_IDX_HEADS):
        pair = qi_ref[:, (h // 2) * 128:(h // 2 + 1) * 128]
        keep = (lane < IDX_HEAD_DIM) if h % 2 == 0 else (lane >= IDX_HEAD_DIM)
        qh_ref[h] = jnp.where(keep, pair, jnp.zeros_like(pair))
        wcol = aux[:, IDX_HEAD_DIM + h:IDX_HEAD_DIM + h + 1] * w_scale
        wb_ref[h] = jnp.broadcast_to(wcol, (tq, 128))

    lane_r = lax.broadcasted_iota(I32, (tr, 128), 1)
    row_r = lax.broadcasted_iota(I32, (tr, 128), 0)

    def score_chunk(c, carry):
        col0 = pl.multiple_of(c * tk, tk)
        kt = kt_ref[:, pl.ds(col0, tk)]
        for r in range(tq // tr):
            rows = slice(r * tr, (r + 1) * tr)
            accs = [jnp.zeros((tr, 128), F32) for _ in range(nj)]
            for h in range(N_IDX_HEADS):
                d = _dot(qh_ref[h, rows], kt)
                wb = wb_ref[h, rows]
                for j in range(nj):
                    accs[j] = accs[j] + wb * jnp.maximum(d[:, j * 128:(j + 1) * 128], 0.0)
            q_pos = qb * tq + r * tr + row_r
            for j in range(nj):
                bits = pltpu.bitcast(accs[j], I32)
                skey = bits ^ ((bits >> 31) & 0x7FFFFFFF)
                skey = jnp.where(col0 + j * 128 + lane_r <= q_pos, skey, INT_MIN)
                cols = pl.ds(pl.multiple_of(col0 + j * 128, 128), 128)
                key_ref[rows, cols] = skey
                half_ref[rows, cols] = (skey >> 16).astype(I16)
        return carry

    lax.fori_loop(0, n_chunks, score_chunk, 0)

    def count_ge(cand):
        cand_b = jnp.broadcast_to(cand, (tq, 128)).astype(I16)

        def body(c, cnt):
            blk = half_ref[:, pl.ds(pl.multiple_of(c * tk, tk), tk)]
            for j in range(nj):
                cnt = cnt + jnp.where(blk[:, j * 128:(j + 1) * 128] >= cand_b, jnp.int16(1), jnp.int16(0))
            return cnt

        cnt = lax.fori_loop(0, n_chunks, body, jnp.zeros((tq, 128), I16))
        return jnp.sum(cnt.astype(I32), axis=-1, keepdims=True)

    def search16():
        def bit_step(i, tau_u):
            cand_u = tau_u | jnp.left_shift(jnp.int32(1), 15 - i)
            cand_s = cand_u ^ 0x8000
            cand_s = jnp.where(cand_s >= 0x8000, cand_s - 0x10000, cand_s)
            return jnp.where(count_ge(cand_s) >= topk, cand_u, tau_u)

        tau_u = lax.fori_loop(0, 16, bit_step, jnp.zeros((tq, 1), I32))
        tau_s = tau_u ^ 0x8000
        return tau_u, jnp.where(tau_s >= 0x8000, tau_s - 0x10000, tau_s)

    _, high = search16()
    base = jnp.broadcast_to(high * 0x10000, (tq, 128))

    def low_chunk(c, carry):
        cols = pl.ds(pl.multiple_of(c * tk, tk), tk)
        blk = key_ref[:, cols]
        parts = [jnp.minimum(jnp.maximum(blk[:, j * 128:(j + 1) * 128], base), base + 0xFFFF) - (base + 0x8000)
                 for j in range(nj)]
        half_ref[:, cols] = jnp.concatenate(parts, axis=-1).astype(I16)
        return carry

    lax.fori_loop(0, n_chunks, low_chunk, 0)
    low_u, _ = search16()
    tau = jnp.maximum(base + jnp.broadcast_to(low_u, (tq, 128)), INT_MIN + 1)

    neg = jnp.full((tq, tk), NEG_INF, F32).astype(BF16)

    def write_chunk(c, carry):
        col0 = pl.multiple_of(c * tk, tk)

        @pl.when(c < n_chunks)
        def _():
            blk = key_ref[:, pl.ds(col0, tk)]
            parts = [jnp.where(blk[:, j * 128:(j + 1) * 128] >= tau, 0.0, NEG_INF) for j in range(nj)]
            mask_ref[:, pl.ds(col0, tk)] = jnp.concatenate(parts, axis=-1).astype(BF16)

        @pl.when(c >= n_chunks)
        def _():
            mask_ref[:, pl.ds(col0, tk)] = neg

        return carry

    lax.fori_loop(0, s_len // tk, write_chunk, 0)


def indexer_mask(p_main, p_aux, kt2, topk):
    s_len = p_main.shape[0]
    tq = IDX_TQ
    w_scale = (N_IDX_HEADS ** -0.5) * (IDX_HEAD_DIM ** -0.5)
    return pl.pallas_call(
        functools.partial(_indexer_kernel, topk=topk, w_scale=w_scale),
        grid=(s_len // tq,),
        in_specs=[
            pl.BlockSpec((tq, 1024), lambda i: (i, 3)),
            pl.BlockSpec((tq, 128), lambda i: (i, 8)),
            pl.BlockSpec((128, s_len), lambda i: (0, 0)),
        ],
        out_specs=pl.BlockSpec((tq, s_len), lambda i: (i, 0)),
        out_shape=jax.ShapeDtypeStruct((s_len, s_len), BF16),
        scratch_shapes=[
            pltpu.VMEM((tq, s_len), I32),
            pltpu.VMEM((tq, s_len), I16),
            pltpu.VMEM((N_IDX_HEADS, tq, 128), BF16),
            pltpu.VMEM((N_IDX_HEADS, tq, 128), F32),
        ],
        compiler_params=_params("parallel"),
        name="indexer_mask",
    )(p_main, p_aux, kt2)


ATT_T = 512
ATT_RC = 32
ATT_SUB = 128


def _attention_kernel(qb_tab, kb_tab, q_ref, k_ref, v_ref, mask_ref, tz_ref, o_ref,
                      qs_ref, mf_ref, p_ref, m_ref, l_ref, acc_ref, *, n_heads, scale):
    step = pl.program_id(0)
    qb = qb_tab[step]
    kb = kb_tab[step]
    dh, t, rc, sub = ATTN_HEAD_DIM, ATT_T, ATT_RC, ATT_SUB
    nsub = t // sub

    @pl.when(kb == 0)
    def _():
        qs_ref[...] = (q_ref[...].astype(F32) * (scale * LOG2E)).astype(BF16)
        m_ref[...] = jnp.full_like(m_ref, NEG_INF)
        l_ref[...] = jnp.zeros_like(l_ref)
        acc_ref[...] = jnp.zeros_like(acc_ref)

    mf_ref[...] = mask_ref[...].astype(F32)
    prev_flag = jnp.where(kb == qb - 1, 1.0, 0.0).astype(F32)

    def run(diag):
        for h in range(n_heads):
            sl = slice(h * dh, (h + 1) * dh)
            s = _dot_nt(qs_ref[:, sl], k_ref[:, sl])
            alphas = []
            for r in range(t // rc):
                rows = slice(r * rc, (r + 1) * rc)
                blk, r0 = divmod(r * rc, sub)
                sr = s[rows] + mf_ref[rows, :]
                parts = [sr[:, b * sub:(b + 1) * sub] for b in range(nsub)]
                if diag:
                    parts[blk] = parts[blk] + tz_ref[0, h, r0:r0 + rc, :]
                    if blk >= 1:
                        parts[blk - 1] = parts[blk - 1] + tz_ref[1, h, r0:r0 + rc, :]
                elif blk == 0:
                    parts[nsub - 1] = parts[nsub - 1] + prev_flag * tz_ref[1, h, r0:r0 + rc, :]
                m_prev = m_ref[h, rows]
                m_new = jnp.maximum(m_prev, jnp.max(functools.reduce(jnp.maximum, parts), axis=-1, keepdims=True))
                alpha = jnp.exp2(m_prev - m_new)
                ps = [jnp.exp2(pb - m_new) for pb in parts]
                l_ref[h, rows] = alpha * l_ref[h, rows] + jnp.sum(functools.reduce(jnp.add, ps), axis=-1, keepdims=True)
                m_ref[h, rows] = m_new
                p_ref[h, rows, :] = jnp.concatenate(ps, axis=-1).astype(BF16)
                alphas.append(alpha)
            pv = _dot(p_ref[h], v_ref[:, sl])
            for r in range(t // rc):
                rows = slice(r * rc, (r + 1) * rc)
                acc_ref[h, rows] = alphas[r] * acc_ref[h, rows] + pv[rows]

    @pl.when(kb != qb)
    def _():
        run(False)

    @pl.when(kb == qb)
    def _():
        run(True)
        for h in range(n_heads):
            o_ref[:, h * dh:(h + 1) * dh] = (acc_ref[h] / l_ref[h]).astype(o_ref.dtype)


def masked_attention(p_main, mask, bias_tables, n_heads):
    s_len = p_main.shape[0]
    t = ATT_T
    nb = s_len // t
    pairs = [(q, k) for q in range(nb) for k in range(q + 1)]
    qb_tab = jnp.array([p[0] for p in pairs], I32)
    kb_tab = jnp.array([p[1] for p in pairs], I32)
    width = n_heads * ATTN_HEAD_DIM
    grid_spec = pltpu.PrefetchScalarGridSpec(
        num_scalar_prefetch=2,
        grid=(len(pairs),),
        in_specs=[
            pl.BlockSpec((t, width), lambda s, qt, kt: (qt[s], 0)),
            pl.BlockSpec((t, width), lambda s, qt, kt: (kt[s], 1)),
            pl.BlockSpec((t, width), lambda s, qt, kt: (kt[s], 2)),
            pl.BlockSpec((t, t), lambda s, qt, kt: (qt[s], kt[s])),
            pl.BlockSpec((2, n_heads, ATT_SUB, ATT_SUB), lambda s, qt, kt: (0, 0, 0, 0)),
        ],
        out_specs=pl.BlockSpec((t, width), lambda s, qt, kt: (qt[s], 0)),
        scratch_shapes=[
            pltpu.VMEM((t, width), BF16),
            pltpu.VMEM((t, t), F32),
            pltpu.VMEM((n_heads, t, t), BF16),
            pltpu.VMEM((n_heads, t, 128), F32),
            pltpu.VMEM((n_heads, t, 128), F32),
            pltpu.VMEM((n_heads, t, ATTN_HEAD_DIM), F32),
        ],
    )
    return pl.pallas_call(
        functools.partial(_attention_kernel, n_heads=n_heads, scale=ATTN_HEAD_DIM ** -0.5),
        grid_spec=grid_spec,
        out_shape=jax.ShapeDtypeStruct((s_len, width), BF16),
        compiler_params=_params("arbitrary"),
        name="masked_attention",
    )(qb_tab, kb_tab, p_main, p_main, p_main, mask, bias_tables)


def _t5_bucket(dist):
    max_exact = N_BUCKETS // 2
    d = jnp.maximum(dist, 0)
    log_ratio = jnp.log(jnp.maximum(d, 1).astype(F32) / max_exact) / math.log(MAX_DISTANCE / max_exact)
    large = jnp.minimum(max_exact + (log_ratio * (N_BUCKETS - max_exact)).astype(I32), N_BUCKETS - 1)
    return jnp.where(d < max_exact, d, large)


def _bias_tables(rel_bias):
    assert MAX_DISTANCE <= ATT_SUB
    i = jnp.arange(ATT_SUB, dtype=I32)[:, None]
    j = jnp.arange(ATT_SUB, dtype=I32)[None, :]
    bucket = _t5_bucket(jnp.stack([i - j, i - j + ATT_SUB]))[:, None]
    rel = rel_bias.astype(F32)
    far = rel[N_BUCKETS - 1]
    tables = jnp.zeros((2, rel_bias.shape[1], ATT_SUB, ATT_SUB), F32)
    for b in range(N_BUCKETS):
        tables = jnp.where(bucket == b, (rel[b] - far)[None, :, None, None], tables)
    return tables * LOG2E


POOL_TR = 512
POOL_G = 256
HALO = 16


def _pool_kernel(x_ref, prev_ref, w_ref, sc_ref, o_ref):
    i = pl.program_id(0)
    g = pl.program_id(1)
    tr = POOL_TR
    cur = x_ref[...]
    prev = jnp.where(i > 0, prev_ref[...], 0.0)
    ext = jnp.concatenate([prev, cur], axis=0)
    sums = []
    run = ext
    for shift in (1, 2, 4, 8):
        run = run + pltpu.roll(run, shift, 0)
        sums.append(run[HALO:])
    win_sum = jnp.where(g == 0, sums[0], jnp.where(g == 1, sums[1], jnp.where(g == 2, sums[2], sums[3])))
    win = jnp.left_shift(jnp.int32(2), g)
    t_pos = i * tr + lax.broadcasted_iota(I32, (tr, 1), 0)
    count = jnp.minimum(t_pos + 1, win).astype(F32)
    pooled = win_sum / count - cur
    o_ref[...] = (_dot(pooled.astype(BF16), w_ref[0]) * sc_ref[...]).astype(o_ref.dtype)


def multiscale_pool(p_aux, pool_w, pool_scale):
    s_len = p_aux.shape[0]
    tr, gw = POOL_TR, POOL_G
    ng = len(POOL_WINDOWS)
    return pl.pallas_call(
        _pool_kernel,
        grid=(s_len // tr, ng),
        in_specs=[
            pl.BlockSpec((tr, gw), lambda i, g: (i, g)),
            pl.BlockSpec((HALO, gw), lambda i, g: (jnp.maximum(i * (tr // HALO) - 1, 0), g)),
            pl.BlockSpec((1, gw, gw), lambda i, g: (g, 0, 0)),
            pl.BlockSpec((1, gw), lambda i, g: (0, g)),
        ],
        out_specs=pl.BlockSpec((tr, gw), lambda i, g: (i, g)),
        out_shape=jax.ShapeDtypeStruct((s_len, ng * gw), BF16),
        compiler_params=_params("parallel", "parallel"),
        name="multiscale_pool",
    )(p_aux, p_aux, pool_w, pool_scale.reshape(1, ng * gw))


CONV_TR = 512
CONV_TC = 512


def _conv_kernel(x_ref, prev_ref, w_ref, o_ref, *, n_norm_blocks):
    i = pl.program_id(0)
    j = pl.program_id(1)
    cur = x_ref[...].astype(F32)
    prev = jnp.where(i > 0, prev_ref[...].astype(F32), 0.0)
    ext = jnp.concatenate([prev, cur], axis=0)
    w = w_ref[...]
    y = ext * w[CONV_K - 1:CONV_K, :]
    for tap in range(CONV_K - 1):
        y = y + pltpu.roll(ext, CONV_K - 1 - tap, 0) * w[tap:tap + 1, :]
    y = y[HALO:]
    y = y * jax.nn.sigmoid(y)

    @pl.when(j < n_norm_blocks)
    def _():
        for h in range(CONV_TC // 128):
            yh = y[:, h * 128:(h + 1) * 128]
            ss = jnp.sum(yh * yh, axis=-1, keepdims=True)
            o_ref[:, h * 128:(h + 1) * 128] = yh * lax.rsqrt(ss + RMS_EPS)

    @pl.when(j >= n_norm_blocks)
    def _():
        o_ref[...] = y


def conv_silu_norm(p_main, conv_w, conv_width, qk_width):
    s_len = p_main.shape[0]
    tr, tc = CONV_TR, CONV_TC
    return pl.pallas_call(
        functools.partial(_conv_kernel, n_norm_blocks=2 * qk_width // tc),
        grid=(s_len // tr, conv_width // tc),
        in_specs=[
            pl.BlockSpec((tr, tc), lambda i, j: (i, j)),
            pl.BlockSpec((HALO, tc), lambda i, j: (jnp.maximum(i * (tr // HALO) - 1, 0), j)),
            pl.BlockSpec((CONV_K, tc), lambda i, j: (0, j)),
        ],
        out_specs=pl.BlockSpec((tr, tc), lambda i, j: (i, j)),
        out_shape=jax.ShapeDtypeStruct((s_len, conv_width), F32),
        compiler_params=_params("parallel", "parallel"),
        name="conv_silu_norm",
    )(p_main, p_main, conv_w)


def _split3(x):
    hi = x.astype(BF16)
    r1 = x - hi.astype(F32)
    mid = r1.astype(BF16)
    lo = (r1 - mid.astype(F32)).astype(BF16)
    return hi, mid, lo


def _dot_exact_lhs(a_bf16, x):
    hi, mid, lo = _split3(x)
    return _dot(a_bf16, lo) + _dot(a_bf16, mid) + _dot(a_bf16, hi)


def _dot_exact_rhs(x, b_bf16):
    hi, mid, lo = _split3(x)
    return _dot(lo, b_bf16) + _dot(mid, b_bf16) + _dot(hi, b_bf16)


def _softplus(x):
    return jnp.maximum(x, 0.0) + jnp.log(1.0 + jnp.exp(-jnp.abs(x)))


def _gdn_prep_kernel(q_ref, k_ref, v_ref, ba_ref, alog_ref, dtb_ref,
                     u_ref, w_ref, qd_ref, kdt_ref, intra_ref, dl_ref, *, n_vheads):
    c = CHUNK
    dk = GDN_DK
    assert 2 * c == 128
    scale = dk ** -0.5
    ba = ba_ref[...]
    beta_all = jax.nn.sigmoid(ba)
    g_all = -jnp.exp(alog_ref[...]) * _softplus(ba + dtb_ref[...])
    row = lax.broadcasted_iota(I32, (c, c), 0)
    col = lax.broadcasted_iota(I32, (c, c), 1)
    lower = jnp.where(row >= col, 1.0, 0.0).astype(BF16)
    upper = jnp.where(row <= col, 1.0, 0.0).astype(BF16)
    gc_all = _dot_exact_lhs(lower, g_all)
    gct_all = _dot_exact_rhs(g_all.T, upper)

    row_w = lax.broadcasted_iota(I32, (c, 2 * c), 0)
    lane_w = lax.broadcasted_iota(I32, (c, 2 * c), 1)
    first = lane_w < c
    col_w = jnp.where(first, lane_w, lane_w - c)
    tril_w = row_w >= col_w
    strict_w = row_w > col_w
    eye_w = jnp.where(row_w == col_w, 1.0, 0.0).astype(F32)

    def block_diag(wide16):
        zero = jnp.zeros_like(wide16)
        return jnp.concatenate([jnp.where(first, wide16, zero), jnp.where(first, zero, wide16)], axis=0)

    pairs = range(n_vheads // 2)
    pws, t_invs, rhss = [], [], []
    for p in pairs:
        kh = k_ref[:, p * dk:(p + 1) * dk]
        qh = q_ref[:, p * dk:(p + 1) * dk] * scale
        kh16 = kh.astype(BF16)
        kk16 = jnp.concatenate([kh16, kh16], axis=0)
        gram = _dot_nt(kh16, kk16)
        qk = _dot_nt(qh.astype(BF16), kk16)
        hv = (2 * p, 2 * p + 1)
        beta = [beta_all[:, h:h + 1] for h in hv]
        gcol = [gc_all[:, n_vheads + h:n_vheads + h + 1] for h in hv]
        grow_w = jnp.concatenate([gct_all[n_vheads + h:n_vheads + h + 1, :] for h in hv], axis=1)
        gcol_w = jnp.where(first, gcol[0], gcol[1])
        decay_w = jnp.where(tril_w, jnp.exp(jnp.where(tril_w, gcol_w - grow_w, 0.0)), 0.0)
        a_w = jnp.where(strict_w, -(jnp.where(first, beta[0], beta[1]) * gram * decay_w), 0.0)
        intra_ref[0, p] = jnp.where(tril_w, qk * decay_w, 0.0).astype(BF16)
        rhs = []
        for i, h in enumerate(hv):
            eg = jnp.exp(gcol[i])
            g_last = gcol[i][c - 1:c, :]
            sl = slice(h * GDN_DV, (h + 1) * GDN_DV)
            qd_ref[:, sl] = (qh * eg).astype(BF16)
            kdt_ref[0, h] = (kh * jnp.exp(g_last - gcol[i])).T.astype(BF16)
            dl_ref[0, h] = jnp.broadcast_to(jnp.exp(g_last), (1, 128))
            kb = kh * beta[i]
            rhs.append(jnp.concatenate([v_ref[:, sl] * beta[i], kb * eg], axis=-1).astype(BF16))
        pws.append(a_w)
        t_invs.append(eye_w + a_w)
        rhss.append(jnp.concatenate(rhs, axis=0))
    for _ in range(int(math.log2(c)) - 1):
        pws = [_dot(pw.astype(BF16), block_diag(pw.astype(BF16))) for pw in pws]
        t_invs = [t + _dot(t.astype(BF16), block_diag(pw.astype(BF16))) for t, pw in zip(t_invs, pws)]
    for p, t_inv, rhs in zip(pairs, t_invs, rhss):
        t16 = t_inv.astype(BF16)
        zero = jnp.zeros_like(t16)
        for h, lhs in ((2 * p, jnp.where(first, t16, zero)), (2 * p + 1, jnp.where(first, zero, t16))):
            sol = _dot(lhs, rhs)
            sl = slice(h * GDN_DV, (h + 1) * GDN_DV)
            u_ref[:, sl] = sol[:, :GDN_DV]
            w_ref[:, sl] = sol[:, GDN_DV:].astype(BF16)


def gdn_prepare(qkv, ba, a_log, dt_bias, n_kheads, n_vheads):
    s_len = qkv.shape[0]
    c = CHUNK
    n = s_len // c
    qk_w = n_kheads * GDN_DK
    v_w = n_vheads * GDN_DV
    assert 2 * n_vheads <= 128 and n_vheads == 2 * n_kheads
    pad = jnp.zeros((128 - 2 * n_vheads,), F32)
    alog_row = jnp.concatenate([jnp.zeros((n_vheads,), F32), a_log, pad]).reshape(1, 128)
    dtb_row = jnp.concatenate([jnp.zeros((n_vheads,), F32), dt_bias, pad]).reshape(1, 128)
    qb, vb = qk_w // qk_w, (2 * qk_w) // v_w
    return pl.pallas_call(
        functools.partial(_gdn_prep_kernel, n_vheads=n_vheads),
        grid=(n,),
        in_specs=[
            pl.BlockSpec((c, qk_w), lambda i: (i, 0)),
            pl.BlockSpec((c, qk_w), lambda i: (i, qb)),
            pl.BlockSpec((c, v_w), lambda i: (i, vb)),
            pl.BlockSpec((c, 128), lambda i: (i, 0)),
            pl.BlockSpec((1, 128), lambda i: (0, 0)),
            pl.BlockSpec((1, 128), lambda i: (0, 0)),
        ],
        out_specs=[
            pl.BlockSpec((c, v_w), lambda i: (i, 0)),
            pl.BlockSpec((c, v_w), lambda i: (i, 0)),
            pl.BlockSpec((c, v_w), lambda i: (i, 0)),
            pl.BlockSpec((1, n_vheads, GDN_DK, c), lambda i: (i, 0, 0, 0)),
            pl.BlockSpec((1, n_vheads // 2, c, 2 * c), lambda i: (i, 0, 0, 0)),
            pl.BlockSpec((1, n_vheads, 1, 128), lambda i: (i, 0, 0, 0)),
        ],
        out_shape=[
            jax.ShapeDtypeStruct((s_len, v_w), F32),
            jax.ShapeDtypeStruct((s_len, v_w), BF16),
            jax.ShapeDtypeStruct((s_len, v_w), BF16),
            jax.ShapeDtypeStruct((n, n_vheads, GDN_DK, c), BF16),
            jax.ShapeDtypeStruct((n, n_vheads // 2, c, 2 * c), BF16),
            jax.ShapeDtypeStruct((n, n_vheads, 1, 128), F32),
        ],
        compiler_params=_params("parallel"),
        name="gdn_prepare",
    )(qkv, qkv, qkv, ba, alog_row, dtb_row)


SCAN_HG = 16


def _gdn_scan_kernel(u_ref, w_ref, qd_ref, kdt_ref, intra_ref, dl_ref, z_ref, nw_ref, o_ref, state_ref):
    @pl.when(pl.program_id(1) == 0)
    def _():
        state_ref[...] = jnp.zeros_like(state_ref)

    nw = nw_ref[...]
    heads = range(SCAN_HG)
    sls = [slice(j * GDN_DV, (j + 1) * GDN_DV) for j in heads]
    first = lax.broadcasted_iota(I32, (CHUNK, 2 * CHUNK), 1) < CHUNK
    states = [state_ref[j] for j in heads]
    s16 = [s.astype(BF16) for s in states]
    ws = [_dot(w_ref[:, sls[j]], s16[j]) for j in heads]
    qs = [_dot(qd_ref[:, sls[j]], s16[j]) for j in heads]
    v16 = [(u_ref[:, sls[j]] - ws[j]).astype(BF16) for j in heads]
    upd = [_dot(kdt_ref[0, j], v16[j]) for j in heads]
    outs = []
    for p in range(SCAN_HG // 2):
        pair = intra_ref[0, p]
        zero = jnp.zeros_like(pair)
        v_pair = jnp.concatenate([v16[2 * p], v16[2 * p + 1]], axis=0)
        outs.append(qs[2 * p] + _dot(jnp.where(first, pair, zero), v_pair))
        outs.append(qs[2 * p + 1] + _dot(jnp.where(first, zero, pair), v_pair))
    for j in heads:
        state_ref[j] = states[j] * dl_ref[0, j] + upd[j]
    for j in heads:
        z = z_ref[:, sls[j]].astype(F32)
        o_ref[:, sls[j]] = (_rms(outs[j], nw) * (z * jax.nn.sigmoid(z))).astype(o_ref.dtype)


def gdn_scan(u, w, qd, kdt, intra, dl, p_main, norm_w, z_col_block):
    s_len, v_w = u.shape
    c = CHUNK
    n = s_len // c
    hg = SCAN_HG
    gw = hg * GDN_DV
    return pl.pallas_call(
        _gdn_scan_kernel,
        grid=(v_w // gw, n),
        in_specs=[
            pl.BlockSpec((c, gw), lambda g, i: (i, g)),
            pl.BlockSpec((c, gw), lambda g, i: (i, g)),
            pl.BlockSpec((c, gw), lambda g, i: (i, g)),
            pl.BlockSpec((1, hg, GDN_DK, c), lambda g, i: (i, g, 0, 0)),
            pl.BlockSpec((1, hg // 2, c, 2 * c), lambda g, i: (i, g, 0, 0)),
            pl.BlockSpec((1, hg, 1, 128), lambda g, i: (i, g, 0, 0)),
            pl.BlockSpec((c, gw), lambda g, i: (i, z_col_block + g)),
            pl.BlockSpec((1, GDN_DV), lambda g, i: (0, 0)),
        ],
        out_specs=pl.BlockSpec((c, gw), lambda g, i: (i, g)),
        out_shape=jax.ShapeDtypeStruct((s_len, v_w), BF16),
        scratch_shapes=[pltpu.VMEM((hg, GDN_DK, GDN_DV), F32)],
        compiler_params=_params("parallel", "arbitrary"),
        name="gdn_scan",
    )(u, w, qd, kdt, intra, dl, p_main, norm_w.reshape(1, GDN_DV))


PROJ_TM = 1024
PROJ_TN = 1024
FFN_TN = 512
OUT_TM = 512
OUT_TK_MAX = 1408


def _out_tk(k):
    return max(c for c in range(128, OUT_TK_MAX + 1, 128) if k % c == 0)


def _even_mixer(x, g_pre, g_post, w_in, w_o, pool_w, pool_scale, bias_tables):
    s_len, d = x.shape
    attn_w = d // 2
    n_heads = attn_w // ATTN_HEAD_DIM
    idx_w = N_IDX_HEADS * IDX_HEAD_DIM
    pool_width = len(POOL_WINDOWS) * POOL_G
    main_w = 3 * attn_w + idx_w
    assert main_w == 4096 and pool_width == 1024
    small = IDX_HEAD_DIM + N_IDX_HEADS
    w_main = w_in[:, :main_w].astype(BF16)
    w_aux = jnp.concatenate([w_in[:, main_w + small:], w_in[:, main_w:main_w + small],
                             jnp.zeros((d, 128 - small), F32)], axis=1).astype(BF16)
    p_main = norm_matmul(x, g_pre, w_main, BF16, PROJ_TM, PROJ_TN)
    p_aux = norm_matmul(x, g_pre, w_aux, F32, PROJ_TM, 384)
    k_idx_t = p_aux[:, pool_width:pool_width + IDX_HEAD_DIM].astype(BF16).T
    kt2 = jnp.concatenate([k_idx_t, k_idx_t], axis=0)
    mask = indexer_mask(p_main, p_aux, kt2, min(TOPK_MAX, s_len // 4))
    attn = masked_attention(p_main, mask, bias_tables, n_heads)
    pooled = multiscale_pool(p_aux, pool_w.astype(BF16), pool_scale)
    return mm_norm_res([attn, pooled], w_o.astype(BF16), g_post, x, OUT_TM, _out_tk(attn.shape[1]))


def _odd_mixer(x, g_pre, g_post, w_in, conv_w, a_log, dt_bias, norm_w, w_o):
    s_len, d = x.shape
    n_kheads = d // 128
    n_vheads = 2 * n_kheads
    qk_w = n_kheads * GDN_DK
    v_w = n_vheads * GDN_DV
    conv_width = 2 * qk_w + v_w
    main_w = conv_width + v_w
    w_main = w_in[:, :main_w].astype(BF16)
    w_ba = jnp.concatenate([w_in[:, main_w:], jnp.zeros((d, 128 - 2 * n_vheads), F32)], axis=1).astype(BF16)
    p_main = norm_matmul(x, g_pre, w_main, BF16, PROJ_TM, PROJ_TN)
    ba = norm_matmul(x, g_pre, w_ba, F32, PROJ_TM, 128)
    qkv = conv_silu_norm(p_main, conv_w, conv_width, qk_w)
    u, w, qd, kdt, intra, dl = gdn_prepare(qkv, ba, a_log, dt_bias, n_kheads, n_vheads)
    o = gdn_scan(u, w, qd, kdt, intra, dl, p_main, norm_w, conv_width // (SCAN_HG * GDN_DV))
    return mm_norm_res([o], w_o.astype(BF16), g_post, x, OUT_TM, _out_tk(v_w))


def _ffn(x, g_pre, g_post, w_gate, w_up, w_down):
    act = norm_swiglu(x, g_pre, w_gate.astype(BF16), w_up.astype(BF16), PROJ_TM, FFN_TN)
    return mm_norm_res([act], w_down.astype(BF16), g_post, x, OUT_TM, _out_tk(act.shape[1]))


def kernel(x, rel_bias, even_w_in, even_w_o, pool_w, pool_scale, odd_w_in, conv_w, a_log, dt_bias,
           gdn_norm_w, odd_w_o, ffn_w_gate, ffn_w_up, ffn_w_down, mix_pre_g, mix_post_g, ffn_pre_g, ffn_post_g):
    b, s_len, d = x.shape
    depth = ffn_w_gate.shape[0]
    bias_tables = _bias_tables(rel_bias)
    outs = []
    for bi in range(b):
        h = x[bi]
        for layer in range(depth):
            i = layer // 2
            if layer % 2 == 0:
                h = _even_mixer(h, mix_pre_g[layer], mix_post_g[layer], even_w_in[i], even_w_o[i],
                                pool_w[i], pool_scale[i], bias_tables)
            else:
                h = _odd_mixer(h, mix_pre_g[layer], mix_post_g[layer], odd_w_in[i], conv_w[i], a_log[i],
                               dt_bias[i], gdn_norm_w[i], odd_w_o[i])
            h = _ffn(h, ffn_pre_g[layer], ffn_post_g[layer], ffn_w_gate[layer], ffn_w_up[layer], ffn_w_down[layer])
        outs.append(h)
    return jnp.stack(outs)
```

```python
import functools
import math

import jax
import jax.numpy as jnp
from jax import lax
from jax.experimental import pallas as pl
from jax.experimental.pallas import tpu as pltpu

F32 = jnp.float32
BF16 = jnp.bfloat16
I32 = jnp.int32

RMS_EPS = 1e-6
NEG_INF = -1e30
INT_MIN = -2 ** 31
LOG2E = 1.4426950408889634

VMEM_LIMIT_BYTES = 56 * 1024 * 1024

ATTN_HEAD_DIM = 128
N_IDX_HEADS = 16
IDX_HEAD_DIM = 64
TOPK_MAX = 256
N_BUCKETS = 32
MAX_DISTANCE = 128
POOL_WINDOWS = (2, 4, 8, 16)
GDN_DK = 128
GDN_DV = 128
CONV_K = 4
CHUNK = 64


def _params(*sem):
    return pltpu.CompilerParams(dimension_semantics=sem, vmem_limit_bytes=VMEM_LIMIT_BYTES)


def _dot(a, b):
    return jnp.dot(a, b, preferred_element_type=F32)


def _dot_nt(a, b):
    return lax.dot_general(a, b, (((1,), (1,)), ((), ())), preferred_element_type=F32)


def _rms(x, g):
    ms = jnp.mean(x * x, axis=-1, keepdims=True)
    return x * lax.rsqrt(ms + RMS_EPS) * g


def _norm_matmul_kernel(x_ref, g_ref, w_ref, o_ref, hn_ref):
    @pl.when(pl.program_id(1) == 0)
    def _():
        hn_ref[...] = _rms(x_ref[...], g_ref[...]).astype(BF16)

    o_ref[...] = _dot(hn_ref[...], w_ref[...]).astype(o_ref.dtype)


def norm_matmul(x, g, w, layer, n, out_dtype, tm, tn):
    m, k = x.shape
    assert m % tm == 0 and n % tn == 0 and n <= w.shape[2]
    return pl.pallas_call(
        _norm_matmul_kernel,
        grid=(m // tm, n // tn),
        in_specs=[
            pl.BlockSpec((tm, k), lambda i, j: (i, 0)),
            pl.BlockSpec((1, k), lambda i, j: (0, 0)),
            pl.BlockSpec((None, k, tn), lambda i, j: (layer, 0, j)),
        ],
        out_specs=pl.BlockSpec((tm, tn), lambda i, j: (i, j)),
        out_shape=jax.ShapeDtypeStruct((m, n), out_dtype),
        scratch_shapes=[pltpu.VMEM((tm, k), BF16)],
        compiler_params=_params("parallel", "arbitrary"),
        name="norm_matmul",
    )(x, g.reshape(1, k), w)


def _norm_swiglu_kernel(x_ref, g_ref, wg_ref, wu_ref, o_ref, hn_ref):
    @pl.when(pl.program_id(1) == 0)
    def _():
        hn_ref[...] = _rms(x_ref[...], g_ref[...]).astype(BF16)

    hn = hn_ref[...]
    a = _dot(hn, wg_ref[...])
    b = _dot(hn, wu_ref[...])
    o_ref[...] = (a * jax.nn.sigmoid(a) * b).astype(o_ref.dtype)


def norm_swiglu(x, g, wg, wu, layer, tm, tn):
    m, k = x.shape
    n = wg.shape[2]
    assert m % tm == 0 and n % tn == 0
    return pl.pallas_call(
        _norm_swiglu_kernel,
        grid=(m // tm, n // tn),
        in_specs=[
            pl.BlockSpec((tm, k), lambda i, j: (i, 0)),
            pl.BlockSpec((1, k), lambda i, j: (0, 0)),
            pl.BlockSpec((None, k, tn), lambda i, j: (layer, 0, j)),
            pl.BlockSpec((None, k, tn), lambda i, j: (layer, 0, j)),
        ],
        out_specs=pl.BlockSpec((tm, tn), lambda i, j: (i, j)),
        out_shape=jax.ShapeDtypeStruct((m, n), BF16),
        scratch_shapes=[pltpu.VMEM((tm, k), BF16)],
        compiler_params=_params("parallel", "arbitrary"),
        name="norm_swiglu",
    )(x, g.reshape(1, k), wg, wu)


def _mm_norm_res_kernel(*refs, n_seg, nk):
    a_refs = refs[:n_seg]
    w_refs = refs[n_seg:2 * n_seg]
    g_ref, r_ref, o_ref, acc_ref = refs[2 * n_seg:]
    kk = pl.program_id(1)

    @pl.when(kk == 0)
    def _():
        acc_ref[...] = jnp.zeros_like(acc_ref)

    part = _dot(a_refs[0][...], w_refs[0][...])
    for s in range(1, n_seg):
        part = part + _dot(a_refs[s][...], w_refs[s][...])
    acc_ref[...] += part

    @pl.when(kk == nk - 1)
    def _():
        o_ref[...] = r_ref[...] + _rms(acc_ref[...], g_ref[...])


def mm_norm_res(a_list, w, layer, g, res, tm, tk):
    n_seg = len(a_list)
    m, ka = a_list[0].shape
    n = w.shape[2]
    assert w.shape[1] == n_seg * ka and m % tm == 0 and ka % tk == 0
    nk = ka // tk
    a_specs = [pl.BlockSpec((tm, tk), lambda i, k: (i, k)) for _ in range(n_seg)]
    w_specs = [pl.BlockSpec((None, tk, n), functools.partial(lambda i, k, s: (layer, k + s * nk, 0), s=s))
               for s in range(n_seg)]
    return pl.pallas_call(
        functools.partial(_mm_norm_res_kernel, n_seg=n_seg, nk=nk),
        grid=(m // tm, nk),
        in_specs=a_specs + w_specs + [
            pl.BlockSpec((1, n), lambda i, k: (0, 0)),
            pl.BlockSpec((tm, n), lambda i, k: (i, 0)),
        ],
        out_specs=pl.BlockSpec((tm, n), lambda i, k: (i, 0)),
        out_shape=jax.ShapeDtypeStruct((m, n), F32),
        scratch_shapes=[pltpu.VMEM((tm, n), F32)],
        compiler_params=_params("parallel", "arbitrary"),
        name="mm_norm_res",
    )(*a_list, *([w] * n_seg), g.reshape(1, n), res)


CAST_ROWS = 128


def _cast_kernel(x_ref, o_ref):
    o_ref[...] = x_ref[...].astype(o_ref.dtype)


def cast_bf16(w):
    layers, k, n = w.shape
    assert k % CAST_ROWS == 0
    return pl.pallas_call(
        _cast_kernel,
        grid=(layers, k // CAST_ROWS),
        in_specs=[pl.BlockSpec((None, CAST_ROWS, n), lambda l, i: (l, i, 0))],
        out_specs=pl.BlockSpec((None, CAST_ROWS, n), lambda l, i: (l, i, 0)),
        out_shape=jax.ShapeDtypeStruct(w.shape, BF16),
        compiler_params=_params("parallel", "parallel"),
        name="cast_bf16",
    )(w)


IDX_TQ = 128
IDX_TK = 512
IDX_ROWS = 128


def _indexer_kernel(qi_ref, aux_ref, kt_ref, mask_ref, key_ref, qh_ref, wb_ref, *, topk, w_scale):
    tq, tk, tr = IDX_TQ, IDX_TK, IDX_ROWS
    nj = tk // 128
    s_len = mask_ref.shape[1]
    qb = pl.program_id(0)
    n_chunks = ((qb + 1) * tq - 1) // tk + 1

    lane = lax.broadcasted_iota(I32, (tq, 128), 1)
    aux = aux_ref[...]
    for h in range(N_IDX_HEADS):
        pair = qi_ref[:, (h // 2) * 128:(h // 2 + 1) * 128]
        keep = (lane < IDX_HEAD_DIM) if h % 2 == 0 else (lane >= IDX_HEAD_DIM)
        qh_ref[h] = jnp.where(keep, pair, jnp.zeros_like(pair))
        wcol = aux[:, IDX_HEAD_DIM + h:IDX_HEAD_DIM + h + 1] * w_scale
        wb_ref[h] = jnp.broadcast_to(wcol, (tq, 128))

    lane_r = lax.broadcasted_iota(I32, (tr, 128), 1)
    row_r = lax.broadcasted_iota(I32, (tr, 128), 0)

    def score_chunk(c, tops):
        col0 = pl.multiple_of(c * tk, tk)
        kt = kt_ref[:, pl.ds(col0, tk)]
        new_tops = []
        for r in range(tq // tr):
            rows = slice(r * tr, (r + 1) * tr)
            accs = [jnp.zeros((tr, 128), F32) for _ in range(nj)]
            for h in range(N_IDX_HEADS):
                d = _dot(qh_ref[h, rows], kt)
                wb = wb_ref[h, rows]
                for j in range(nj):
                    accs[j] = accs[j] + wb * jnp.maximum(d[:, j * 128:(j + 1) * 128], 0.0)
            q_pos = qb * tq + r * tr + row_r
            m1, m2 = tops[r]
            for j in range(nj):
                bits = pltpu.bitcast(accs[j], I32)
                skey = bits ^ ((bits >> 31) & 0x7FFFFFFF)
                skey = jnp.where(col0 + j * 128 + lane_r <= q_pos, skey, INT_MIN)
                key_ref[rows, pl.ds(pl.multiple_of(col0 + j * 128, 128), 128)] = skey
                m2 = jnp.maximum(m2, jnp.minimum(m1, skey))
                m1 = jnp.maximum(m1, skey)
            new_tops.append((m1, m2))
        return tuple(new_tops)

    lowest = jnp.full((tr, 128), INT_MIN, I32)
    tops = lax.fori_loop(0, n_chunks, score_chunk, tuple((lowest, lowest) for _ in range(tq // tr)))
    m1 = jnp.concatenate([t[0] for t in tops], axis=0)
    m2 = jnp.concatenate([t[1] for t in tops], axis=0)
    assert topk <= 2 * 128
    hi_u = jnp.max(m1, axis=-1, keepdims=True) ^ INT_MIN
    lo_u = jnp.min(m2, axis=-1, keepdims=True) ^ INT_MIN
    n_bits = jnp.max(32 - lax.clz(hi_u ^ lo_u))

    def count_ge(cand):
        cand_b = jnp.broadcast_to(cand, (tq, 128))

        def body(c, cnt):
            blk = key_ref[:, pl.ds(pl.multiple_of(c * tk, tk), tk)]
            for j in range(nj):
                cnt = cnt + jnp.where(blk[:, j * 128:(j + 1) * 128] >= cand_b, 1, 0)
            return cnt

        cnt = lax.fori_loop(0, n_chunks, body, jnp.zeros((tq, 128), I32))
        return jnp.sum(cnt, axis=-1, keepdims=True)

    def bit_step(i, tau_u):
        cand_u = tau_u | jnp.left_shift(jnp.int32(1), n_bits - 1 - i)
        return jnp.where(count_ge(cand_u ^ INT_MIN) >= topk, cand_u, tau_u)

    fixed = jnp.where(n_bits >= 32, 0, jnp.left_shift(jnp.int32(-1), jnp.minimum(n_bits, 31)))
    tau_u = lax.fori_loop(0, n_bits, bit_step, lo_u & fixed)
    tau = jnp.broadcast_to(jnp.maximum(tau_u ^ INT_MIN, INT_MIN + 1), (tq, 128))

    neg = jnp.full((tq, tk), NEG_INF, F32).astype(BF16)

    def write_chunk(c, carry):
        col0 = pl.multiple_of(c * tk, tk)

        @pl.when(c < n_chunks)
        def _():
            blk = key_ref[:, pl.ds(col0, tk)]
            parts = [jnp.where(blk[:, j * 128:(j + 1) * 128] >= tau, 0.0, NEG_INF) for j in range(nj)]
            mask_ref[:, pl.ds(col0, tk)] = jnp.concatenate(parts, axis=-1).astype(BF16)

        @pl.when(c >= n_chunks)
        def _():
            mask_ref[:, pl.ds(col0, tk)] = neg

        return carry

    lax.fori_loop(0, s_len // tk, write_chunk, 0)


def indexer_mask(p_main, p_aux, kt2, topk):
    s_len = p_main.shape[0]
    tq = IDX_TQ
    w_scale = (N_IDX_HEADS ** -0.5) * (IDX_HEAD_DIM ** -0.5)
    return pl.pallas_call(
        functools.partial(_indexer_kernel, topk=topk, w_scale=w_scale),
        grid=(s_len // tq,),
        in_specs=[
            pl.BlockSpec((tq, 1024), lambda i: (i, 3)),
            pl.BlockSpec((tq, 128), lambda i: (i, 8)),
            pl.BlockSpec((128, s_len), lambda i: (0, 0)),
        ],
        out_specs=pl.BlockSpec((tq, s_len), lambda i: (i, 0)),
        out_shape=jax.ShapeDtypeStruct((s_len, s_len), BF16),
        scratch_shapes=[
            pltpu.VMEM((tq, s_len), I32),
            pltpu.VMEM((N_IDX_HEADS, tq, 128), BF16),
            pltpu.VMEM((N_IDX_HEADS, tq, 128), F32),
        ],
        compiler_params=_params("parallel"),
        name="indexer_mask",
    )(p_main, p_aux, kt2)


ATT_T = 512
ATT_RC = 32
ATT_SUB = 128


def _attention_kernel(qb_tab, kb_tab, q_ref, k_ref, v_ref, mask_ref, tz_ref, o_ref,
                      qs_ref, mf_ref, p_ref, m_ref, l_ref, acc_ref, *, n_heads, scale):
    step = pl.program_id(0)
    qb = qb_tab[step]
    kb = kb_tab[step]
    dh, t, rc, sub = ATTN_HEAD_DIM, ATT_T, ATT_RC, ATT_SUB
    nsub = t // sub

    @pl.when(kb == 0)
    def _():
        qs_ref[...] = (q_ref[...].astype(F32) * (scale * LOG2E)).astype(BF16)
        m_ref[...] = jnp.full_like(m_ref, NEG_INF)
        l_ref[...] = jnp.zeros_like(l_ref)
        acc_ref[...] = jnp.zeros_like(acc_ref)

    mf_ref[...] = mask_ref[...].astype(F32)
    prev_flag = jnp.where(kb == qb - 1, 1.0, 0.0).astype(F32)

    def run(diag):
        for h in range(n_heads):
            sl = slice(h * dh, (h + 1) * dh)
            s = _dot_nt(qs_ref[:, sl], k_ref[:, sl])
            alphas = []
            for r in range(t // rc):
                rows = slice(r * rc, (r + 1) * rc)
                blk, r0 = divmod(r * rc, sub)
                sr = s[rows] + mf_ref[rows, :]
                parts = [sr[:, b * sub:(b + 1) * sub] for b in range(nsub)]
                if diag:
                    parts[blk] = parts[blk] + tz_ref[0, h, r0:r0 + rc, :]
                    if blk >= 1:
                        parts[blk - 1] = parts[blk - 1] + tz_ref[1, h, r0:r0 + rc, :]
                elif blk == 0:
                    parts[nsub - 1] = parts[nsub - 1] + prev_flag * tz_ref[1, h, r0:r0 + rc, :]
                m_prev = m_ref[h, rows]
                m_new = jnp.maximum(m_prev, jnp.max(functools.reduce(jnp.maximum, parts), axis=-1, keepdims=True))
                alpha = jnp.exp2(m_prev - m_new)
                ps = [jnp.exp2(pb - m_new) for pb in parts]
                l_ref[h, rows] = alpha * l_ref[h, rows] + jnp.sum(functools.reduce(jnp.add, ps), axis=-1, keepdims=True)
                m_ref[h, rows] = m_new
                p_ref[h, rows, :] = jnp.concatenate(ps, axis=-1).astype(BF16)
                alphas.append(alpha)
            pv = _dot(p_ref[h], v_ref[:, sl])
            for r in range(t // rc):
                rows = slice(r * rc, (r + 1) * rc)
                acc_ref[h, rows] = alphas[r] * acc_ref[h, rows] + pv[rows]

    @pl.when(kb != qb)
    def _():
        run(False)

    @pl.when(kb == qb)
    def _():
        run(True)
        for h in range(n_heads):
            o_ref[:, h * dh:(h + 1) * dh] = (acc_ref[h] / l_ref[h]).astype(o_ref.dtype)


def masked_attention(p_main, mask, bias_tables, n_heads):
    s_len = p_main.shape[0]
    t = ATT_T
    nb = s_len // t
    pairs = [(q, k) for q in range(nb) for k in range(q + 1)]
    qb_tab = jnp.array([p[0] for p in pairs], I32)
    kb_tab = jnp.array([p[1] for p in pairs], I32)
    width = n_heads * ATTN_HEAD_DIM
    grid_spec = pltpu.PrefetchScalarGridSpec(
        num_scalar_prefetch=2,
        grid=(len(pairs),),
        in_specs=[
            pl.BlockSpec((t, width), lambda s, qt, kt: (qt[s], 0)),
            pl.BlockSpec((t, width), lambda s, qt, kt: (kt[s], 1)),
            pl.BlockSpec((t, width), lambda s, qt, kt: (kt[s], 2)),
            pl.BlockSpec((t, t), lambda s, qt, kt: (qt[s], kt[s])),
            pl.BlockSpec((2, n_heads, ATT_SUB, ATT_SUB), lambda s, qt, kt: (0, 0, 0, 0)),
        ],
        out_specs=pl.BlockSpec((t, width), lambda s, qt, kt: (qt[s], 0)),
        scratch_shapes=[
            pltpu.VMEM((t, width), BF16),
            pltpu.VMEM((t, t), F32),
            pltpu.VMEM((n_heads, t, t), BF16),
            pltpu.VMEM((n_heads, t, 128), F32),
            pltpu.VMEM((n_heads, t, 128), F32),
            pltpu.VMEM((n_heads, t, ATTN_HEAD_DIM), F32),
        ],
    )
    return pl.pallas_call(
        functools.partial(_attention_kernel, n_heads=n_heads, scale=ATTN_HEAD_DIM ** -0.5),
        grid_spec=grid_spec,
        out_shape=jax.ShapeDtypeStruct((s_len, width), BF16),
        compiler_params=_params("arbitrary"),
        name="masked_attention",
    )(qb_tab, kb_tab, p_main, p_main, p_main, mask, bias_tables)


def _t5_bucket(dist):
    max_exact = N_BUCKETS // 2
    d = jnp.maximum(dist, 0)
    log_ratio = jnp.log(jnp.maximum(d, 1).astype(F32) / max_exact) / math.log(MAX_DISTANCE / max_exact)
    large = jnp.minimum(max_exact + (log_ratio * (N_BUCKETS - max_exact)).astype(I32), N_BUCKETS - 1)
    return jnp.where(d < max_exact, d, large)


def _bias_tables(rel_bias):
    assert MAX_DISTANCE <= ATT_SUB
    i = jnp.arange(ATT_SUB, dtype=I32)[:, None]
    j = jnp.arange(ATT_SUB, dtype=I32)[None, :]
    bucket = _t5_bucket(jnp.stack([i - j, i - j + ATT_SUB]))[:, None]
    rel = rel_bias.astype(F32)
    far = rel[N_BUCKETS - 1]
    tables = jnp.zeros((2, rel_bias.shape[1], ATT_SUB, ATT_SUB), F32)
    for b in range(N_BUCKETS):
        tables = jnp.where(bucket == b, (rel[b] - far)[None, :, None, None], tables)
    return tables * LOG2E


POOL_TR = 512
POOL_G = 256
HALO = 16


def _pool_kernel(x_ref, prev_ref, w_ref, sc_ref, o_ref):
    i = pl.program_id(0)
    g = pl.program_id(1)
    tr = POOL_TR
    cur = x_ref[...]
    prev = jnp.where(i > 0, prev_ref[...], 0.0)
    ext = jnp.concatenate([prev, cur], axis=0)
    sums = []
    run = ext
    for shift in (1, 2, 4, 8):
        run = run + pltpu.roll(run, shift, 0)
        sums.append(run[HALO:])
    win_sum = jnp.where(g == 0, sums[0], jnp.where(g == 1, sums[1], jnp.where(g == 2, sums[2], sums[3])))
    win = jnp.left_shift(jnp.int32(2), g)
    t_pos = i * tr + lax.broadcasted_iota(I32, (tr, 1), 0)
    count = jnp.minimum(t_pos + 1, win).astype(F32)
    pooled = win_sum / count - cur
    o_ref[...] = (_dot(pooled.astype(BF16), w_ref[0]) * sc_ref[...]).astype(o_ref.dtype)


def multiscale_pool(p_aux, pool_w, pool_scale):
    s_len = p_aux.shape[0]
    tr, gw = POOL_TR, POOL_G
    ng = len(POOL_WINDOWS)
    return pl.pallas_call(
        _pool_kernel,
        grid=(s_len // tr, ng),
        in_specs=[
            pl.BlockSpec((tr, gw), lambda i, g: (i, g)),
            pl.BlockSpec((HALO, gw), lambda i, g: (jnp.maximum(i * (tr // HALO) - 1, 0), g)),
            pl.BlockSpec((1, gw, gw), lambda i, g: (g, 0, 0)),
            pl.BlockSpec((1, gw), lambda i, g: (0, g)),
        ],
        out_specs=pl.BlockSpec((tr, gw), lambda i, g: (i, g)),
        out_shape=jax.ShapeDtypeStruct((s_len, ng * gw), BF16),
        compiler_params=_params("parallel", "parallel"),
        name="multiscale_pool",
    )(p_aux, p_aux, pool_w, pool_scale.reshape(1, ng * gw))


CONV_TR = 512
CONV_TC = 512
CONV_RC = 32


def _conv_kernel(x_ref, prev_ref, w_ref, o_ref, ext_ref, *, n_norm_blocks):
    i = pl.program_id(0)
    j = pl.program_id(1)
    tr, rc = CONV_TR, CONV_RC
    nslab = CONV_TC // 128
    for sb in range(nslab):
        cols = slice(sb * 128, (sb + 1) * 128)
        ext_ref[sb, :HALO, :] = jnp.where(i > 0, prev_ref[:, cols].astype(F32), 0.0)
        ext_ref[sb, HALO:, :] = x_ref[:, cols].astype(F32)
    w = w_ref[...]

    def run(normalise):
        for r in range(tr // rc):
            base = HALO + r * rc
            for sb in range(nslab):
                cols = slice(sb * 128, (sb + 1) * 128)
                y = ext_ref[sb, base:base + rc, :] * w[CONV_K - 1:CONV_K, cols]
                for tap in range(CONV_K - 1):
                    lo = base - (CONV_K - 1 - tap)
                    y = y + ext_ref[sb, lo:lo + rc, :] * w[tap:tap + 1, cols]
                y = y * jax.nn.sigmoid(y)
                if normalise:
                    y = y * lax.rsqrt(jnp.sum(y * y, axis=-1, keepdims=True) + RMS_EPS)
                o_ref[r * rc:(r + 1) * rc, cols] = y

    @pl.when(j < n_norm_blocks)
    def _():
        run(True)

    @pl.when(j >= n_norm_blocks)
    def _():
        run(False)


def conv_silu_norm(p_main, conv_w, conv_width, qk_width):
    s_len = p_main.shape[0]
    tr, tc = CONV_TR, CONV_TC
    return pl.pallas_call(
        functools.partial(_conv_kernel, n_norm_blocks=2 * qk_width // tc),
        grid=(s_len // tr, conv_width // tc),
        in_specs=[
            pl.BlockSpec((tr, tc), lambda i, j: (i, j)),
            pl.BlockSpec((HALO, tc), lambda i, j: (jnp.maximum(i * (tr // HALO) - 1, 0), j)),
            pl.BlockSpec((CONV_K, tc), lambda i, j: (0, j)),
        ],
        out_specs=pl.BlockSpec((tr, tc), lambda i, j: (i, j)),
        out_shape=jax.ShapeDtypeStruct((s_len, conv_width), F32),
        scratch_shapes=[pltpu.VMEM((tc // 128, HALO + tr, 128), F32)],
        compiler_params=_params("parallel", "parallel"),
        name="conv_silu_norm",
    )(p_main, p_main, conv_w)


def _split3(x):
    hi = x.astype(BF16)
    r1 = x - hi.astype(F32)
    mid = r1.astype(BF16)
    lo = (r1 - mid.astype(F32)).astype(BF16)
    return hi, mid, lo


def _dot_exact_lhs(a_bf16, x):
    hi, mid, lo = _split3(x)
    return _dot(a_bf16, lo) + _dot(a_bf16, mid) + _dot(a_bf16, hi)


def _dot_exact_rhs(x, b_bf16):
    hi, mid, lo = _split3(x)
    return _dot(lo, b_bf16) + _dot(mid, b_bf16) + _dot(hi, b_bf16)


def _softplus(x):
    return jnp.maximum(x, 0.0) + jnp.log(1.0 + jnp.exp(-jnp.abs(x)))


def _gdn_prep_kernel(q_ref, k_ref, v_ref, ba_ref, alog_ref, dtb_ref,
                     u_ref, w_ref, qd_ref, kdt_ref, intra_ref, dl_ref, *, n_vheads):
    c = CHUNK
    dk = GDN_DK
    assert 2 * c == 128
    scale = dk ** -0.5
    ba = ba_ref[...]
    beta_all = jax.nn.sigmoid(ba)
    g_all = -jnp.exp(alog_ref[...]) * _softplus(ba + dtb_ref[...])
    row = lax.broadcasted_iota(I32, (c, c), 0)
    col = lax.broadcasted_iota(I32, (c, c), 1)
    lower = jnp.where(row >= col, 1.0, 0.0).astype(BF16)
    upper = jnp.where(row <= col, 1.0, 0.0).astype(BF16)
    gc_all = _dot_exact_lhs(lower, g_all)
    gct_all = _dot_exact_rhs(g_all.T, upper)

    row_w = lax.broadcasted_iota(I32, (c, 2 * c), 0)
    lane_w = lax.broadcasted_iota(I32, (c, 2 * c), 1)
    first = lane_w < c
    col_w = jnp.where(first, lane_w, lane_w - c)
    tril_w = row_w >= col_w
    strict_w = row_w > col_w
    eye_w = jnp.where(row_w == col_w, 1.0, 0.0).astype(F32)

    def block_diag(wide16):
        zero = jnp.zeros_like(wide16)
        return jnp.concatenate([jnp.where(first, wide16, zero), jnp.where(first, zero, wide16)], axis=0)

    pairs = range(n_vheads // 2)
    pws, t_invs, rhss = [], [], []
    for p in pairs:
        kh = k_ref[:, p * dk:(p + 1) * dk]
        qh = q_ref[:, p * dk:(p + 1) * dk] * scale
        kh16 = kh.astype(BF16)
        kk16 = jnp.concatenate([kh16, kh16], axis=0)
        gram = _dot_nt(kh16, kk16)
        qk = _dot_nt(qh.astype(BF16), kk16)
        hv = (2 * p, 2 * p + 1)
        beta = [beta_all[:, h:h + 1] for h in hv]
        gcol = [gc_all[:, n_vheads + h:n_vheads + h + 1] for h in hv]
        grow_w = jnp.concatenate([gct_all[n_vheads + h:n_vheads + h + 1, :] for h in hv], axis=1)
        gcol_w = jnp.where(first, gcol[0], gcol[1])
        decay_w = jnp.where(tril_w, jnp.exp(jnp.where(tril_w, gcol_w - grow_w, 0.0)), 0.0)
        a_w = jnp.where(strict_w, -(jnp.where(first, beta[0], beta[1]) * gram * decay_w), 0.0)
        intra_ref[0, p] = jnp.where(tril_w, qk * decay_w, 0.0).astype(BF16)
        rhs = []
        for i, h in enumerate(hv):
            eg = jnp.exp(gcol[i])
            g_last = gcol[i][c - 1:c, :]
            sl = slice(h * GDN_DV, (h + 1) * GDN_DV)
            qd_ref[:, sl] = (qh * eg).astype(BF16)
            kdt_ref[0, h] = (kh * jnp.exp(g_last - gcol[i])).T.astype(BF16)
            dl_ref[0, h] = jnp.broadcast_to(jnp.exp(g_last), (1, 128))
            kb = kh * beta[i]
            rhs.append(jnp.concatenate([v_ref[:, sl] * beta[i], kb * eg], axis=-1).astype(BF16))
        pws.append(a_w)
        t_invs.append(eye_w + a_w)
        rhss.append(jnp.concatenate(rhs, axis=0))
    for _ in range(int(math.log2(c)) - 1):
        pws = [_dot(pw.astype(BF16), block_diag(pw.astype(BF16))) for pw in pws]
        t_invs = [t + _dot(t.astype(BF16), block_diag(pw.astype(BF16))) for t, pw in zip(t_invs, pws)]
    for p, t_inv, rhs in zip(pairs, t_invs, rhss):
        t16 = t_inv.astype(BF16)
        zero = jnp.zeros_like(t16)
        for h, lhs in ((2 * p, jnp.where(first, t16, zero)), (2 * p + 1, jnp.where(first, zero, t16))):
            sol = _dot(lhs, rhs)
            sl = slice(h * GDN_DV, (h + 1) * GDN_DV)
            u_ref[:, sl] = sol[:, :GDN_DV]
            w_ref[:, sl] = sol[:, GDN_DV:].astype(BF16)


def gdn_prepare(qkv, ba, a_log, dt_bias, n_kheads, n_vheads):
    s_len = qkv.shape[0]
    c = CHUNK
    n = s_len // c
    qk_w = n_kheads * GDN_DK
    v_w = n_vheads * GDN_DV
    assert 2 * n_vheads <= 128 and n_vheads == 2 * n_kheads
    pad = jnp.zeros((128 - 2 * n_vheads,), F32)
    alog_row = jnp.concatenate([jnp.zeros((n_vheads,), F32), a_log, pad]).reshape(1, 128)
    dtb_row = jnp.concatenate([jnp.zeros((n_vheads,), F32), dt_bias, pad]).reshape(1, 128)
    qb, vb = qk_w // qk_w, (2 * qk_w) // v_w
    return pl.pallas_call(
        functools.partial(_gdn_prep_kernel, n_vheads=n_vheads),
        grid=(n,),
        in_specs=[
            pl.BlockSpec((c, qk_w), lambda i: (i, 0)),
            pl.BlockSpec((c, qk_w), lambda i: (i, qb)),
            pl.BlockSpec((c, v_w), lambda i: (i, vb)),
            pl.BlockSpec((c, 128), lambda i: (i, 0)),
            pl.BlockSpec((1, 128), lambda i: (0, 0)),
            pl.BlockSpec((1, 128), lambda i: (0, 0)),
        ],
        out_specs=[
            pl.BlockSpec((c, v_w), lambda i: (i, 0)),
            pl.BlockSpec((c, v_w), lambda i: (i, 0)),
            pl.BlockSpec((c, v_w), lambda i: (i, 0)),
            pl.BlockSpec((1, n_vheads, GDN_DK, c), lambda i: (i, 0, 0, 0)),
            pl.BlockSpec((1, n_vheads // 2, c, 2 * c), lambda i: (i, 0, 0, 0)),
            pl.BlockSpec((1, n_vheads, 1, 128), lambda i: (i, 0, 0, 0)),
        ],
        out_shape=[
            jax.ShapeDtypeStruct((s_len, v_w), F32),
            jax.ShapeDtypeStruct((s_len, v_w), BF16),
            jax.ShapeDtypeStruct((s_len, v_w), BF16),
            jax.ShapeDtypeStruct((n, n_vheads, GDN_DK, c), BF16),
            jax.ShapeDtypeStruct((n, n_vheads // 2, c, 2 * c), BF16),
            jax.ShapeDtypeStruct((n, n_vheads, 1, 128), F32),
        ],
        compiler_params=_params("parallel"),
        name="gdn_prepare",
    )(qkv, qkv, qkv, ba, alog_row, dtb_row)


SCAN_HG = 16


def _gdn_scan_kernel(u_ref, w_ref, qd_ref, kdt_ref, intra_ref, dl_ref, z_ref, nw_ref, o_ref, state_ref):
    @pl.when(pl.program_id(1) == 0)
    def _():
        state_ref[...] = jnp.zeros_like(state_ref)

    nw = nw_ref[...]
    heads = range(SCAN_HG)
    sls = [slice(j * GDN_DV, (j + 1) * GDN_DV) for j in heads]
    first = lax.broadcasted_iota(I32, (CHUNK, 2 * CHUNK), 1) < CHUNK
    states = [state_ref[j] for j in heads]
    s16 = [s.astype(BF16) for s in states]
    ws = [_dot(w_ref[:, sls[j]], s16[j]) for j in heads]
    qs = [_dot(qd_ref[:, sls[j]], s16[j]) for j in heads]
    v16 = [(u_ref[:, sls[j]] - ws[j]).astype(BF16) for j in heads]
    upd = [_dot(kdt_ref[0, j], v16[j]) for j in heads]
    outs = []
    for p in range(SCAN_HG // 2):
        pair = intra_ref[0, p]
        zero = jnp.zeros_like(pair)
        v_pair = jnp.concatenate([v16[2 * p], v16[2 * p + 1]], axis=0)
        outs.append(qs[2 * p] + _dot(jnp.where(first, pair, zero), v_pair))
        outs.append(qs[2 * p + 1] + _dot(jnp.where(first, zero, pair), v_pair))
    for j in heads:
        state_ref[j] = states[j] * dl_ref[0, j] + upd[j]
    for j in heads:
        z = z_ref[:, sls[j]].astype(F32)
        o_ref[:, sls[j]] = (_rms(outs[j], nw) * (z * jax.nn.sigmoid(z))).astype(o_ref.dtype)


def gdn_scan(u, w, qd, kdt, intra, dl, p_main, norm_w, z_col_block):
    s_len, v_w = u.shape
    c = CHUNK
    n = s_len // c
    hg = SCAN_HG
    gw = hg * GDN_DV
    return pl.pallas_call(
        _gdn_scan_kernel,
        grid=(v_w // gw, n),
        in_specs=[
            pl.BlockSpec((c, gw), lambda g, i: (i, g)),
            pl.BlockSpec((c, gw), lambda g, i: (i, g)),
            pl.BlockSpec((c, gw), lambda g, i: (i, g)),
            pl.BlockSpec((1, hg, GDN_DK, c), lambda g, i: (i, g, 0, 0)),
            pl.BlockSpec((1, hg // 2, c, 2 * c), lambda g, i: (i, g, 0, 0)),
            pl.BlockSpec((1, hg, 1, 128), lambda g, i: (i, g, 0, 0)),
            pl.BlockSpec((c, gw), lambda g, i: (i, z_col_block + g)),
            pl.BlockSpec((1, GDN_DV), lambda g, i: (0, 0)),
        ],
        out_specs=pl.BlockSpec((c, gw), lambda g, i: (i, g)),
        out_shape=jax.ShapeDtypeStruct((s_len, v_w), BF16),
        scratch_shapes=[pltpu.VMEM((hg, GDN_DK, GDN_DV), F32)],
        compiler_params=_params("parallel", "arbitrary"),
        name="gdn_scan",
    )(u, w, qd, kdt, intra, dl, p_main, norm_w.reshape(1, GDN_DV))


PROJ_TM = 1024
PROJ_TN = 1024
FFN_TN = 512
OUT_TM = 512
OUT_TK_MAX = 1408


def _out_tk(k):
    return max(c for c in range(128, OUT_TK_MAX + 1, 128) if k % c == 0)


def _even_mixer(x, g_pre, g_post, w_in16, w_in, w_o16, layer, pool_w, pool_scale, bias_tables):
    s_len, d = x.shape
    attn_w = d // 2
    n_heads = attn_w // ATTN_HEAD_DIM
    idx_w = N_IDX_HEADS * IDX_HEAD_DIM
    pool_width = len(POOL_WINDOWS) * POOL_G
    main_w = 3 * attn_w + idx_w
    assert main_w == 4096 and pool_width == 1024
    small = IDX_HEAD_DIM + N_IDX_HEADS
    w_aux = jnp.concatenate([w_in[:, main_w + small:], w_in[:, main_w:main_w + small],
                             jnp.zeros((d, 128 - small), F32)], axis=1).astype(BF16)
    p_main = norm_matmul(x, g_pre, w_in16, layer, main_w, BF16, PROJ_TM, PROJ_TN)
    p_aux = norm_matmul(x, g_pre, w_aux[None], 0, w_aux.shape[1], F32, PROJ_TM, 384)
    k_idx_t = p_aux[:, pool_width:pool_width + IDX_HEAD_DIM].astype(BF16).T
    kt2 = jnp.concatenate([k_idx_t, k_idx_t], axis=0)
    mask = indexer_mask(p_main, p_aux, kt2, min(TOPK_MAX, s_len // 4))
    attn = masked_attention(p_main, mask, bias_tables, n_heads)
    pooled = multiscale_pool(p_aux, pool_w.astype(BF16), pool_scale)
    return mm_norm_res([attn, pooled], w_o16, layer, g_post, x, OUT_TM, _out_tk(attn.shape[1]))


def _odd_mixer(x, g_pre, g_post, w_in16, w_in, w_o16, layer, conv_w, a_log, dt_bias, norm_w):
    s_len, d = x.shape
    n_kheads = d // 128
    n_vheads = 2 * n_kheads
    qk_w = n_kheads * GDN_DK
    v_w = n_vheads * GDN_DV
    conv_width = 2 * qk_w + v_w
    main_w = conv_width + v_w
    w_ba = jnp.concatenate([w_in[:, main_w:], jnp.zeros((d, 128 - 2 * n_vheads), F32)], axis=1).astype(BF16)
    p_main = norm_matmul(x, g_pre, w_in16, layer, main_w, BF16, PROJ_TM, PROJ_TN)
    ba = norm_matmul(x, g_pre, w_ba[None], 0, 128, F32, PROJ_TM, 128)
    qkv = conv_silu_norm(p_main, conv_w, conv_width, qk_w)
    u, w, qd, kdt, intra, dl = gdn_prepare(qkv, ba, a_log, dt_bias, n_kheads, n_vheads)
    o = gdn_scan(u, w, qd, kdt, intra, dl, p_main, norm_w, conv_width // (SCAN_HG * GDN_DV))
    return mm_norm_res([o], w_o16, layer, g_post, x, OUT_TM, _out_tk(v_w))


def _ffn(x, g_pre, g_post, w_gate16, w_up16, w_down16, layer):
    act = norm_swiglu(x, g_pre, w_gate16, w_up16, layer, PROJ_TM, FFN_TN)
    return mm_norm_res([act], w_down16, layer, g_post, x, OUT_TM, _out_tk(act.shape[1]))


def kernel(x, rel_bias, even_w_in, even_w_o, pool_w, pool_scale, odd_w_in, conv_w, a_log, dt_bias,
           gdn_norm_w, odd_w_o, ffn_w_gate, ffn_w_up, ffn_w_down, mix_pre_g, mix_post_g, ffn_pre_g, ffn_post_g):
    b, s_len, d = x.shape
    depth = ffn_w_gate.shape[0]
    bias_tables = _bias_tables(rel_bias)
    even_in16, even_o16 = cast_bf16(even_w_in), cast_bf16(even_w_o)
    odd_in16, odd_o16 = cast_bf16(odd_w_in), cast_bf16(odd_w_o)
    gate16, up16, down16 = cast_bf16(ffn_w_gate), cast_bf16(ffn_w_up), cast_bf16(ffn_w_down)
    outs = []
    for bi in range(b):
        h = x[bi]
        for layer in range(depth):
            i = layer // 2
            if layer % 2 == 0:
                h = _even_mixer(h, mix_pre_g[layer], mix_post_g[layer], even_in16, even_w_in[i], even_o16, i,
                                pool_w[i], pool_scale[i], bias_tables)
            else:
                h = _odd_mixer(h, mix_pre_g[layer], mix_post_g[layer], odd_in16, odd_w_in[i], odd_o16, i,
                               conv_w[i], a_log[i], dt_bias[i], gdn_norm_w[i])
            h = _ffn(h, ffn_pre_g[layer], ffn_post_g[layer], gate16, up16, down16, layer)
        outs.append(h)
    return jnp.stack(outs)
```

```python
import functools
import math

import jax
import jax.numpy as jnp
from jax import lax
from jax.experimental import pallas as pl
from jax.experimental.pallas import tpu as pltpu

F32 = jnp.float32
BF16 = jnp.bfloat16
I32 = jnp.int32

RMS_EPS = 1e-6
NEG_INF = -1e30
INT_MIN = -2 ** 31
LOG2E = 1.4426950408889634

VMEM_LIMIT_BYTES = 56 * 1024 * 1024

ATTN_HEAD_DIM = 128
N_IDX_HEADS = 16
IDX_HEAD_DIM = 64
TOPK_MAX = 256
N_BUCKETS = 32
MAX_DISTANCE = 128
POOL_WINDOWS = (2, 4, 8, 16)
GDN_DK = 128
GDN_DV = 128
CONV_K = 4
CHUNK = 64


def _params(*sem):
    return pltpu.CompilerParams(dimension_semantics=sem, vmem_limit_bytes=VMEM_LIMIT_BYTES)


def _dot(a, b):
    return jnp.dot(a, b, preferred_element_type=F32)


def _dot_nt(a, b):
    return lax.dot_general(a, b, (((1,), (1,)), ((), ())), preferred_element_type=F32)


def _to_u32(x):
    return lax.bitcast_convert_type(x ^ INT_MIN, jnp.uint32)


def _to_i32(x):
    return lax.bitcast_convert_type(x, I32)


def _rms(x, g):
    ms = jnp.mean(x * x, axis=-1, keepdims=True)
    return x * lax.rsqrt(ms + RMS_EPS) * g


def _norm_matmul_kernel(x_ref, g_ref, w_ref, o_ref, hn_ref):
    @pl.when(pl.program_id(1) == 0)
    def _():
        hn_ref[...] = _rms(x_ref[...], g_ref[...]).astype(BF16)

    o_ref[...] = _dot(hn_ref[...], w_ref[...]).astype(o_ref.dtype)


def norm_matmul(x, g, w, layer, n, out_dtype, tm, tn):
    m, k = x.shape
    assert m % tm == 0 and n % tn == 0 and n <= w.shape[2]
    return pl.pallas_call(
        _norm_matmul_kernel,
        grid=(m // tm, n // tn),
        in_specs=[
            pl.BlockSpec((tm, k), lambda i, j: (i, 0)),
            pl.BlockSpec((1, k), lambda i, j: (0, 0)),
            pl.BlockSpec((None, k, tn), lambda i, j: (layer, 0, j)),
        ],
        out_specs=pl.BlockSpec((tm, tn), lambda i, j: (i, j)),
        out_shape=jax.ShapeDtypeStruct((m, n), out_dtype),
        scratch_shapes=[pltpu.VMEM((tm, k), BF16)],
        compiler_params=_params("parallel", "arbitrary"),
        name="norm_matmul",
    )(x, g.reshape(1, k), w)


def _norm_swiglu_kernel(x_ref, g_ref, wg_ref, wu_ref, o_ref, hn_ref):
    @pl.when(pl.program_id(1) == 0)
    def _():
        hn_ref[...] = _rms(x_ref[...], g_ref[...]).astype(BF16)

    hn = hn_ref[...]
    a = _dot(hn, wg_ref[...])
    b = _dot(hn, wu_ref[...])
    o_ref[...] = (a * jax.nn.sigmoid(a) * b).astype(o_ref.dtype)


def norm_swiglu(x, g, wg, wu, layer, tm, tn):
    m, k = x.shape
    n = wg.shape[2]
    assert m % tm == 0 and n % tn == 0
    return pl.pallas_call(
        _norm_swiglu_kernel,
        grid=(m // tm, n // tn),
        in_specs=[
            pl.BlockSpec((tm, k), lambda i, j: (i, 0)),
            pl.BlockSpec((1, k), lambda i, j: (0, 0)),
            pl.BlockSpec((None, k, tn), lambda i, j: (layer, 0, j)),
            pl.BlockSpec((None, k, tn), lambda i, j: (layer, 0, j)),
        ],
        out_specs=pl.BlockSpec((tm, tn), lambda i, j: (i, j)),
        out_shape=jax.ShapeDtypeStruct((m, n), BF16),
        scratch_shapes=[pltpu.VMEM((tm, k), BF16)],
        compiler_params=_params("parallel", "arbitrary"),
        name="norm_swiglu",
    )(x, g.reshape(1, k), wg, wu)


def _mm_norm_res_kernel(*refs, n_seg, nk):
    a_refs = refs[:n_seg]
    w_refs = refs[n_seg:2 * n_seg]
    g_ref, r_ref, o_ref, acc_ref = refs[2 * n_seg:]
    kk = pl.program_id(1)

    @pl.when(kk == 0)
    def _():
        acc_ref[...] = jnp.zeros_like(acc_ref)

    part = _dot(a_refs[0][...], w_refs[0][...])
    for s in range(1, n_seg):
        part = part + _dot(a_refs[s][...], w_refs[s][...])
    acc_ref[...] += part

    @pl.when(kk == nk - 1)
    def _():
        o_ref[...] = r_ref[...] + _rms(acc_ref[...], g_ref[...])


def mm_norm_res(a_list, w, layer, g, res, tm, tk):
    n_seg = len(a_list)
    m, ka = a_list[0].shape
    n = w.shape[2]
    assert w.shape[1] == n_seg * ka and m % tm == 0 and ka % tk == 0
    nk = ka // tk
    a_specs = [pl.BlockSpec((tm, tk), lambda i, k: (i, k)) for _ in range(n_seg)]
    w_specs = [pl.BlockSpec((None, tk, n), functools.partial(lambda i, k, s: (layer, k + s * nk, 0), s=s))
               for s in range(n_seg)]
    return pl.pallas_call(
        functools.partial(_mm_norm_res_kernel, n_seg=n_seg, nk=nk),
        grid=(m // tm, nk),
        in_specs=a_specs + w_specs + [
            pl.BlockSpec((1, n), lambda i, k: (0, 0)),
            pl.BlockSpec((tm, n), lambda i, k: (i, 0)),
        ],
        out_specs=pl.BlockSpec((tm, n), lambda i, k: (i, 0)),
        out_shape=jax.ShapeDtypeStruct((m, n), F32),
        scratch_shapes=[pltpu.VMEM((tm, n), F32)],
        compiler_params=_params("parallel", "arbitrary"),
        name="mm_norm_res",
    )(*a_list, *([w] * n_seg), g.reshape(1, n), res)


CAST_ROWS = 128


def _cast_kernel(x_ref, o_ref):
    o_ref[...] = x_ref[...].astype(o_ref.dtype)


def cast_bf16(w):
    layers, k, n = w.shape
    assert k % CAST_ROWS == 0
    return pl.pallas_call(
        _cast_kernel,
        grid=(layers, k // CAST_ROWS),
        in_specs=[pl.BlockSpec((None, CAST_ROWS, n), lambda l, i: (l, i, 0))],
        out_specs=pl.BlockSpec((None, CAST_ROWS, n), lambda l, i: (l, i, 0)),
        out_shape=jax.ShapeDtypeStruct(w.shape, BF16),
        compiler_params=_params("parallel", "parallel"),
        name="cast_bf16",
    )(w)


IDX_TQ = 128
IDX_TK = 512
IDX_ROWS = 128


def _indexer_kernel(qi_ref, aux_ref, kt_ref, mask_ref, key_ref, qh_ref, wb_ref, *, topk, w_scale):
    tq, tk, tr = IDX_TQ, IDX_TK, IDX_ROWS
    nj = tk // 128
    s_len = mask_ref.shape[1]
    qb = pl.program_id(0)
    n_chunks = ((qb + 1) * tq - 1) // tk + 1

    lane = lax.broadcasted_iota(I32, (tq, 128), 1)
    aux = aux_ref[...]
    for h in range(N_IDX_HEADS):
        pair = qi_ref[:, (h // 2) * 128:(h // 2 + 1) * 128]
        keep = (lane < IDX_HEAD_DIM) if h % 2 == 0 else (lane >= IDX_HEAD_DIM)
        qh_ref[h] = jnp.where(keep, pair, jnp.zeros_like(pair))
        wcol = aux[:, IDX_HEAD_DIM + h:IDX_HEAD_DIM + h + 1] * w_scale
        wb_ref[h] = jnp.broadcast_to(wcol, (tq, 128))

    lane_r = lax.broadcasted_iota(I32, (tr, 128), 1)
    row_r = lax.broadcasted_iota(I32, (tr, 128), 0)

    def score_chunk(c, tops):
        col0 = pl.multiple_of(c * tk, tk)
        kt = kt_ref[:, pl.ds(col0, tk)]
        new_tops = []
        for r in range(tq // tr):
            rows = slice(r * tr, (r + 1) * tr)
            accs = [jnp.zeros((tr, 128), F32) for _ in range(nj)]
            for h in range(N_IDX_HEADS):
                d = _dot(qh_ref[h, rows], kt)
                wb = wb_ref[h, rows]
                for j in range(nj):
                    accs[j] = accs[j] + wb * jnp.maximum(d[:, j * 128:(j + 1) * 128], 0.0)
            q_pos = qb * tq + r * tr + row_r
            m1, m2 = tops[r]
            for j in range(nj):
                bits = pltpu.bitcast(accs[j], I32)
                skey = bits ^ ((bits >> 31) & 0x7FFFFFFF)
                skey = jnp.where(col0 + j * 128 + lane_r <= q_pos, skey, INT_MIN)
                key_ref[rows, pl.ds(pl.multiple_of(col0 + j * 128, 128), 128)] = skey
                m2 = jnp.maximum(m2, jnp.minimum(m1, skey))
                m1 = jnp.maximum(m1, skey)
            new_tops.append((m1, m2))
        return tuple(new_tops)

    lowest = jnp.full((tr, 128), INT_MIN, I32)
    tops = lax.fori_loop(0, n_chunks, score_chunk, tuple((lowest, lowest) for _ in range(tq // tr)))
    m1 = jnp.concatenate([t[0] for t in tops], axis=0)
    m2 = jnp.concatenate([t[1] for t in tops], axis=0)
    assert topk <= 2 * 128 and 127 + s_len // 128 < topk
    top = jnp.max(m2, axis=-1, keepdims=True)
    n_top = jnp.sum(jnp.where(m2 == top, 1, 0), axis=-1, keepdims=True)
    below = jnp.max(jnp.where(m2 == top, INT_MIN, m2), axis=-1, keepdims=True)
    hi = _to_u32(jnp.where(n_top >= 2, top, below))
    lo = _to_u32(jnp.min(m2, axis=-1, keepdims=True))
    n_iter = 32 - jnp.max(lax.clz(_to_i32(hi - lo)))

    def count_ge(cand):
        cand_b = jnp.broadcast_to(cand, (tq, 128))

        def body(c, cnt):
            blk = key_ref[:, pl.ds(pl.multiple_of(c * tk, tk), tk)]
            for j in range(nj):
                cnt = cnt + jnp.where(blk[:, j * 128:(j + 1) * 128] >= cand_b, 1, 0)
            return cnt

        cnt = lax.fori_loop(0, n_chunks, body, jnp.zeros((tq, 128), I32))
        return jnp.sum(cnt, axis=-1, keepdims=True)

    def bisect(_, bounds):
        lo, hi = bounds
        width = hi - lo
        mid = lo + (width >> 1) + (width & 1)
        ok = count_ge(_to_i32(mid) ^ INT_MIN) >= topk
        return jnp.where(ok, mid, lo), jnp.where(ok, hi, jnp.where(width > 0, mid - 1, hi))

    tau_u, _ = lax.fori_loop(0, n_iter, bisect, (lo, hi))
    tau_u = _to_i32(tau_u)
    tau = jnp.broadcast_to(jnp.maximum(tau_u ^ INT_MIN, INT_MIN + 1), (tq, 128))

    neg = jnp.full((tq, tk), NEG_INF, F32).astype(BF16)

    def write_chunk(c, carry):
        col0 = pl.multiple_of(c * tk, tk)

        @pl.when(c < n_chunks)
        def _():
            blk = key_ref[:, pl.ds(col0, tk)]
            parts = [jnp.where(blk[:, j * 128:(j + 1) * 128] >= tau, 0.0, NEG_INF) for j in range(nj)]
            mask_ref[:, pl.ds(col0, tk)] = jnp.concatenate(parts, axis=-1).astype(BF16)

        @pl.when(c >= n_chunks)
        def _():
            mask_ref[:, pl.ds(col0, tk)] = neg

        return carry

    lax.fori_loop(0, s_len // tk, write_chunk, 0)


def indexer_mask(p_main, p_aux, kt2, topk):
    s_len = p_main.shape[0]
    tq = IDX_TQ
    w_scale = (N_IDX_HEADS ** -0.5) * (IDX_HEAD_DIM ** -0.5)
    return pl.pallas_call(
        functools.partial(_indexer_kernel, topk=topk, w_scale=w_scale),
        grid=(s_len // tq,),
        in_specs=[
            pl.BlockSpec((tq, 1024), lambda i: (i, 3)),
            pl.BlockSpec((tq, 128), lambda i: (i, 8)),
            pl.BlockSpec((128, s_len), lambda i: (0, 0)),
        ],
        out_specs=pl.BlockSpec((tq, s_len), lambda i: (i, 0)),
        out_shape=jax.ShapeDtypeStruct((s_len, s_len), BF16),
        scratch_shapes=[
            pltpu.VMEM((tq, s_len), I32),
            pltpu.VMEM((N_IDX_HEADS, tq, 128), BF16),
            pltpu.VMEM((N_IDX_HEADS, tq, 128), F32),
        ],
        compiler_params=_params("parallel"),
        name="indexer_mask",
    )(p_main, p_aux, kt2)


ATT_T = 512
ATT_RC = 32
ATT_SUB = 128


def _attention_kernel(qb_tab, kb_tab, q_ref, k_ref, v_ref, mask_ref, tz_ref, o_ref,
                      qs_ref, mf_ref, p_ref, m_ref, l_ref, acc_ref, *, n_heads, scale):
    step = pl.program_id(0)
    qb = qb_tab[step]
    kb = kb_tab[step]
    dh, t, rc, sub = ATTN_HEAD_DIM, ATT_T, ATT_RC, ATT_SUB
    nsub = t // sub

    @pl.when(kb == 0)
    def _():
        qs_ref[...] = (q_ref[...].astype(F32) * (scale * LOG2E)).astype(BF16)
        m_ref[...] = jnp.full_like(m_ref, NEG_INF)
        l_ref[...] = jnp.zeros_like(l_ref)
        acc_ref[...] = jnp.zeros_like(acc_ref)

    mf_ref[...] = mask_ref[...].astype(F32)
    prev_flag = jnp.where(kb == qb - 1, 1.0, 0.0).astype(F32)

    def run(diag):
        for h in range(n_heads):
            sl = slice(h * dh, (h + 1) * dh)
            s = _dot_nt(qs_ref[:, sl], k_ref[:, sl])
            alphas = []
            for r in range(t // rc):
                rows = slice(r * rc, (r + 1) * rc)
                blk, r0 = divmod(r * rc, sub)
                sr = s[rows] + mf_ref[rows, :]
                parts = [sr[:, b * sub:(b + 1) * sub] for b in range(nsub)]
                if diag:
                    parts[blk] = parts[blk] + tz_ref[0, h, r0:r0 + rc, :]
                    if blk >= 1:
                        parts[blk - 1] = parts[blk - 1] + tz_ref[1, h, r0:r0 + rc, :]
                elif blk == 0:
                    parts[nsub - 1] = parts[nsub - 1] + prev_flag * tz_ref[1, h, r0:r0 + rc, :]
                m_prev = m_ref[h, rows]
                m_new = jnp.maximum(m_prev, jnp.max(functools.reduce(jnp.maximum, parts), axis=-1, keepdims=True))
                alpha = jnp.exp2(m_prev - m_new)
                ps = [jnp.exp2(pb - m_new) for pb in parts]
                l_ref[h, rows] = alpha * l_ref[h, rows] + jnp.sum(functools.reduce(jnp.add, ps), axis=-1, keepdims=True)
                m_ref[h, rows] = m_new
                p_ref[h, rows, :] = jnp.concatenate(ps, axis=-1).astype(BF16)
                alphas.append(alpha)
            pv = _dot(p_ref[h], v_ref[:, sl])
            for r in range(t // rc):
                rows = slice(r * rc, (r + 1) * rc)
                acc_ref[h, rows] = alphas[r] * acc_ref[h, rows] + pv[rows]

    @pl.when(kb != qb)
    def _():
        run(False)

    @pl.when(kb == qb)
    def _():
        run(True)
        for h in range(n_heads):
            o_ref[:, h * dh:(h + 1) * dh] = (acc_ref[h] / l_ref[h]).astype(o_ref.dtype)


def masked_attention(p_main, mask, bias_tables, n_heads):
    s_len = p_main.shape[0]
    t = ATT_T
    nb = s_len // t
    pairs = [(q, k) for q in range(nb) for k in range(q + 1)]
    qb_tab = jnp.array([p[0] for p in pairs], I32)
    kb_tab = jnp.array([p[1] for p in pairs], I32)
    width = n_heads * ATTN_HEAD_DIM
    grid_spec = pltpu.PrefetchScalarGridSpec(
        num_scalar_prefetch=2,
        grid=(len(pairs),),
        in_specs=[
            pl.BlockSpec((t, width), lambda s, qt, kt: (qt[s], 0)),
            pl.BlockSpec((t, width), lambda s, qt, kt: (kt[s], 1)),
            pl.BlockSpec((t, width), lambda s, qt, kt: (kt[s], 2)),
            pl.BlockSpec((t, t), lambda s, qt, kt: (qt[s], kt[s])),
            pl.BlockSpec((2, n_heads, ATT_SUB, ATT_SUB), lambda s, qt, kt: (0, 0, 0, 0)),
        ],
        out_specs=pl.BlockSpec((t, width), lambda s, qt, kt: (qt[s], 0)),
        scratch_shapes=[
            pltpu.VMEM((t, width), BF16),
            pltpu.VMEM((t, t), F32),
            pltpu.VMEM((n_heads, t, t), BF16),
            pltpu.VMEM((n_heads, t, 128), F32),
            pltpu.VMEM((n_heads, t, 128), F32),
            pltpu.VMEM((n_heads, t, ATTN_HEAD_DIM), F32),
        ],
    )
    return pl.pallas_call(
        functools.partial(_attention_kernel, n_heads=n_heads, scale=ATTN_HEAD_DIM ** -0.5),
        grid_spec=grid_spec,
        out_shape=jax.ShapeDtypeStruct((s_len, width), BF16),
        compiler_params=_params("arbitrary"),
        name="masked_attention",
    )(qb_tab, kb_tab, p_main, p_main, p_main, mask, bias_tables)


def _t5_bucket(dist):
    max_exact = N_BUCKETS // 2
    d = jnp.maximum(dist, 0)
    log_ratio = jnp.log(jnp.maximum(d, 1).astype(F32) / max_exact) / math.log(MAX_DISTANCE / max_exact)
    large = jnp.minimum(max_exact + (log_ratio * (N_BUCKETS - max_exact)).astype(I32), N_BUCKETS - 1)
    return jnp.where(d < max_exact, d, large)


def _bias_tables(rel_bias):
    assert MAX_DISTANCE <= ATT_SUB
    i = jnp.arange(ATT_SUB, dtype=I32)[:, None]
    j = jnp.arange(ATT_SUB, dtype=I32)[None, :]
    bucket = _t5_bucket(jnp.stack([i - j, i - j + ATT_SUB]))[:, None]
    rel = rel_bias.astype(F32)
    far = rel[N_BUCKETS - 1]
    tables = jnp.zeros((2, rel_bias.shape[1], ATT_SUB, ATT_SUB), F32)
    for b in range(N_BUCKETS):
        tables = jnp.where(bucket == b, (rel[b] - far)[None, :, None, None], tables)
    return tables * LOG2E


POOL_TR = 512
POOL_G = 256
HALO = 16


def _pool_kernel(x_ref, prev_ref, w_ref, sc_ref, o_ref):
    i = pl.program_id(0)
    g = pl.program_id(1)
    tr = POOL_TR
    cur = x_ref[...]
    prev = jnp.where(i > 0, prev_ref[...], 0.0)
    ext = jnp.concatenate([prev, cur], axis=0)
    sums = []
    run = ext
    for shift in (1, 2, 4, 8):
        run = run + pltpu.roll(run, shift, 0)
        sums.append(run[HALO:])
    win_sum = jnp.where(g == 0, sums[0], jnp.where(g == 1, sums[1], jnp.where(g == 2, sums[2], sums[3])))
    win = jnp.left_shift(jnp.int32(2), g)
    t_pos = i * tr + lax.broadcasted_iota(I32, (tr, 1), 0)
    count = jnp.minimum(t_pos + 1, win).astype(F32)
    pooled = win_sum / count - cur
    o_ref[...] = (_dot(pooled.astype(BF16), w_ref[0]) * sc_ref[...]).astype(o_ref.dtype)


def multiscale_pool(p_aux, pool_w, pool_scale):
    s_len = p_aux.shape[0]
    tr, gw = POOL_TR, POOL_G
    ng = len(POOL_WINDOWS)
    return pl.pallas_call(
        _pool_kernel,
        grid=(s_len // tr, ng),
        in_specs=[
            pl.BlockSpec((tr, gw), lambda i, g: (i, g)),
            pl.BlockSpec((HALO, gw), lambda i, g: (jnp.maximum(i * (tr // HALO) - 1, 0), g)),
            pl.BlockSpec((1, gw, gw), lambda i, g: (g, 0, 0)),
            pl.BlockSpec((1, gw), lambda i, g: (0, g)),
        ],
        out_specs=pl.BlockSpec((tr, gw), lambda i, g: (i, g)),
        out_shape=jax.ShapeDtypeStruct((s_len, ng * gw), BF16),
        compiler_params=_params("parallel", "parallel"),
        name="multiscale_pool",
    )(p_aux, p_aux, pool_w, pool_scale.reshape(1, ng * gw))


CONV_TR = 512
CONV_TC = 512
CONV_RC = 32


def _conv_kernel(x_ref, prev_ref, w_ref, o_ref, ext_ref, *, n_norm_blocks):
    i = pl.program_id(0)
    j = pl.program_id(1)
    tr, rc = CONV_TR, CONV_RC
    nslab = CONV_TC // 128
    for sb in range(nslab):
        cols = slice(sb * 128, (sb + 1) * 128)
        ext_ref[sb, :HALO, :] = jnp.where(i > 0, prev_ref[:, cols].astype(F32), 0.0)
        ext_ref[sb, HALO:, :] = x_ref[:, cols].astype(F32)
    w = w_ref[...]

    def run(normalise):
        for r in range(tr // rc):
            base = HALO + r * rc
            for sb in range(nslab):
                cols = slice(sb * 128, (sb + 1) * 128)
                y = ext_ref[sb, base:base + rc, :] * w[CONV_K - 1:CONV_K, cols]
                for tap in range(CONV_K - 1):
                    lo = base - (CONV_K - 1 - tap)
                    y = y + ext_ref[sb, lo:lo + rc, :] * w[tap:tap + 1, cols]
                y = y * jax.nn.sigmoid(y)
                if normalise:
                    y = y * lax.rsqrt(jnp.sum(y * y, axis=-1, keepdims=True) + RMS_EPS)
                o_ref[r * rc:(r + 1) * rc, cols] = y

    @pl.when(j < n_norm_blocks)
    def _():
        run(True)

    @pl.when(j >= n_norm_blocks)
    def _():
        run(False)


def conv_silu_norm(p_main, conv_w, conv_width, qk_width):
    s_len = p_main.shape[0]
    tr, tc = CONV_TR, CONV_TC
    return pl.pallas_call(
        functools.partial(_conv_kernel, n_norm_blocks=2 * qk_width // tc),
        grid=(s_len // tr, conv_width // tc),
        in_specs=[
            pl.BlockSpec((tr, tc), lambda i, j: (i, j)),
            pl.BlockSpec((HALO, tc), lambda i, j: (jnp.maximum(i * (tr // HALO) - 1, 0), j)),
            pl.BlockSpec((CONV_K, tc), lambda i, j: (0, j)),
        ],
        out_specs=pl.BlockSpec((tr, tc), lambda i, j: (i, j)),
        out_shape=jax.ShapeDtypeStruct((s_len, conv_width), F32),
        scratch_shapes=[pltpu.VMEM((tc // 128, HALO + tr, 128), F32)],
        compiler_params=_params("parallel", "parallel"),
        name="conv_silu_norm",
    )(p_main, p_main, conv_w)


def _split3(x):
    hi = x.astype(BF16)
    r1 = x - hi.astype(F32)
    mid = r1.astype(BF16)
    lo = (r1 - mid.astype(F32)).astype(BF16)
    return hi, mid, lo


def _dot_exact_lhs(a_bf16, x):
    hi, mid, lo = _split3(x)
    return _dot(a_bf16, lo) + _dot(a_bf16, mid) + _dot(a_bf16, hi)


def _dot_exact_rhs(x, b_bf16):
    hi, mid, lo = _split3(x)
    return _dot(lo, b_bf16) + _dot(mid, b_bf16) + _dot(hi, b_bf16)


def _softplus(x):
    return jnp.maximum(x, 0.0) + jnp.log(1.0 + jnp.exp(-jnp.abs(x)))


def _gdn_prep_kernel(q_ref, k_ref, v_ref, ba_ref, alog_ref, dtb_ref,
                     u_ref, w_ref, qd_ref, kdt_ref, intra_ref, dl_ref, *, n_vheads):
    c = CHUNK
    dk = GDN_DK
    assert 2 * c == 128
    scale = dk ** -0.5
    ba = ba_ref[...]
    beta_all = jax.nn.sigmoid(ba)
    g_all = -jnp.exp(alog_ref[...]) * _softplus(ba + dtb_ref[...])
    row = lax.broadcasted_iota(I32, (c, c), 0)
    col = lax.broadcasted_iota(I32, (c, c), 1)
    lower = jnp.where(row >= col, 1.0, 0.0).astype(BF16)
    upper = jnp.where(row <= col, 1.0, 0.0).astype(BF16)
    gc_all = _dot_exact_lhs(lower, g_all)
    gct_all = _dot_exact_rhs(g_all.T, upper)

    row_w = lax.broadcasted_iota(I32, (c, 2 * c), 0)
    lane_w = lax.broadcasted_iota(I32, (c, 2 * c), 1)
    first = lane_w < c
    col_w = jnp.where(first, lane_w, lane_w - c)
    tril_w = row_w >= col_w
    strict_w = row_w > col_w
    eye_w = jnp.where(row_w == col_w, 1.0, 0.0).astype(F32)

    def block_diag(wide16):
        zero = jnp.zeros_like(wide16)
        return jnp.concatenate([jnp.where(first, wide16, zero), jnp.where(first, zero, wide16)], axis=0)

    pairs = range(n_vheads // 2)
    pws, t_invs, rhss = [], [], []
    for p in pairs:
        kh = k_ref[:, p * dk:(p + 1) * dk]
        qh = q_ref[:, p * dk:(p + 1) * dk] * scale
        kh16 = kh.astype(BF16)
        kk16 = jnp.concatenate([kh16, kh16], axis=0)
        gram = _dot_nt(kh16, kk16)
        qk = _dot_nt(qh.astype(BF16), kk16)
        hv = (2 * p, 2 * p + 1)
        beta = [beta_all[:, h:h + 1] for h in hv]
        gcol = [gc_all[:, n_vheads + h:n_vheads + h + 1] for h in hv]
        grow_w = jnp.concatenate([gct_all[n_vheads + h:n_vheads + h + 1, :] for h in hv], axis=1)
        gcol_w = jnp.where(first, gcol[0], gcol[1])
        decay_w = jnp.where(tril_w, jnp.exp(jnp.where(tril_w, gcol_w - grow_w, 0.0)), 0.0)
        a_w = jnp.where(strict_w, -(jnp.where(first, beta[0], beta[1]) * gram * decay_w), 0.0)
        intra_ref[0, p] = jnp.where(tril_w, qk * decay_w, 0.0).astype(BF16)
        rhs = []
        for i, h in enumerate(hv):
            eg = jnp.exp(gcol[i])
            g_last = gcol[i][c - 1:c, :]
            sl = slice(h * GDN_DV, (h + 1) * GDN_DV)
            qd_ref[:, sl] = (qh * eg).astype(BF16)
            kdt_ref[0, h] = (kh * jnp.exp(g_last - gcol[i])).T.astype(BF16)
            dl_ref[0, h] = jnp.broadcast_to(jnp.exp(g_last), (1, 128))
            kb = kh * beta[i]
            rhs.append(jnp.concatenate([v_ref[:, sl] * beta[i], kb * eg], axis=-1).astype(BF16))
        pws.append(a_w)
        t_invs.append(eye_w + a_w)
        rhss.append(jnp.concatenate(rhs, axis=0))
    for _ in range(int(math.log2(c)) - 1):
        pws = [_dot(pw.astype(BF16), block_diag(pw.astype(BF16))) for pw in pws]
        t_invs = [t + _dot(t.astype(BF16), block_diag(pw.astype(BF16))) for t, pw in zip(t_invs, pws)]
    for p, t_inv, rhs in zip(pairs, t_invs, rhss):
        t16 = t_inv.astype(BF16)
        zero = jnp.zeros_like(t16)
        for h, lhs in ((2 * p, jnp.where(first, t16, zero)), (2 * p + 1, jnp.where(first, zero, t16))):
            sol = _dot(lhs, rhs)
            sl = slice(h * GDN_DV, (h + 1) * GDN_DV)
            u_ref[:, sl] = sol[:, :GDN_DV]
            w_ref[:, sl] = sol[:, GDN_DV:].astype(BF16)


def gdn_prepare(qkv, ba, a_log, dt_bias, n_kheads, n_vheads):
    s_len = qkv.shape[0]
    c = CHUNK
    n = s_len // c
    qk_w = n_kheads * GDN_DK
    v_w = n_vheads * GDN_DV
    assert 2 * n_vheads <= 128 and n_vheads == 2 * n_kheads
    pad = jnp.zeros((128 - 2 * n_vheads,), F32)
    alog_row = jnp.concatenate([jnp.zeros((n_vheads,), F32), a_log, pad]).reshape(1, 128)
    dtb_row = jnp.concatenate([jnp.zeros((n_vheads,), F32), dt_bias, pad]).reshape(1, 128)
    qb, vb = qk_w // qk_w, (2 * qk_w) // v_w
    return pl.pallas_call(
        functools.partial(_gdn_prep_kernel, n_vheads=n_vheads),
        grid=(n,),
        in_specs=[
            pl.BlockSpec((c, qk_w), lambda i: (i, 0)),
            pl.BlockSpec((c, qk_w), lambda i: (i, qb)),
            pl.BlockSpec((c, v_w), lambda i: (i, vb)),
            pl.BlockSpec((c, 128), lambda i: (i, 0)),
            pl.BlockSpec((1, 128), lambda i: (0, 0)),
            pl.BlockSpec((1, 128), lambda i: (0, 0)),
        ],
        out_specs=[
            pl.BlockSpec((c, v_w), lambda i: (i, 0)),
            pl.BlockSpec((c, v_w), lambda i: (i, 0)),
            pl.BlockSpec((c, v_w), lambda i: (i, 0)),
            pl.BlockSpec((1, n_vheads, GDN_DK, c), lambda i: (i, 0, 0, 0)),
            pl.BlockSpec((1, n_vheads // 2, c, 2 * c), lambda i: (i, 0, 0, 0)),
            pl.BlockSpec((1, n_vheads, 1, 128), lambda i: (i, 0, 0, 0)),
        ],
        out_shape=[
            jax.ShapeDtypeStruct((s_len, v_w), F32),
            jax.ShapeDtypeStruct((s_len, v_w), BF16),
            jax.ShapeDtypeStruct((s_len, v_w), BF16),
            jax.ShapeDtypeStruct((n, n_vheads, GDN_DK, c), BF16),
            jax.ShapeDtypeStruct((n, n_vheads // 2, c, 2 * c), BF16),
            jax.ShapeDtypeStruct((n, n_vheads, 1, 128), F32),
        ],
        compiler_params=_params("parallel"),
        name="gdn_prepare",
    )(qkv, qkv, qkv, ba, alog_row, dtb_row)


SCAN_HG = 16
SCAN_CHUNKS = 4


def _gdn_scan_kernel(u_ref, w_ref, qd_ref, kdt_ref, intra_ref, dl_ref, z_ref, nw_ref, o_ref, state_ref):
    @pl.when(pl.program_id(1) == 0)
    def _():
        state_ref[...] = jnp.zeros_like(state_ref)

    nw = nw_ref[...]
    heads = range(SCAN_HG)
    sls = [slice(j * GDN_DV, (j + 1) * GDN_DV) for j in heads]
    first = lax.broadcasted_iota(I32, (CHUNK, 2 * CHUNK), 1) < CHUNK
    states = [state_ref[j] for j in heads]
    for ci in range(SCAN_CHUNKS):
        rows = slice(ci * CHUNK, (ci + 1) * CHUNK)
        s16 = [s.astype(BF16) for s in states]
        ws = [_dot(w_ref[rows, sls[j]], s16[j]) for j in heads]
        qs = [_dot(qd_ref[rows, sls[j]], s16[j]) for j in heads]
        v16 = [(u_ref[rows, sls[j]] - ws[j]).astype(BF16) for j in heads]
        upd = [_dot(kdt_ref[ci, j], v16[j]) for j in heads]
        outs = []
        for p in range(SCAN_HG // 2):
            pair = intra_ref[ci, p]
            zero = jnp.zeros_like(pair)
            v_pair = jnp.concatenate([v16[2 * p], v16[2 * p + 1]], axis=0)
            outs.append(qs[2 * p] + _dot(jnp.where(first, pair, zero), v_pair))
            outs.append(qs[2 * p + 1] + _dot(jnp.where(first, zero, pair), v_pair))
        states = [states[j] * dl_ref[ci, j] + upd[j] for j in heads]
        for j in heads:
            z = z_ref[rows, sls[j]].astype(F32)
            o_ref[rows, sls[j]] = (_rms(outs[j], nw) * (z * jax.nn.sigmoid(z))).astype(o_ref.dtype)
    for j in heads:
        state_ref[j] = states[j]


def gdn_scan(u, w, qd, kdt, intra, dl, p_main, norm_w, z_col_block):
    s_len, v_w = u.shape
    nc = SCAN_CHUNKS
    c = CHUNK * nc
    n = s_len // c
    hg = SCAN_HG
    gw = hg * GDN_DV
    return pl.pallas_call(
        _gdn_scan_kernel,
        grid=(v_w // gw, n),
        in_specs=[
            pl.BlockSpec((c, gw), lambda g, i: (i, g)),
            pl.BlockSpec((c, gw), lambda g, i: (i, g)),
            pl.BlockSpec((c, gw), lambda g, i: (i, g)),
            pl.BlockSpec((nc, hg, GDN_DK, CHUNK), lambda g, i: (i, g, 0, 0)),
            pl.BlockSpec((nc, hg // 2, CHUNK, 2 * CHUNK), lambda g, i: (i, g, 0, 0)),
            pl.BlockSpec((nc, hg, 1, 128), lambda g, i: (i, g, 0, 0)),
            pl.BlockSpec((c, gw), lambda g, i: (i, z_col_block + g)),
            pl.BlockSpec((1, GDN_DV), lambda g, i: (0, 0)),
        ],
        out_specs=pl.BlockSpec((c, gw), lambda g, i: (i, g)),
        out_shape=jax.ShapeDtypeStruct((s_len, v_w), BF16),
        scratch_shapes=[pltpu.VMEM((hg, GDN_DK, GDN_DV), F32)],
        compiler_params=_params("parallel", "arbitrary"),
        name="gdn_scan",
    )(u, w, qd, kdt, intra, dl, p_main, norm_w.reshape(1, GDN_DV))


PROJ_TM = 1024
PROJ_TN = 1024
FFN_TN = 512
OUT_TM = 512
OUT_TK_MAX = 1408


def _out_tk(k):
    return max(c for c in range(128, OUT_TK_MAX + 1, 128) if k % c == 0)


def _even_main_width(d):
    return 3 * (d // 2) + N_IDX_HEADS * IDX_HEAD_DIM


def _odd_main_width(d):
    n_kheads = d // 128
    return 2 * n_kheads * GDN_DK + 2 * (2 * n_kheads) * GDN_DV


def _even_mixer(x, g_pre, g_post, w_in16, w_in, w_o16, layer, pool_w, pool_scale, bias_tables):
    s_len, d = x.shape
    attn_w = d // 2
    n_heads = attn_w // ATTN_HEAD_DIM
    idx_w = N_IDX_HEADS * IDX_HEAD_DIM
    pool_width = len(POOL_WINDOWS) * POOL_G
    main_w = _even_main_width(d)
    assert main_w == 3 * attn_w + idx_w == 4096 and pool_width == 1024
    small = IDX_HEAD_DIM + N_IDX_HEADS
    w_aux = jnp.concatenate([w_in[:, main_w + small:], w_in[:, main_w:main_w + small],
                             jnp.zeros((d, 128 - small), F32)], axis=1).astype(BF16)
    p_main = norm_matmul(x, g_pre, w_in16, layer, main_w, BF16, PROJ_TM, PROJ_TN)
    p_aux = norm_matmul(x, g_pre, w_aux[None], 0, w_aux.shape[1], F32, PROJ_TM, 384)
    k_idx_t = p_aux[:, pool_width:pool_width + IDX_HEAD_DIM].astype(BF16).T
    kt2 = jnp.concatenate([k_idx_t, k_idx_t], axis=0)
    mask = indexer_mask(p_main, p_aux, kt2, min(TOPK_MAX, s_len // 4))
    attn = masked_attention(p_main, mask, bias_tables, n_heads)
    pooled = multiscale_pool(p_aux, pool_w.astype(BF16), pool_scale)
    return mm_norm_res([attn, pooled], w_o16, layer, g_post, x, OUT_TM, _out_tk(attn.shape[1]))


def _odd_mixer(x, g_pre, g_post, w_in16, w_in, w_o16, layer, conv_w, a_log, dt_bias, norm_w):
    s_len, d = x.shape
    n_kheads = d // 128
    n_vheads = 2 * n_kheads
    qk_w = n_kheads * GDN_DK
    v_w = n_vheads * GDN_DV
    conv_width = 2 * qk_w + v_w
    main_w = _odd_main_width(d)
    assert main_w == conv_width + v_w
    w_ba = jnp.concatenate([w_in[:, main_w:], jnp.zeros((d, 128 - 2 * n_vheads), F32)], axis=1).astype(BF16)
    p_main = norm_matmul(x, g_pre, w_in16, layer, main_w, BF16, PROJ_TM, PROJ_TN)
    ba = norm_matmul(x, g_pre, w_ba[None], 0, 128, F32, PROJ_TM, 128)
    qkv = conv_silu_norm(p_main, conv_w, conv_width, qk_w)
    u, w, qd, kdt, intra, dl = gdn_prepare(qkv, ba, a_log, dt_bias, n_kheads, n_vheads)
    o = gdn_scan(u, w, qd, kdt, intra, dl, p_main, norm_w, conv_width // (SCAN_HG * GDN_DV))
    return mm_norm_res([o], w_o16, layer, g_post, x, OUT_TM, _out_tk(v_w))


def _ffn(x, g_pre, g_post, w_gate16, w_up16, w_down16, layer):
    act = norm_swiglu(x, g_pre, w_gate16, w_up16, layer, PROJ_TM, FFN_TN)
    return mm_norm_res([act], w_down16, layer, g_post, x, OUT_TM, _out_tk(act.shape[1]))


def kernel(x, rel_bias, even_w_in, even_w_o, pool_w, pool_scale, odd_w_in, conv_w, a_log, dt_bias,
           gdn_norm_w, odd_w_o, ffn_w_gate, ffn_w_up, ffn_w_down, mix_pre_g, mix_post_g, ffn_pre_g, ffn_post_g):
    b, s_len, d = x.shape
    depth = ffn_w_gate.shape[0]
    bias_tables = _bias_tables(rel_bias)
    even_in16 = even_w_in[:, :, :_even_main_width(d)].astype(BF16)
    odd_in16 = odd_w_in[:, :, :_odd_main_width(d)].astype(BF16)
    even_o16, odd_o16 = cast_bf16(even_w_o), cast_bf16(odd_w_o)
    gate16, up16, down16 = cast_bf16(ffn_w_gate), cast_bf16(ffn_w_up), cast_bf16(ffn_w_down)
    outs = []
    for bi in range(b):
        h = x[bi]
        for layer in range(depth):
            i = layer // 2
            if layer % 2 == 0:
                h = _even_mixer(h, mix_pre_g[layer], mix_post_g[layer], even_in16, even_w_in[i], even_o16, i,
                                pool_w[i], pool_scale[i], bias_tables)
            else:
                h = _odd_mixer(h, mix_pre_g[layer], mix_post_g[layer], odd_in16, odd_w_in[i], odd_o16, i,
                               conv_w[i], a_log[i], dt_bias[i], gdn_norm_w[i])
            h = _ffn(h, ffn_pre_g[layer], ffn_post_g[layer], gate16, up16, down16, layer)
        outs.append(h)
    return jnp.stack(outs)
```

```python
import functools
import math

import jax
import jax.numpy as jnp
from jax import lax
from jax.experimental import pallas as pl
from jax.experimental.pallas import tpu as pltpu

F32 = jnp.float32
BF16 = jnp.bfloat16
I32 = jnp.int32

RMS_EPS = 1e-6
NEG_INF = -1e30
INT_MIN = -2 ** 31
LOG2E = 1.4426950408889634

VMEM_LIMIT_BYTES = 56 * 1024 * 1024

ATTN_HEAD_DIM = 128
N_IDX_HEADS = 16
IDX_HEAD_DIM = 64
TOPK_MAX = 256
N_BUCKETS = 32
MAX_DISTANCE = 128
POOL_WINDOWS = (2, 4, 8, 16)
GDN_DK = 128
GDN_DV = 128
CONV_K = 4
CHUNK = 64


def _params(*sem):
    return pltpu.CompilerParams(dimension_semantics=sem, vmem_limit_bytes=VMEM_LIMIT_BYTES)


def _dot(a, b):
    return jnp.dot(a, b, preferred_element_type=F32)


def _dot_nt(a, b):
    return lax.dot_general(a, b, (((1,), (1,)), ((), ())), preferred_element_type=F32)


def _to_u32(x):
    return lax.bitcast_convert_type(x ^ INT_MIN, jnp.uint32)


def _to_i32(x):
    return lax.bitcast_convert_type(x, I32)


def _rms(x, g):
    ms = jnp.mean(x * x, axis=-1, keepdims=True)
    return x * lax.rsqrt(ms + RMS_EPS) * g


def _norm_matmul_kernel(x_ref, g_ref, w_ref, o_ref, hn_ref):
    @pl.when(pl.program_id(1) == 0)
    def _():
        hn_ref[...] = _rms(x_ref[...], g_ref[...]).astype(BF16)

    o_ref[...] = _dot(hn_ref[...], w_ref[...]).astype(o_ref.dtype)


def norm_matmul(x, g, w, layer, n, out_dtype, tm, tn):
    m, k = x.shape
    assert m % tm == 0 and n % tn == 0 and n <= w.shape[2]
    return pl.pallas_call(
        _norm_matmul_kernel,
        grid=(m // tm, n // tn),
        in_specs=[
            pl.BlockSpec((tm, k), lambda i, j: (i, 0)),
            pl.BlockSpec((1, k), lambda i, j: (0, 0)),
            pl.BlockSpec((None, k, tn), lambda i, j: (layer, 0, j)),
        ],
        out_specs=pl.BlockSpec((tm, tn), lambda i, j: (i, j)),
        out_shape=jax.ShapeDtypeStruct((m, n), out_dtype),
        scratch_shapes=[pltpu.VMEM((tm, k), BF16)],
        compiler_params=_params("parallel", "arbitrary"),
        name="norm_matmul",
    )(x, g.reshape(1, k), w)


def _norm_swiglu_kernel(x_ref, g_ref, wg_ref, wu_ref, o_ref, hn_ref):
    @pl.when(pl.program_id(1) == 0)
    def _():
        hn_ref[...] = _rms(x_ref[...], g_ref[...]).astype(BF16)

    hn = hn_ref[...]
    a = _dot(hn, wg_ref[...])
    b = _dot(hn, wu_ref[...])
    o_ref[...] = (a * jax.nn.sigmoid(a) * b).astype(o_ref.dtype)


def norm_swiglu(x, g, wg, wu, layer, tm, tn):
    m, k = x.shape
    n = wg.shape[2]
    assert m % tm == 0 and n % tn == 0
    return pl.pallas_call(
        _norm_swiglu_kernel,
        grid=(m // tm, n // tn),
        in_specs=[
            pl.BlockSpec((tm, k), lambda i, j: (i, 0)),
            pl.BlockSpec((1, k), lambda i, j: (0, 0)),
            pl.BlockSpec((None, k, tn), lambda i, j: (layer, 0, j)),
            pl.BlockSpec((None, k, tn), lambda i, j: (layer, 0, j)),
        ],
        out_specs=pl.BlockSpec((tm, tn), lambda i, j: (i, j)),
        out_shape=jax.ShapeDtypeStruct((m, n), BF16),
        scratch_shapes=[pltpu.VMEM((tm, k), BF16)],
        compiler_params=_params("parallel", "arbitrary"),
        name="norm_swiglu",
    )(x, g.reshape(1, k), wg, wu)


def _ffn_kernel(x_ref, g_ref, wg_ref, wu_ref, wd_ref, gp_ref, o_ref, hn_ref, acc_ref, *, nj):
    j = pl.program_id(1)

    @pl.when(j == 0)
    def _():
        hn_ref[...] = _rms(x_ref[...], g_ref[...]).astype(BF16)
        acc_ref[...] = jnp.zeros_like(acc_ref)

    hn = hn_ref[...]
    a = _dot(hn, wg_ref[...])
    b = _dot(hn, wu_ref[...])
    act = (a * jax.nn.sigmoid(a) * b).astype(BF16)
    acc_ref[...] += _dot(act, wd_ref[...])

    @pl.when(j == nj - 1)
    def _():
        o_ref[...] = x_ref[...] + _rms(acc_ref[...], gp_ref[...])


def ffn(x, g_pre, g_post, wg, wu, wd, layer, tm, tf):
    m, d = x.shape
    ff = wg.shape[2]
    assert m % tm == 0 and ff % tf == 0
    nj = ff // tf
    return pl.pallas_call(
        functools.partial(_ffn_kernel, nj=nj),
        grid=(m // tm, nj),
        in_specs=[
            pl.BlockSpec((tm, d), lambda i, j: (i, 0)),
            pl.BlockSpec((1, d), lambda i, j: (0, 0)),
            pl.BlockSpec((None, d, tf), lambda i, j: (layer, 0, j)),
            pl.BlockSpec((None, d, tf), lambda i, j: (layer, 0, j)),
            pl.BlockSpec((None, tf, d), lambda i, j: (layer, j, 0)),
            pl.BlockSpec((1, d), lambda i, j: (0, 0)),
        ],
        out_specs=pl.BlockSpec((tm, d), lambda i, j: (i, 0)),
        out_shape=jax.ShapeDtypeStruct((m, d), F32),
        scratch_shapes=[pltpu.VMEM((tm, d), BF16), pltpu.VMEM((tm, d), F32)],
        compiler_params=_params("parallel", "arbitrary"),
        name="ffn",
    )(x, g_pre.reshape(1, d), wg, wu, wd, g_post.reshape(1, d))


def _mm_norm_res_kernel(*refs, n_seg, nk):
    a_refs = refs[:n_seg]
    w_refs = refs[n_seg:2 * n_seg]
    g_ref, r_ref, o_ref, acc_ref = refs[2 * n_seg:]
    kk = pl.program_id(1)

    @pl.when(kk == 0)
    def _():
        acc_ref[...] = jnp.zeros_like(acc_ref)

    part = _dot(a_refs[0][...], w_refs[0][...])
    for s in range(1, n_seg):
        part = part + _dot(a_refs[s][...], w_refs[s][...])
    acc_ref[...] += part

    @pl.when(kk == nk - 1)
    def _():
        o_ref[...] = r_ref[...] + _rms(acc_ref[...], g_ref[...])


def mm_norm_res(a_list, w, layer, g, res, tm, tk):
    n_seg = len(a_list)
    m, ka = a_list[0].shape
    n = w.shape[2]
    assert w.shape[1] == n_seg * ka and m % tm == 0 and ka % tk == 0
    nk = ka // tk
    a_specs = [pl.BlockSpec((tm, tk), lambda i, k: (i, k)) for _ in range(n_seg)]
    w_specs = [pl.BlockSpec((None, tk, n), functools.partial(lambda i, k, s: (layer, k + s * nk, 0), s=s))
               for s in range(n_seg)]
    return pl.pallas_call(
        functools.partial(_mm_norm_res_kernel, n_seg=n_seg, nk=nk),
        grid=(m // tm, nk),
        in_specs=a_specs + w_specs + [
            pl.BlockSpec((1, n), lambda i, k: (0, 0)),
            pl.BlockSpec((tm, n), lambda i, k: (i, 0)),
        ],
        out_specs=pl.BlockSpec((tm, n), lambda i, k: (i, 0)),
        out_shape=jax.ShapeDtypeStruct((m, n), F32),
        scratch_shapes=[pltpu.VMEM((tm, n), F32)],
        compiler_params=_params("parallel", "arbitrary"),
        name="mm_norm_res",
    )(*a_list, *([w] * n_seg), g.reshape(1, n), res)


CAST_BLOCK_BYTES = 4 * 1024 * 1024


def _cast_kernel(x_ref, o_ref):
    o_ref[...] = x_ref[...].astype(o_ref.dtype)


def cast_bf16(w):
    layers, k, n = w.shape
    rows = max(r for r in range(16, k + 1, 16) if k % r == 0 and r * n * 4 <= CAST_BLOCK_BYTES)
    return pl.pallas_call(
        _cast_kernel,
        grid=(layers, k // rows),
        in_specs=[pl.BlockSpec((None, rows, n), lambda l, i: (l, i, 0))],
        out_specs=pl.BlockSpec((None, rows, n), lambda l, i: (l, i, 0)),
        out_shape=jax.ShapeDtypeStruct(w.shape, BF16),
        compiler_params=_params("parallel", "parallel"),
        name="cast_bf16",
    )(w)


IDX_TQ = 128
IDX_TK = 512
IDX_ROWS = 128


def _indexer_kernel(qi_ref, aux_ref, kt_ref, mask_ref, key_ref, qh_ref, wb_ref, *, topk, w_scale):
    tq, tk, tr = IDX_TQ, IDX_TK, IDX_ROWS
    nj = tk // 128
    s_len = mask_ref.shape[1]
    qb = pl.program_id(0)
    n_chunks = ((qb + 1) * tq - 1) // tk + 1

    lane = lax.broadcasted_iota(I32, (tq, 128), 1)
    aux = aux_ref[...]
    for h in range(N_IDX_HEADS):
        pair = qi_ref[:, (h // 2) * 128:(h // 2 + 1) * 128]
        keep = (lane < IDX_HEAD_DIM) if h % 2 == 0 else (lane >= IDX_HEAD_DIM)
        qh_ref[h] = jnp.where(keep, pair, jnp.zeros_like(pair))
        wcol = aux[:, IDX_HEAD_DIM + h:IDX_HEAD_DIM + h + 1] * w_scale
        wb_ref[h] = jnp.broadcast_to(wcol, (tq, 128))

    lane_r = lax.broadcasted_iota(I32, (tr, 128), 1)
    row_r = lax.broadcasted_iota(I32, (tr, 128), 0)

    def score_chunk(c, tops):
        col0 = pl.multiple_of(c * tk, tk)
        kt = kt_ref[:, pl.ds(col0, tk)]
        new_tops = []
        for r in range(tq // tr):
            rows = slice(r * tr, (r + 1) * tr)
            accs = [jnp.zeros((tr, 128), F32) for _ in range(nj)]
            for h in range(N_IDX_HEADS):
                d = _dot(qh_ref[h, rows], kt)
                wb = wb_ref[h, rows]
                for j in range(nj):
                    accs[j] = accs[j] + wb * jnp.maximum(d[:, j * 128:(j + 1) * 128], 0.0)
            q_pos = qb * tq + r * tr + row_r
            m1, m2 = tops[r]
            for j in range(nj):
                bits = pltpu.bitcast(accs[j], I32)
                skey = bits ^ ((bits >> 31) & 0x7FFFFFFF)
                skey = jnp.where(col0 + j * 128 + lane_r <= q_pos, skey, INT_MIN)
                key_ref[rows, pl.ds(pl.multiple_of(col0 + j * 128, 128), 128)] = skey
                m2 = jnp.maximum(m2, jnp.minimum(m1, skey))
                m1 = jnp.maximum(m1, skey)
            new_tops.append((m1, m2))
        return tuple(new_tops)

    lowest = jnp.full((tr, 128), INT_MIN, I32)
    tops = lax.fori_loop(0, n_chunks, score_chunk, tuple((lowest, lowest) for _ in range(tq // tr)))
    m1 = jnp.concatenate([t[0] for t in tops], axis=0)
    m2 = jnp.concatenate([t[1] for t in tops], axis=0)
    assert topk <= 2 * 128 and 127 + s_len // 128 < topk
    top = jnp.max(m2, axis=-1, keepdims=True)
    n_top = jnp.sum(jnp.where(m2 == top, 1, 0), axis=-1, keepdims=True)
    below = jnp.max(jnp.where(m2 == top, INT_MIN, m2), axis=-1, keepdims=True)
    hi = _to_u32(jnp.where(n_top >= 2, top, below))
    lo = _to_u32(jnp.min(m2, axis=-1, keepdims=True))
    n_iter = 32 - jnp.max(lax.clz(_to_i32(hi - lo)))

    def count_ge(cand):
        cand_b = jnp.broadcast_to(cand, (tq, 128))

        def body(c, cnt):
            blk = key_ref[:, pl.ds(pl.multiple_of(c * tk, tk), tk)]
            for j in range(nj):
                cnt = cnt + jnp.where(blk[:, j * 128:(j + 1) * 128] >= cand_b, 1, 0)
            return cnt

        cnt = lax.fori_loop(0, n_chunks, body, jnp.zeros((tq, 128), I32))
        return jnp.sum(cnt, axis=-1, keepdims=True)

    def bisect(_, bounds):
        lo, hi = bounds
        width = hi - lo
        mid = lo + (width >> 1) + (width & 1)
        ok = count_ge(_to_i32(mid) ^ INT_MIN) >= topk
        return jnp.where(ok, mid, lo), jnp.where(ok, hi, jnp.where(width > 0, mid - 1, hi))

    tau_u, _ = lax.fori_loop(0, n_iter, bisect, (lo, hi))
    tau_u = _to_i32(tau_u)
    tau = jnp.broadcast_to(jnp.maximum(tau_u ^ INT_MIN, INT_MIN + 1), (tq, 128))

    neg = jnp.full((tq, tk), NEG_INF, F32).astype(BF16)

    def write_chunk(c, carry):
        col0 = pl.multiple_of(c * tk, tk)

        @pl.when(c < n_chunks)
        def _():
            blk = key_ref[:, pl.ds(col0, tk)]
            parts = [jnp.where(blk[:, j * 128:(j + 1) * 128] >= tau, 0.0, NEG_INF) for j in range(nj)]
            mask_ref[:, pl.ds(col0, tk)] = jnp.concatenate(parts, axis=-1).astype(BF16)

        @pl.when(c >= n_chunks)
        def _():
            mask_ref[:, pl.ds(col0, tk)] = neg

        return carry

    lax.fori_loop(0, s_len // tk, write_chunk, 0)


def indexer_mask(p_main, p_aux, kt2, topk):
    s_len = p_main.shape[0]
    tq = IDX_TQ
    w_scale = (N_IDX_HEADS ** -0.5) * (IDX_HEAD_DIM ** -0.5)
    return pl.pallas_call(
        functools.partial(_indexer_kernel, topk=topk, w_scale=w_scale),
        grid=(s_len // tq,),
        in_specs=[
            pl.BlockSpec((tq, 1024), lambda i: (i, 3)),
            pl.BlockSpec((tq, 128), lambda i: (i, 8)),
            pl.BlockSpec((128, s_len), lambda i: (0, 0)),
        ],
        out_specs=pl.BlockSpec((tq, s_len), lambda i: (i, 0)),
        out_shape=jax.ShapeDtypeStruct((s_len, s_len), BF16),
        scratch_shapes=[
            pltpu.VMEM((tq, s_len), I32),
            pltpu.VMEM((N_IDX_HEADS, tq, 128), BF16),
            pltpu.VMEM((N_IDX_HEADS, tq, 128), F32),
        ],
        compiler_params=_params("parallel"),
        name="indexer_mask",
    )(p_main, p_aux, kt2)


ATT_T = 512
ATT_RC = 32
ATT_SUB = 128


def _attention_kernel(qb_tab, kb_tab, q_ref, k_ref, v_ref, mask_ref, tz_ref, o_ref,
                      qs_ref, mf_ref, p_ref, m_ref, l_ref, acc_ref, *, n_heads, scale):
    step = pl.program_id(0)
    qb = qb_tab[step]
    kb = kb_tab[step]
    dh, t, rc, sub = ATTN_HEAD_DIM, ATT_T, ATT_RC, ATT_SUB
    nsub = t // sub

    @pl.when(kb == 0)
    def _():
        qs_ref[...] = (q_ref[...].astype(F32) * (scale * LOG2E)).astype(BF16)
        m_ref[...] = jnp.full_like(m_ref, NEG_INF)
        l_ref[...] = jnp.zeros_like(l_ref)
        acc_ref[...] = jnp.zeros_like(acc_ref)

    mf_ref[...] = mask_ref[...].astype(F32)
    prev_flag = jnp.where(kb == qb - 1, 1.0, 0.0).astype(F32)

    def run(diag):
        for h in range(n_heads):
            sl = slice(h * dh, (h + 1) * dh)
            s = _dot_nt(qs_ref[:, sl], k_ref[:, sl])
            alphas = []
            for r in range(t // rc):
                rows = slice(r * rc, (r + 1) * rc)
                blk, r0 = divmod(r * rc, sub)
                sr = s[rows] + mf_ref[rows, :]
                parts = [sr[:, b * sub:(b + 1) * sub] for b in range(nsub)]
                if diag:
                    parts[blk] = parts[blk] + tz_ref[0, h, r0:r0 + rc, :]
                    if blk >= 1:
                        parts[blk - 1] = parts[blk - 1] + tz_ref[1, h, r0:r0 + rc, :]
                elif blk == 0:
                    parts[nsub - 1] = parts[nsub - 1] + prev_flag * tz_ref[1, h, r0:r0 + rc, :]
                m_prev = m_ref[h, rows]
                m_new = jnp.maximum(m_prev, jnp.max(functools.reduce(jnp.maximum, parts), axis=-1, keepdims=True))
                alpha = jnp.exp2(m_prev - m_new)
                ps = [jnp.exp2(pb - m_new) for pb in parts]
                l_ref[h, rows] = alpha * l_ref[h, rows] + jnp.sum(functools.reduce(jnp.add, ps), axis=-1, keepdims=True)
                m_ref[h, rows] = m_new
                p_ref[h, rows, :] = jnp.concatenate(ps, axis=-1).astype(BF16)
                alphas.append(alpha)
            pv = _dot(p_ref[h], v_ref[:, sl])
            for r in range(t // rc):
                rows = slice(r * rc, (r + 1) * rc)
                acc_ref[h, rows] = alphas[r] * acc_ref[h, rows] + pv[rows]

    @pl.when(kb != qb)
    def _():
        run(False)

    @pl.when(kb == qb)
    def _():
        run(True)
        for h in range(n_heads):
            o_ref[:, h * dh:(h + 1) * dh] = (acc_ref[h] / l_ref[h]).astype(o_ref.dtype)


def masked_attention(p_main, mask, bias_tables, n_heads):
    s_len = p_main.shape[0]
    t = ATT_T
    nb = s_len // t
    pairs = [(q, k) for q in range(nb) for k in range(q + 1)]
    qb_tab = jnp.array([p[0] for p in pairs], I32)
    kb_tab = jnp.array([p[1] for p in pairs], I32)
    width = n_heads * ATTN_HEAD_DIM
    grid_spec = pltpu.PrefetchScalarGridSpec(
        num_scalar_prefetch=2,
        grid=(len(pairs),),
        in_specs=[
            pl.BlockSpec((t, width), lambda s, qt, kt: (qt[s], 0)),
            pl.BlockSpec((t, width), lambda s, qt, kt: (kt[s], 1)),
            pl.BlockSpec((t, width), lambda s, qt, kt: (kt[s], 2)),
            pl.BlockSpec((t, t), lambda s, qt, kt: (qt[s], kt[s])),
            pl.BlockSpec((2, n_heads, ATT_SUB, ATT_SUB), lambda s, qt, kt: (0, 0, 0, 0)),
        ],
        out_specs=pl.BlockSpec((t, width), lambda s, qt, kt: (qt[s], 0)),
        scratch_shapes=[
            pltpu.VMEM((t, width), BF16),
            pltpu.VMEM((t, t), F32),
            pltpu.VMEM((n_heads, t, t), BF16),
            pltpu.VMEM((n_heads, t, 128), F32),
            pltpu.VMEM((n_heads, t, 128), F32),
            pltpu.VMEM((n_heads, t, ATTN_HEAD_DIM), F32),
        ],
    )
    return pl.pallas_call(
        functools.partial(_attention_kernel, n_heads=n_heads, scale=ATTN_HEAD_DIM ** -0.5),
        grid_spec=grid_spec,
        out_shape=jax.ShapeDtypeStruct((s_len, width), BF16),
        compiler_params=_params("arbitrary"),
        name="masked_attention",
    )(qb_tab, kb_tab, p_main, p_main, p_main, mask, bias_tables)


def _t5_bucket(dist):
    max_exact = N_BUCKETS // 2
    d = jnp.maximum(dist, 0)
    log_ratio = jnp.log(jnp.maximum(d, 1).astype(F32) / max_exact) / math.log(MAX_DISTANCE / max_exact)
    large = jnp.minimum(max_exact + (log_ratio * (N_BUCKETS - max_exact)).astype(I32), N_BUCKETS - 1)
    return jnp.where(d < max_exact, d, large)


def _bias_tables(rel_bias):
    assert MAX_DISTANCE <= ATT_SUB
    i = jnp.arange(ATT_SUB, dtype=I32)[:, None]
    j = jnp.arange(ATT_SUB, dtype=I32)[None, :]
    bucket = _t5_bucket(jnp.stack([i - j, i - j + ATT_SUB]))[:, None]
    rel = rel_bias.astype(F32)
    far = rel[N_BUCKETS - 1]
    tables = jnp.zeros((2, rel_bias.shape[1], ATT_SUB, ATT_SUB), F32)
    for b in range(N_BUCKETS):
        tables = jnp.where(bucket == b, (rel[b] - far)[None, :, None, None], tables)
    return tables * LOG2E


POOL_TR = 512
POOL_G = 256
HALO = 16


def _pool_kernel(x_ref, prev_ref, w_ref, sc_ref, o_ref):
    i = pl.program_id(0)
    g = pl.program_id(1)
    tr = POOL_TR
    cur = x_ref[...]
    prev = jnp.where(i > 0, prev_ref[...], 0.0)
    ext = jnp.concatenate([prev, cur], axis=0)
    sums = []
    run = ext
    for shift in (1, 2, 4, 8):
        run = run + pltpu.roll(run, shift, 0)
        sums.append(run[HALO:])
    win_sum = jnp.where(g == 0, sums[0], jnp.where(g == 1, sums[1], jnp.where(g == 2, sums[2], sums[3])))
    win = jnp.left_shift(jnp.int32(2), g)
    t_pos = i * tr + lax.broadcasted_iota(I32, (tr, 1), 0)
    count = jnp.minimum(t_pos + 1, win).astype(F32)
    pooled = win_sum / count - cur
    o_ref[...] = (_dot(pooled.astype(BF16), w_ref[0]) * sc_ref[...]).astype(o_ref.dtype)


def multiscale_pool(p_aux, pool_w, pool_scale):
    s_len = p_aux.shape[0]
    tr, gw = POOL_TR, POOL_G
    ng = len(POOL_WINDOWS)
    return pl.pallas_call(
        _pool_kernel,
        grid=(s_len // tr, ng),
        in_specs=[
            pl.BlockSpec((tr, gw), lambda i, g: (i, g)),
            pl.BlockSpec((HALO, gw), lambda i, g: (jnp.maximum(i * (tr // HALO) - 1, 0), g)),
            pl.BlockSpec((1, gw, gw), lambda i, g: (g, 0, 0)),
            pl.BlockSpec((1, gw), lambda i, g: (0, g)),
        ],
        out_specs=pl.BlockSpec((tr, gw), lambda i, g: (i, g)),
        out_shape=jax.ShapeDtypeStruct((s_len, ng * gw), BF16),
        compiler_params=_params("parallel", "parallel"),
        name="multiscale_pool",
    )(p_aux, p_aux, pool_w, pool_scale.reshape(1, ng * gw))


CONV_TR = 512
CONV_TC = 512
CONV_RC = 32


def _conv_kernel(x_ref, prev_ref, w_ref, o_ref, ext_ref, *, n_norm_blocks):
    i = pl.program_id(0)
    j = pl.program_id(1)
    tr, rc = CONV_TR, CONV_RC
    nslab = CONV_TC // 128
    for sb in range(nslab):
        cols = slice(sb * 128, (sb + 1) * 128)
        ext_ref[sb, :HALO, :] = jnp.where(i > 0, prev_ref[:, cols].astype(F32), 0.0)
        ext_ref[sb, HALO:, :] = x_ref[:, cols].astype(F32)
    w = w_ref[...]

    def run(normalise):
        for r in range(tr // rc):
            base = HALO + r * rc
            for sb in range(nslab):
                cols = slice(sb * 128, (sb + 1) * 128)
                y = ext_ref[sb, base:base + rc, :] * w[CONV_K - 1:CONV_K, cols]
                for tap in range(CONV_K - 1):
                    lo = base - (CONV_K - 1 - tap)
                    y = y + ext_ref[sb, lo:lo + rc, :] * w[tap:tap + 1, cols]
                y = y * jax.nn.sigmoid(y)
                if normalise:
                    y = y * lax.rsqrt(jnp.sum(y * y, axis=-1, keepdims=True) + RMS_EPS)
                o_ref[r * rc:(r + 1) * rc, cols] = y

    @pl.when(j < n_norm_blocks)
    def _():
        run(True)

    @pl.when(j >= n_norm_blocks)
    def _():
        run(False)


def conv_silu_norm(p_main, conv_w, conv_width, qk_width):
    s_len = p_main.shape[0]
    tr, tc = CONV_TR, CONV_TC
    return pl.pallas_call(
        functools.partial(_conv_kernel, n_norm_blocks=2 * qk_width // tc),
        grid=(s_len // tr, conv_width // tc),
        in_specs=[
            pl.BlockSpec((tr, tc), lambda i, j: (i, j)),
            pl.BlockSpec((HALO, tc), lambda i, j: (jnp.maximum(i * (tr // HALO) - 1, 0), j)),
            pl.BlockSpec((CONV_K, tc), lambda i, j: (0, j)),
        ],
        out_specs=pl.BlockSpec((tr, tc), lambda i, j: (i, j)),
        out_shape=jax.ShapeDtypeStruct((s_len, conv_width), F32),
        scratch_shapes=[pltpu.VMEM((tc // 128, HALO + tr, 128), F32)],
        compiler_params=_params("parallel", "parallel"),
        name="conv_silu_norm",
    )(p_main, p_main, conv_w)


def _split3(x):
    hi = x.astype(BF16)
    r1 = x - hi.astype(F32)
    mid = r1.astype(BF16)
    lo = (r1 - mid.astype(F32)).astype(BF16)
    return hi, mid, lo


def _dot_exact_lhs(a_bf16, x):
    hi, mid, lo = _split3(x)
    return _dot(a_bf16, lo) + _dot(a_bf16, mid) + _dot(a_bf16, hi)


def _dot_exact_rhs(x, b_bf16):
    hi, mid, lo = _split3(x)
    return _dot(lo, b_bf16) + _dot(mid, b_bf16) + _dot(hi, b_bf16)


def _softplus(x):
    return jnp.maximum(x, 0.0) + jnp.log(1.0 + jnp.exp(-jnp.abs(x)))


PREP_CHUNKS = 2


def _gdn_prep_kernel(q_ref, k_ref, v_ref, ba_ref, alog_ref, dtb_ref,
                     u_ref, w_ref, qd_ref, kdt_ref, intra_ref, dl_ref, *, n_vheads):
    c = CHUNK
    dk = GDN_DK
    assert 2 * c == 128
    scale = dk ** -0.5
    row = lax.broadcasted_iota(I32, (c, c), 0)
    col = lax.broadcasted_iota(I32, (c, c), 1)
    lower = jnp.where(row >= col, 1.0, 0.0).astype(BF16)
    upper = jnp.where(row <= col, 1.0, 0.0).astype(BF16)
    row_w = lax.broadcasted_iota(I32, (c, 2 * c), 0)
    lane_w = lax.broadcasted_iota(I32, (c, 2 * c), 1)
    first = lane_w < c
    col_w = jnp.where(first, lane_w, lane_w - c)
    tril_w = row_w >= col_w
    strict_w = row_w > col_w
    eye_w = jnp.where(row_w == col_w, 1.0, 0.0).astype(F32)

    def block_diag(wide16):
        zero = jnp.zeros_like(wide16)
        return jnp.concatenate([jnp.where(first, wide16, zero), jnp.where(first, zero, wide16)], axis=0)

    pairs = range(n_vheads // 2)
    pws, t_invs, rhss, where = [], [], [], []
    for ci in range(PREP_CHUNKS):
        rows = slice(ci * c, (ci + 1) * c)
        ba = ba_ref[rows, :]
        beta_all = jax.nn.sigmoid(ba)
        g_all = -jnp.exp(alog_ref[...]) * _softplus(ba + dtb_ref[...])
        gc_all = _dot_exact_lhs(lower, g_all)
        gct_all = _dot_exact_rhs(g_all.T, upper)
        for p in pairs:
            kh = k_ref[rows, p * dk:(p + 1) * dk]
            qh = q_ref[rows, p * dk:(p + 1) * dk] * scale
            kh_t = kh.T
            kh16 = kh.astype(BF16)
            kk16 = jnp.concatenate([kh16, kh16], axis=0)
            both = _dot_nt(jnp.concatenate([kh16, qh.astype(BF16)], axis=0), kk16)
            gram, qk = both[:c], both[c:]
            hv = (2 * p, 2 * p + 1)
            beta = [jnp.broadcast_to(beta_all[:, h:h + 1], (c, 2 * c)) for h in hv]
            gcol = [jnp.broadcast_to(gc_all[:, n_vheads + h:n_vheads + h + 1], (c, 2 * c)) for h in hv]
            grow = [gct_all[n_vheads + h:n_vheads + h + 1, :] for h in hv]
            grow_w = jnp.concatenate(grow, axis=1)
            gcol_w = jnp.where(first, gcol[0], gcol[1])
            decay_w = jnp.where(tril_w, jnp.exp(jnp.where(tril_w, gcol_w - grow_w, 0.0)), 0.0)
            a_w = jnp.where(strict_w, -(jnp.where(first, beta[0], beta[1]) * gram * decay_w), 0.0)
            intra_ref[ci, p] = jnp.where(tril_w, qk * decay_w, 0.0).astype(BF16)
            rhs = []
            for i, h in enumerate(hv):
                eg = jnp.exp(gcol[i])
                g_last = grow[i][:, c - 1:c]
                sl = slice(h * GDN_DV, (h + 1) * GDN_DV)
                qd_ref[rows, sl] = (qh * eg).astype(BF16)
                kdt_ref[ci, h] = (kh_t * jnp.exp(g_last - grow[i])).astype(BF16)
                dl_ref[ci, h] = jnp.broadcast_to(jnp.exp(g_last), (1, 128))
                rhs.append(jnp.concatenate([v_ref[rows, sl] * beta[i], kh * (beta[i] * eg)], axis=-1).astype(BF16))
            pws.append(a_w)
            t_invs.append(eye_w + a_w)
            rhss.append(jnp.concatenate(rhs, axis=0))
            where.append((rows, p))
    n_levels = int(math.log2(c)) - 1
    pws = [_dot(a.astype(BF16), block_diag(a.astype(BF16))) for a in pws]
    for level in range(n_levels):
        new_t, new_pw = [], []
        for t, pw in zip(t_invs, pws):
            pw16 = pw.astype(BF16)
            if level == n_levels - 1:
                new_t.append(t + _dot(t.astype(BF16), block_diag(pw16)))
            else:
                both = _dot(jnp.concatenate([t.astype(BF16), pw16], axis=0), block_diag(pw16))
                new_t.append(t + both[:c])
                new_pw.append(both[c:])
        t_invs, pws = new_t, new_pw
    for (rows, p), t_inv, rhs in zip(where, t_invs, rhss):
        t16 = t_inv.astype(BF16)
        zero = jnp.zeros_like(t16)
        lhs = jnp.concatenate([jnp.where(first, t16, zero), jnp.where(first, zero, t16)], axis=0)
        sol = _dot(lhs, rhs)
        for i, h in enumerate((2 * p, 2 * p + 1)):
            sl = slice(h * GDN_DV, (h + 1) * GDN_DV)
            u_ref[rows, sl] = sol[i * c:(i + 1) * c, :GDN_DV]
            w_ref[rows, sl] = sol[i * c:(i + 1) * c, GDN_DV:].astype(BF16)


def gdn_prepare(qkv, ba, a_log, dt_bias, n_kheads, n_vheads):
    s_len = qkv.shape[0]
    nc = PREP_CHUNKS
    c = CHUNK * nc
    n = s_len // c
    qk_w = n_kheads * GDN_DK
    v_w = n_vheads * GDN_DV
    assert 2 * n_vheads <= 128 and n_vheads == 2 * n_kheads
    pad = jnp.zeros((128 - 2 * n_vheads,), F32)
    alog_row = jnp.concatenate([jnp.zeros((n_vheads,), F32), a_log, pad]).reshape(1, 128)
    dtb_row = jnp.concatenate([jnp.zeros((n_vheads,), F32), dt_bias, pad]).reshape(1, 128)
    qb, vb = qk_w // qk_w, (2 * qk_w) // v_w
    return pl.pallas_call(
        functools.partial(_gdn_prep_kernel, n_vheads=n_vheads),
        grid=(n,),
        in_specs=[
            pl.BlockSpec((c, qk_w), lambda i: (i, 0)),
            pl.BlockSpec((c, qk_w), lambda i: (i, qb)),
            pl.BlockSpec((c, v_w), lambda i: (i, vb)),
            pl.BlockSpec((c, 128), lambda i: (i, 0)),
            pl.BlockSpec((1, 128), lambda i: (0, 0)),
            pl.BlockSpec((1, 128), lambda i: (0, 0)),
        ],
        out_specs=[
            pl.BlockSpec((c, v_w), lambda i: (i, 0)),
            pl.BlockSpec((c, v_w), lambda i: (i, 0)),
            pl.BlockSpec((c, v_w), lambda i: (i, 0)),
            pl.BlockSpec((nc, n_vheads, GDN_DK, CHUNK), lambda i: (i, 0, 0, 0)),
            pl.BlockSpec((nc, n_vheads // 2, CHUNK, 2 * CHUNK), lambda i: (i, 0, 0, 0)),
            pl.BlockSpec((nc, n_vheads, 1, 128), lambda i: (i, 0, 0, 0)),
        ],
        out_shape=[
            jax.ShapeDtypeStruct((s_len, v_w), F32),
            jax.ShapeDtypeStruct((s_len, v_w), BF16),
            jax.ShapeDtypeStruct((s_len, v_w), BF16),
            jax.ShapeDtypeStruct((n * nc, n_vheads, GDN_DK, CHUNK), BF16),
            jax.ShapeDtypeStruct((n * nc, n_vheads // 2, CHUNK, 2 * CHUNK), BF16),
            jax.ShapeDtypeStruct((n * nc, n_vheads, 1, 128), F32),
        ],
        compiler_params=_params("parallel"),
        name="gdn_prepare",
    )(qkv, qkv, qkv, ba, alog_row, dtb_row)


SCAN_HG = 16
SCAN_CHUNKS = 4


def _gdn_scan_kernel(u_ref, w_ref, qd_ref, kdt_ref, intra_ref, dl_ref, z_ref, nw_ref, o_ref, state_ref):
    @pl.when(pl.program_id(1) == 0)
    def _():
        state_ref[...] = jnp.zeros_like(state_ref)

    nw = nw_ref[...]
    heads = range(SCAN_HG)
    sls = [slice(j * GDN_DV, (j + 1) * GDN_DV) for j in heads]
    first = lax.broadcasted_iota(I32, (CHUNK, 2 * CHUNK), 1) < CHUNK
    states = [state_ref[j] for j in heads]
    for ci in range(SCAN_CHUNKS):
        rows = slice(ci * CHUNK, (ci + 1) * CHUNK)
        s16 = [s.astype(BF16) for s in states]
        ws = [_dot(w_ref[rows, sls[j]], s16[j]) for j in heads]
        qs = [_dot(qd_ref[rows, sls[j]], s16[j]) for j in heads]
        v16 = [(u_ref[rows, sls[j]] - ws[j]).astype(BF16) for j in heads]
        upd = [_dot(kdt_ref[ci, j], v16[j]) for j in heads]
        outs = []
        for p in range(SCAN_HG // 2):
            pair = intra_ref[ci, p]
            zero = jnp.zeros_like(pair)
            v_pair = jnp.concatenate([v16[2 * p], v16[2 * p + 1]], axis=0)
            outs.append(qs[2 * p] + _dot(jnp.where(first, pair, zero), v_pair))
            outs.append(qs[2 * p + 1] + _dot(jnp.where(first, zero, pair), v_pair))
        states = [states[j] * dl_ref[ci, j] + upd[j] for j in heads]
        for j in heads:
            z = z_ref[rows, sls[j]].astype(F32)
            o_ref[rows, sls[j]] = (_rms(outs[j], nw) * (z * jax.nn.sigmoid(z))).astype(o_ref.dtype)
    for j in heads:
        state_ref[j] = states[j]


def gdn_scan(u, w, qd, kdt, intra, dl, p_main, norm_w, z_col_block):
    s_len, v_w = u.shape
    nc = SCAN_CHUNKS
    c = CHUNK * nc
    n = s_len // c
    hg = SCAN_HG
    gw = hg * GDN_DV
    return pl.pallas_call(
        _gdn_scan_kernel,
        grid=(v_w // gw, n),
        in_specs=[
            pl.BlockSpec((c, gw), lambda g, i: (i, g)),
            pl.BlockSpec((c, gw), lambda g, i: (i, g)),
            pl.BlockSpec((c, gw), lambda g, i: (i, g)),
            pl.BlockSpec((nc, hg, GDN_DK, CHUNK), lambda g, i: (i, g, 0, 0)),
            pl.BlockSpec((nc, hg // 2, CHUNK, 2 * CHUNK), lambda g, i: (i, g, 0, 0)),
            pl.BlockSpec((nc, hg, 1, 128), lambda g, i: (i, g, 0, 0)),
            pl.BlockSpec((c, gw), lambda g, i: (i, z_col_block + g)),
            pl.BlockSpec((1, GDN_DV), lambda g, i: (0, 0)),
        ],
        out_specs=pl.BlockSpec((c, gw), lambda g, i: (i, g)),
        out_shape=jax.ShapeDtypeStruct((s_len, v_w), BF16),
        scratch_shapes=[pltpu.VMEM((hg, GDN_DK, GDN_DV), F32)],
        compiler_params=_params("parallel", "arbitrary"),
        name="gdn_scan",
    )(u, w, qd, kdt, intra, dl, p_main, norm_w.reshape(1, GDN_DV))


PROJ_TM = 1024
PROJ_TN = 1024
FFN_TM = 512
FFN_TF = 512
OUT_TM = 512
OUT_TK_MAX = 1408


def _out_tk(k):
    return max(c for c in range(128, OUT_TK_MAX + 1, 128) if k % c == 0)


def _even_main_width(d):
    return 3 * (d // 2) + N_IDX_HEADS * IDX_HEAD_DIM


def _odd_main_width(d):
    n_kheads = d // 128
    return 2 * n_kheads * GDN_DK + 2 * (2 * n_kheads) * GDN_DV


def _even_mixer(x, g_pre, g_post, w_in16, w_rest, w_o16, layer, pool_w, pool_scale, bias_tables):
    s_len, d = x.shape
    attn_w = d // 2
    n_heads = attn_w // ATTN_HEAD_DIM
    idx_w = N_IDX_HEADS * IDX_HEAD_DIM
    pool_width = len(POOL_WINDOWS) * POOL_G
    main_w = _even_main_width(d)
    assert main_w == 3 * attn_w + idx_w == 4096 and pool_width == 1024
    small = IDX_HEAD_DIM + N_IDX_HEADS
    w_aux = jnp.concatenate([w_rest[:, small:], w_rest[:, :small],
                             jnp.zeros((d, 128 - small), F32)], axis=1).astype(BF16)
    p_main = norm_matmul(x, g_pre, w_in16, layer, main_w, BF16, PROJ_TM, PROJ_TN)
    p_aux = norm_matmul(x, g_pre, w_aux[None], 0, w_aux.shape[1], F32, PROJ_TM, 384)
    k_idx_t = p_aux[:, pool_width:pool_width + IDX_HEAD_DIM].astype(BF16).T
    kt2 = jnp.concatenate([k_idx_t, k_idx_t], axis=0)
    mask = indexer_mask(p_main, p_aux, kt2, min(TOPK_MAX, s_len // 4))
    attn = masked_attention(p_main, mask, bias_tables, n_heads)
    pooled = multiscale_pool(p_aux, pool_w.astype(BF16), pool_scale)
    return mm_norm_res([attn, pooled], w_o16, layer, g_post, x, OUT_TM, _out_tk(attn.shape[1]))


def _odd_mixer(x, g_pre, g_post, w_in16, w_rest, w_o16, layer, conv_w, a_log, dt_bias, norm_w):
    s_len, d = x.shape
    n_kheads = d // 128
    n_vheads = 2 * n_kheads
    qk_w = n_kheads * GDN_DK
    v_w = n_vheads * GDN_DV
    conv_width = 2 * qk_w + v_w
    main_w = _odd_main_width(d)
    assert main_w == conv_width + v_w
    w_ba = jnp.concatenate([w_rest, jnp.zeros((d, 128 - 2 * n_vheads), F32)], axis=1).astype(BF16)
    p_main = norm_matmul(x, g_pre, w_in16, layer, main_w, BF16, PROJ_TM, PROJ_TN)
    ba = norm_matmul(x, g_pre, w_ba[None], 0, 128, F32, PROJ_TM, 128)
    qkv = conv_silu_norm(p_main, conv_w, conv_width, qk_w)
    u, w, qd, kdt, intra, dl = gdn_prepare(qkv, ba, a_log, dt_bias, n_kheads, n_vheads)
    o = gdn_scan(u, w, qd, kdt, intra, dl, p_main, norm_w, conv_width // (SCAN_HG * GDN_DV))
    return mm_norm_res([o], w_o16, layer, g_post, x, OUT_TM, _out_tk(v_w))


def _ffn(x, g_pre, g_post, w_gate16, w_up16, w_down16, layer):
    return ffn(x, g_pre, g_post, w_gate16, w_up16, w_down16, layer, FFN_TM, FFN_TF)


def kernel(x, rel_bias, even_w_in, even_w_o, pool_w, pool_scale, odd_w_in, conv_w, a_log, dt_bias,
           gdn_norm_w, odd_w_o, ffn_w_gate, ffn_w_up, ffn_w_down, mix_pre_g, mix_post_g, ffn_pre_g, ffn_post_g):
    b, s_len, d = x.shape
    depth = ffn_w_gate.shape[0]
    bias_tables = _bias_tables(rel_bias)
    even_in16 = even_w_in[:, :, :_even_main_width(d)].astype(BF16)
    odd_in16 = odd_w_in[:, :, :_odd_main_width(d)].astype(BF16)
    even_o16, odd_o16 = cast_bf16(even_w_o), cast_bf16(odd_w_o)
    gate16, up16, down16 = cast_bf16(ffn_w_gate), cast_bf16(ffn_w_up), cast_bf16(ffn_w_down)
    outs = []
    for bi in range(b):
        h = x[bi]
        for layer in range(depth):
            i = layer // 2
            if layer % 2 == 0:
                h = _even_mixer(h, mix_pre_g[layer], mix_post_g[layer], even_in16, even_w_in[i, :, _even_main_width(d):], even_o16, i,
                                pool_w[i], pool_scale[i], bias_tables)
            else:
                h = _odd_mixer(h, mix_pre_g[layer], mix_post_g[layer], odd_in16, odd_w_in[i, :, _odd_main_width(d):], odd_o16, i,
                               conv_w[i], a_log[i], dt_bias[i], gdn_norm_w[i])
            h = _ffn(h, ffn_pre_g[layer], ffn_post_g[layer], gate16, up16, down16, layer)
        outs.append(h)
    return jnp.stack(outs)
```

```python
import functools
import math

import jax
import jax.numpy as jnp
from jax import lax
from jax.experimental import pallas as pl
from jax.experimental.pallas import tpu as pltpu

F32 = jnp.float32
BF16 = jnp.bfloat16
I32 = jnp.int32

RMS_EPS = 1e-6
NEG_INF = -1e30
INT_MIN = -2 ** 31
LOG2E = 1.4426950408889634

VMEM_LIMIT_BYTES = 56 * 1024 * 1024

ATTN_HEAD_DIM = 128
N_IDX_HEADS = 16
IDX_HEAD_DIM = 64
TOPK_MAX = 256
N_BUCKETS = 32
MAX_DISTANCE = 128
POOL_WINDOWS = (2, 4, 8, 16)
GDN_DK = 128
GDN_DV = 128
CONV_K = 4
CHUNK = 64


def _params(*sem):
    return pltpu.CompilerParams(dimension_semantics=sem, vmem_limit_bytes=VMEM_LIMIT_BYTES)


def _dot(a, b):
    return jnp.dot(a, b, preferred_element_type=F32)


def _dot_nt(a, b):
    return lax.dot_general(a, b, (((1,), (1,)), ((), ())), preferred_element_type=F32)


def _to_u32(x):
    return lax.bitcast_convert_type(x ^ INT_MIN, jnp.uint32)


def _to_i32(x):
    return lax.bitcast_convert_type(x, I32)


def _rms(x, g):
    ms = jnp.mean(x * x, axis=-1, keepdims=True)
    return x * lax.rsqrt(ms + RMS_EPS) * g


def _norm_matmul_kernel(x_ref, g_ref, w_ref, o_ref, hn_ref):
    @pl.when(pl.program_id(1) == 0)
    def _():
        hn_ref[...] = _rms(x_ref[...], g_ref[...]).astype(BF16)

    o_ref[...] = _dot(hn_ref[...], w_ref[...]).astype(o_ref.dtype)


def norm_matmul(x, g, w, layer, n, out_dtype, tm, tn):
    m, k = x.shape
    assert m % tm == 0 and n % tn == 0 and n <= w.shape[2]
    return pl.pallas_call(
        _norm_matmul_kernel,
        grid=(m // tm, n // tn),
        in_specs=[
            pl.BlockSpec((tm, k), lambda i, j: (i, 0)),
            pl.BlockSpec((1, k), lambda i, j: (0, 0)),
            pl.BlockSpec((None, k, tn), lambda i, j: (layer, 0, j)),
        ],
        out_specs=pl.BlockSpec((tm, tn), lambda i, j: (i, j)),
        out_shape=jax.ShapeDtypeStruct((m, n), out_dtype),
        scratch_shapes=[pltpu.VMEM((tm, k), BF16)],
        compiler_params=_params("parallel", "arbitrary"),
        name="norm_matmul",
    )(x, g.reshape(1, k), w)


def _norm_swiglu_kernel(x_ref, g_ref, wg_ref, wu_ref, o_ref, hn_ref):
    @pl.when(pl.program_id(1) == 0)
    def _():
        hn_ref[...] = _rms(x_ref[...], g_ref[...]).astype(BF16)

    hn = hn_ref[...]
    a = _dot(hn, wg_ref[...])
    b = _dot(hn, wu_ref[...])
    o_ref[...] = (a * jax.nn.sigmoid(a) * b).astype(o_ref.dtype)


def norm_swiglu(x, g, wg, wu, layer, tm, tn):
    m, k = x.shape
    n = wg.shape[2]
    assert m % tm == 0 and n % tn == 0
    return pl.pallas_call(
        _norm_swiglu_kernel,
        grid=(m // tm, n // tn),
        in_specs=[
            pl.BlockSpec((tm, k), lambda i, j: (i, 0)),
            pl.BlockSpec((1, k), lambda i, j: (0, 0)),
            pl.BlockSpec((None, k, tn), lambda i, j: (layer, 0, j)),
            pl.BlockSpec((None, k, tn), lambda i, j: (layer, 0, j)),
        ],
        out_specs=pl.BlockSpec((tm, tn), lambda i, j: (i, j)),
        out_shape=jax.ShapeDtypeStruct((m, n), BF16),
        scratch_shapes=[pltpu.VMEM((tm, k), BF16)],
        compiler_params=_params("parallel", "arbitrary"),
        name="norm_swiglu",
    )(x, g.reshape(1, k), wg, wu)


def _ffn_kernel(x_ref, g_ref, wg_ref, wu_ref, wd_ref, gp_ref, o_ref, hn_ref, acc_ref, *, nj):
    j = pl.program_id(1)

    @pl.when(j == 0)
    def _():
        hn_ref[...] = _rms(x_ref[...], g_ref[...]).astype(BF16)
        acc_ref[...] = jnp.zeros_like(acc_ref)

    hn = hn_ref[...]
    a = _dot(hn, wg_ref[...])
    b = _dot(hn, wu_ref[...])
    act = (a * jax.nn.sigmoid(a) * b).astype(BF16)
    acc_ref[...] += _dot(act, wd_ref[...])

    @pl.when(j == nj - 1)
    def _():
        o_ref[...] = x_ref[...] + _rms(acc_ref[...], gp_ref[...])


def ffn(x, g_pre, g_post, wg, wu, wd, layer, tm, tf):
    m, d = x.shape
    ff = wg.shape[2]
    assert m % tm == 0 and ff % tf == 0
    nj = ff // tf
    return pl.pallas_call(
        functools.partial(_ffn_kernel, nj=nj),
        grid=(m // tm, nj),
        in_specs=[
            pl.BlockSpec((tm, d), lambda i, j: (i, 0)),
            pl.BlockSpec((1, d), lambda i, j: (0, 0)),
            pl.BlockSpec((None, d, tf), lambda i, j: (layer, 0, j)),
            pl.BlockSpec((None, d, tf), lambda i, j: (layer, 0, j)),
            pl.BlockSpec((None, tf, d), lambda i, j: (layer, j, 0)),
            pl.BlockSpec((1, d), lambda i, j: (0, 0)),
        ],
        out_specs=pl.BlockSpec((tm, d), lambda i, j: (i, 0)),
        out_shape=jax.ShapeDtypeStruct((m, d), F32),
        scratch_shapes=[pltpu.VMEM((tm, d), BF16), pltpu.VMEM((tm, d), F32)],
        compiler_params=_params("parallel", "arbitrary"),
        name="ffn",
    )(x, g_pre.reshape(1, d), wg, wu, wd, g_post.reshape(1, d))


def _mm_norm_res_kernel(*refs, n_seg, nk):
    a_refs = refs[:n_seg]
    w_refs = refs[n_seg:2 * n_seg]
    g_ref, r_ref, o_ref, acc_ref = refs[2 * n_seg:]
    kk = pl.program_id(1)

    @pl.when(kk == 0)
    def _():
        acc_ref[...] = jnp.zeros_like(acc_ref)

    part = _dot(a_refs[0][...], w_refs[0][...])
    for s in range(1, n_seg):
        part = part + _dot(a_refs[s][...], w_refs[s][...])
    acc_ref[...] += part

    @pl.when(kk == nk - 1)
    def _():
        o_ref[...] = r_ref[...] + _rms(acc_ref[...], g_ref[...])


def mm_norm_res(a_list, w, layer, g, res, tm, tk):
    n_seg = len(a_list)
    m, ka = a_list[0].shape
    n = w.shape[2]
    assert w.shape[1] == n_seg * ka and m % tm == 0 and ka % tk == 0
    nk = ka // tk
    a_specs = [pl.BlockSpec((tm, tk), lambda i, k: (i, k)) for _ in range(n_seg)]
    w_specs = [pl.BlockSpec((None, tk, n), functools.partial(lambda i, k, s: (layer, k + s * nk, 0), s=s))
               for s in range(n_seg)]
    return pl.pallas_call(
        functools.partial(_mm_norm_res_kernel, n_seg=n_seg, nk=nk),
        grid=(m // tm, nk),
        in_specs=a_specs + w_specs + [
            pl.BlockSpec((1, n), lambda i, k: (0, 0)),
            pl.BlockSpec((tm, n), lambda i, k: (i, 0)),
        ],
        out_specs=pl.BlockSpec((tm, n), lambda i, k: (i, 0)),
        out_shape=jax.ShapeDtypeStruct((m, n), F32),
        scratch_shapes=[pltpu.VMEM((tm, n), F32)],
        compiler_params=_params("parallel", "arbitrary"),
        name="mm_norm_res",
    )(*a_list, *([w] * n_seg), g.reshape(1, n), res)


CAST_BLOCK_BYTES = 4 * 1024 * 1024


def _cast_kernel(x_ref, o_ref):
    o_ref[...] = x_ref[...].astype(o_ref.dtype)


def cast_bf16(w):
    layers, k, n = w.shape
    rows = max(r for r in range(16, k + 1, 16) if k % r == 0 and r * n * 4 <= CAST_BLOCK_BYTES)
    return pl.pallas_call(
        _cast_kernel,
        grid=(layers, k // rows),
        in_specs=[pl.BlockSpec((None, rows, n), lambda l, i: (l, i, 0))],
        out_specs=pl.BlockSpec((None, rows, n), lambda l, i: (l, i, 0)),
        out_shape=jax.ShapeDtypeStruct(w.shape, BF16),
        compiler_params=_params("parallel", "parallel"),
        name="cast_bf16",
    )(w)


IDX_TQ = 128
IDX_TK = 512
IDX_ROWS = 128
IDX_STEPS_PER_CHECK = 4


def _indexer_kernel(qi_ref, aux_ref, kt_ref, mask_ref, key_ref, qh_ref, wb_ref, *, topk, w_scale):
    tq, tk, tr = IDX_TQ, IDX_TK, IDX_ROWS
    nj = tk // 128
    s_len = mask_ref.shape[1]
    qb = pl.program_id(0)
    n_chunks = ((qb + 1) * tq - 1) // tk + 1

    lane = lax.broadcasted_iota(I32, (tq, 128), 1)
    aux = aux_ref[...]
    for h in range(N_IDX_HEADS):
        pair = qi_ref[:, (h // 2) * 128:(h // 2 + 1) * 128]
        keep = (lane < IDX_HEAD_DIM) if h % 2 == 0 else (lane >= IDX_HEAD_DIM)
        qh_ref[h] = jnp.where(keep, pair, jnp.zeros_like(pair))
        wcol = aux[:, IDX_HEAD_DIM + h:IDX_HEAD_DIM + h + 1] * w_scale
        wb_ref[h] = jnp.broadcast_to(wcol, (tq, 128))

    lane_r = lax.broadcasted_iota(I32, (tr, 128), 1)
    row_r = lax.broadcasted_iota(I32, (tr, 128), 0)

    def score_chunk(c, tops):
        col0 = pl.multiple_of(c * tk, tk)
        kt = kt_ref[:, pl.ds(col0, tk)]
        new_tops = []
        for r in range(tq // tr):
            rows = slice(r * tr, (r + 1) * tr)
            accs = [jnp.zeros((tr, 128), F32) for _ in range(nj)]
            for h in range(N_IDX_HEADS):
                d = _dot(qh_ref[h, rows], kt)
                wb = wb_ref[h, rows]
                for j in range(nj):
                    accs[j] = accs[j] + wb * jnp.maximum(d[:, j * 128:(j + 1) * 128], 0.0)
            q_pos = qb * tq + r * tr + row_r
            m1, m2 = tops[r]
            for j in range(nj):
                bits = pltpu.bitcast(accs[j], I32)
                skey = bits ^ ((bits >> 31) & 0x7FFFFFFF)
                skey = jnp.where(col0 + j * 128 + lane_r <= q_pos, skey, INT_MIN)
                key_ref[rows, pl.ds(pl.multiple_of(col0 + j * 128, 128), 128)] = skey
                m2 = jnp.maximum(m2, jnp.minimum(m1, skey))
                m1 = jnp.maximum(m1, skey)
            new_tops.append((m1, m2))
        return tuple(new_tops)

    lowest = jnp.full((tr, 128), INT_MIN, I32)
    tops = lax.fori_loop(0, n_chunks, score_chunk, tuple((lowest, lowest) for _ in range(tq // tr)))
    m1 = jnp.concatenate([t[0] for t in tops], axis=0)
    m2 = jnp.concatenate([t[1] for t in tops], axis=0)
    assert topk <= 2 * 128 and 127 + s_len // 128 < topk
    top = jnp.max(m2, axis=-1, keepdims=True)
    n_top = jnp.sum(jnp.where(m2 == top, 1, 0), axis=-1, keepdims=True)
    below = jnp.max(jnp.where(m2 == top, INT_MIN, m2), axis=-1, keepdims=True)
    hi = _to_u32(jnp.where(n_top >= 2, top, below))
    lo = _to_u32(jnp.min(m2, axis=-1, keepdims=True))
    n_iter = 32 - jnp.max(lax.clz(_to_i32(hi - lo)))

    def count_ge(cand):
        cand_b = jnp.broadcast_to(cand, (tq, 128))

        def body(c, cnt):
            blk = key_ref[:, pl.ds(pl.multiple_of(c * tk, tk), tk)]
            for j in range(nj):
                cnt = cnt + jnp.where(blk[:, j * 128:(j + 1) * 128] >= cand_b, 1, 0)
            return cnt

        cnt = lax.fori_loop(0, n_chunks, body, jnp.zeros((tq, 128), I32))
        return jnp.sum(cnt, axis=-1, keepdims=True)

    def not_settled(state):
        i, _, _, _, settled = state
        return jnp.logical_and(i < n_iter, settled == 0)

    def bisect(state):
        i, lo, hi, n_lo, _ = state
        for _ in range(IDX_STEPS_PER_CHECK):
            width = hi - lo
            mid = lo + (width >> 1) + (width & 1)
            n_mid = count_ge(_to_i32(mid) ^ INT_MIN)
            ok = n_mid >= topk
            lo = jnp.where(ok, mid, lo)
            hi = jnp.where(ok, hi, jnp.where(width > 0, mid - 1, hi))
            n_lo = jnp.where(ok, n_mid, n_lo)
        settled = jnp.min(jnp.where(jnp.logical_or(n_lo == topk, hi == lo), 1, 0))
        return i + IDX_STEPS_PER_CHECK, lo, hi, n_lo, settled

    unknown = jnp.full((tq, 1), s_len + 1, I32)
    _, tau_u, _, _, _ = lax.while_loop(not_settled, bisect, (jnp.int32(0), lo, hi, unknown, jnp.int32(0)))
    tau_u = _to_i32(tau_u)
    tau = jnp.broadcast_to(jnp.maximum(tau_u ^ INT_MIN, INT_MIN + 1), (tq, 128))

    neg = jnp.full((tq, tk), NEG_INF, F32).astype(BF16)

    def write_chunk(c, carry):
        col0 = pl.multiple_of(c * tk, tk)

        @pl.when(c < n_chunks)
        def _():
            blk = key_ref[:, pl.ds(col0, tk)]
            parts = [jnp.where(blk[:, j * 128:(j + 1) * 128] >= tau, 0.0, NEG_INF) for j in range(nj)]
            mask_ref[:, pl.ds(col0, tk)] = jnp.concatenate(parts, axis=-1).astype(BF16)

        @pl.when(c >= n_chunks)
        def _():
            mask_ref[:, pl.ds(col0, tk)] = neg

        return carry

    lax.fori_loop(0, s_len // tk, write_chunk, 0)


def indexer_mask(p_main, p_aux, kt2, topk):
    s_len = p_main.shape[0]
    tq = IDX_TQ
    w_scale = (N_IDX_HEADS ** -0.5) * (IDX_HEAD_DIM ** -0.5)
    return pl.pallas_call(
        functools.partial(_indexer_kernel, topk=topk, w_scale=w_scale),
        grid=(s_len // tq,),
        in_specs=[
            pl.BlockSpec((tq, 1024), lambda i: (i, 3)),
            pl.BlockSpec((tq, 128), lambda i: (i, 8)),
            pl.BlockSpec((128, s_len), lambda i: (0, 0)),
        ],
        out_specs=pl.BlockSpec((tq, s_len), lambda i: (i, 0)),
        out_shape=jax.ShapeDtypeStruct((s_len, s_len), BF16),
        scratch_shapes=[
            pltpu.VMEM((tq, s_len), I32),
            pltpu.VMEM((N_IDX_HEADS, tq, 128), BF16),
            pltpu.VMEM((N_IDX_HEADS, tq, 128), F32),
        ],
        compiler_params=_params("parallel"),
        name="indexer_mask",
    )(p_main, p_aux, kt2)


ATT_T = 512
ATT_RC = 32
ATT_SUB = 128


def _attention_kernel(qb_tab, kb_tab, q_ref, k_ref, v_ref, mask_ref, tz_ref, o_ref,
                      qs_ref, mf_ref, p_ref, m_ref, l_ref, acc_ref, *, n_heads, scale):
    step = pl.program_id(0)
    qb = qb_tab[step]
    kb = kb_tab[step]
    dh, t, rc, sub = ATTN_HEAD_DIM, ATT_T, ATT_RC, ATT_SUB
    nsub = t // sub

    @pl.when(kb == 0)
    def _():
        qs_ref[...] = (q_ref[...].astype(F32) * (scale * LOG2E)).astype(BF16)
        m_ref[...] = jnp.full_like(m_ref, NEG_INF)
        l_ref[...] = jnp.zeros_like(l_ref)
        acc_ref[...] = jnp.zeros_like(acc_ref)

    mf_ref[...] = mask_ref[...].astype(F32)
    prev_flag = jnp.where(kb == qb - 1, 1.0, 0.0).astype(F32)

    def run(diag):
        sls = [slice(h * dh, (h + 1) * dh) for h in range(n_heads)]
        s_next = _dot_nt(qs_ref[:, sls[0]], k_ref[:, sls[0]])
        for h in range(n_heads):
            sl = sls[h]
            s = s_next
            if h + 1 < n_heads:
                s_next = _dot_nt(qs_ref[:, sls[h + 1]], k_ref[:, sls[h + 1]])
            alphas = []
            for r in range(t // rc):
                rows = slice(r * rc, (r + 1) * rc)
                blk, r0 = divmod(r * rc, sub)
                sr = s[rows] + mf_ref[rows, :]
                parts = [sr[:, b * sub:(b + 1) * sub] for b in range(nsub)]
                if diag:
                    parts[blk] = parts[blk] + tz_ref[0, h, r0:r0 + rc, :]
                    if blk >= 1:
                        parts[blk - 1] = parts[blk - 1] + tz_ref[1, h, r0:r0 + rc, :]
                elif blk == 0:
                    parts[nsub - 1] = parts[nsub - 1] + prev_flag * tz_ref[1, h, r0:r0 + rc, :]
                m_prev = m_ref[h, rows]
                m_new = jnp.maximum(m_prev, jnp.max(functools.reduce(jnp.maximum, parts), axis=-1, keepdims=True))
                alpha = jnp.exp2(m_prev - m_new)
                ps = [jnp.exp2(pb - m_new) for pb in parts]
                l_ref[h, rows] = alpha * l_ref[h, rows] + jnp.sum(functools.reduce(jnp.add, ps), axis=-1, keepdims=True)
                m_ref[h, rows] = m_new
                p_ref[h, rows, :] = jnp.concatenate(ps, axis=-1).astype(BF16)
                alphas.append(alpha)
            pv = _dot(p_ref[h], v_ref[:, sl])
            for r in range(t // rc):
                rows = slice(r * rc, (r + 1) * rc)
                acc_ref[h, rows] = alphas[r] * acc_ref[h, rows] + pv[rows]

    @pl.when(kb != qb)
    def _():
        run(False)

    @pl.when(kb == qb)
    def _():
        run(True)
        for h in range(n_heads):
            o_ref[:, h * dh:(h + 1) * dh] = (acc_ref[h] / l_ref[h]).astype(o_ref.dtype)


def masked_attention(p_main, mask, bias_tables, n_heads):
    s_len = p_main.shape[0]
    t = ATT_T
    nb = s_len // t
    pairs = [(q, k) for q in range(nb) for k in range(q + 1)]
    qb_tab = jnp.array([p[0] for p in pairs], I32)
    kb_tab = jnp.array([p[1] for p in pairs], I32)
    width = n_heads * ATTN_HEAD_DIM
    grid_spec = pltpu.PrefetchScalarGridSpec(
        num_scalar_prefetch=2,
        grid=(len(pairs),),
        in_specs=[
            pl.BlockSpec((t, width), lambda s, qt, kt: (qt[s], 0)),
            pl.BlockSpec((t, width), lambda s, qt, kt: (kt[s], 1)),
            pl.BlockSpec((t, width), lambda s, qt, kt: (kt[s], 2)),
            pl.BlockSpec((t, t), lambda s, qt, kt: (qt[s], kt[s])),
            pl.BlockSpec((2, n_heads, ATT_SUB, ATT_SUB), lambda s, qt, kt: (0, 0, 0, 0)),
        ],
        out_specs=pl.BlockSpec((t, width), lambda s, qt, kt: (qt[s], 0)),
        scratch_shapes=[
            pltpu.VMEM((t, width), BF16),
            pltpu.VMEM((t, t), F32),
            pltpu.VMEM((n_heads, t, t), BF16),
            pltpu.VMEM((n_heads, t, 128), F32),
            pltpu.VMEM((n_heads, t, 128), F32),
            pltpu.VMEM((n_heads, t, ATTN_HEAD_DIM), F32),
        ],
    )
    return pl.pallas_call(
        functools.partial(_attention_kernel, n_heads=n_heads, scale=ATTN_HEAD_DIM ** -0.5),
        grid_spec=grid_spec,
        out_shape=jax.ShapeDtypeStruct((s_len, width), BF16),
        compiler_params=_params("arbitrary"),
        name="masked_attention",
    )(qb_tab, kb_tab, p_main, p_main, p_main, mask, bias_tables)


def _t5_bucket(dist):
    max_exact = N_BUCKETS // 2
    d = jnp.maximum(dist, 0)
    log_ratio = jnp.log(jnp.maximum(d, 1).astype(F32) / max_exact) / math.log(MAX_DISTANCE / max_exact)
    large = jnp.minimum(max_exact + (log_ratio * (N_BUCKETS - max_exact)).astype(I32), N_BUCKETS - 1)
    return jnp.where(d < max_exact, d, large)


def _bias_tables(rel_bias):
    assert MAX_DISTANCE <= ATT_SUB
    i = jnp.arange(ATT_SUB, dtype=I32)[:, None]
    j = jnp.arange(ATT_SUB, dtype=I32)[None, :]
    bucket = _t5_bucket(jnp.stack([i - j, i - j + ATT_SUB]))[:, None]
    rel = rel_bias.astype(F32)
    far = rel[N_BUCKETS - 1]
    tables = jnp.zeros((2, rel_bias.shape[1], ATT_SUB, ATT_SUB), F32)
    for b in range(N_BUCKETS):
        tables = jnp.where(bucket == b, (rel[b] - far)[None, :, None, None], tables)
    return tables * LOG2E


POOL_TR = 512
POOL_G = 256
HALO = 16


def _pool_kernel(x_ref, prev_ref, w_ref, sc_ref, o_ref):
    i = pl.program_id(0)
    g = pl.program_id(1)
    tr = POOL_TR
    cur = x_ref[...]
    prev = jnp.where(i > 0, prev_ref[...], 0.0)
    ext = jnp.concatenate([prev, cur], axis=0)
    sums = []
    run = ext
    for shift in (1, 2, 4, 8):
        run = run + pltpu.roll(run, shift, 0)
        sums.append(run[HALO:])
    win_sum = jnp.where(g == 0, sums[0], jnp.where(g == 1, sums[1], jnp.where(g == 2, sums[2], sums[3])))
    win = jnp.left_shift(jnp.int32(2), g)
    t_pos = i * tr + lax.broadcasted_iota(I32, (tr, 1), 0)
    count = jnp.minimum(t_pos + 1, win).astype(F32)
    pooled = win_sum / count - cur
    o_ref[...] = (_dot(pooled.astype(BF16), w_ref[0]) * sc_ref[...]).astype(o_ref.dtype)


def multiscale_pool(p_aux, pool_w, pool_scale):
    s_len = p_aux.shape[0]
    tr, gw = POOL_TR, POOL_G
    ng = len(POOL_WINDOWS)
    return pl.pallas_call(
        _pool_kernel,
        grid=(s_len // tr, ng),
        in_specs=[
            pl.BlockSpec((tr, gw), lambda i, g: (i, g)),
            pl.BlockSpec((HALO, gw), lambda i, g: (jnp.maximum(i * (tr // HALO) - 1, 0), g)),
            pl.BlockSpec((1, gw, gw), lambda i, g: (g, 0, 0)),
            pl.BlockSpec((1, gw), lambda i, g: (0, g)),
        ],
        out_specs=pl.BlockSpec((tr, gw), lambda i, g: (i, g)),
        out_shape=jax.ShapeDtypeStruct((s_len, ng * gw), BF16),
        compiler_params=_params("parallel", "parallel"),
        name="multiscale_pool",
    )(p_aux, p_aux, pool_w, pool_scale.reshape(1, ng * gw))


def _split3(x):
    hi = x.astype(BF16)
    r1 = x - hi.astype(F32)
    mid = r1.astype(BF16)
    lo = (r1 - mid.astype(F32)).astype(BF16)
    return hi, mid, lo


def _dot_exact_lhs(a_bf16, x):
    hi, mid, lo = _split3(x)
    return _dot(a_bf16, lo) + _dot(a_bf16, mid) + _dot(a_bf16, hi)


def _dot_exact_rhs(x, b_bf16):
    hi, mid, lo = _split3(x)
    return _dot(lo, b_bf16) + _dot(mid, b_bf16) + _dot(hi, b_bf16)


def _softplus(x):
    return jnp.maximum(x, 0.0) + jnp.log(1.0 + jnp.exp(-jnp.abs(x)))


PREP_CHUNKS = 2


def _gdn_prep_kernel(x_ref, prev_ref, cw_ref, ba_ref, alog_ref, dtb_ref,
                     u_ref, w_ref, qd_ref, kdt_ref, intra_ref, dl_ref, ext_ref, *, n_vheads):
    c = CHUNK
    dk = GDN_DK
    assert 2 * c == 128 and dk == 128 and GDN_DV == 128
    scale = dk ** -0.5
    n_kheads = n_vheads // 2
    step = pl.program_id(0)
    for sb in range(2 * n_kheads + n_vheads):
        cols = slice(sb * 128, (sb + 1) * 128)
        ext_ref[sb, :HALO, :] = jnp.where(step > 0, prev_ref[:, cols].astype(F32), 0.0)
        ext_ref[sb, HALO:, :] = x_ref[:, cols].astype(F32)
    cw = cw_ref[...]

    def conv_head(sb, ci, out_scale):
        cols = slice(sb * 128, (sb + 1) * 128)
        base = HALO + ci * c
        y = ext_ref[sb, base:base + c, :] * cw[CONV_K - 1:CONV_K, cols]
        for tap in range(CONV_K - 1):
            lo = base - (CONV_K - 1 - tap)
            y = y + ext_ref[sb, lo:lo + c, :] * cw[tap:tap + 1, cols]
        y = y / (1.0 + jnp.exp2(y * (-LOG2E)))
        if out_scale is not None:
            y = y * (lax.rsqrt(jnp.sum(y * y, axis=-1, keepdims=True) + RMS_EPS) * out_scale)
        return y

    row = lax.broadcasted_iota(I32, (c, c), 0)
    col = lax.broadcasted_iota(I32, (c, c), 1)
    lower = jnp.where(row >= col, 1.0, 0.0).astype(BF16)
    upper = jnp.where(row <= col, 1.0, 0.0).astype(BF16)
    row_w = lax.broadcasted_iota(I32, (c, 2 * c), 0)
    lane_w = lax.broadcasted_iota(I32, (c, 2 * c), 1)
    first = lane_w < c
    col_w = jnp.where(first, lane_w, lane_w - c)
    tril_w = row_w >= col_w
    strict_w = row_w > col_w
    eye_w = jnp.where(row_w == col_w, 1.0, 0.0).astype(F32)

    def block_diag(wide16):
        zero = jnp.zeros_like(wide16)
        return jnp.concatenate([jnp.where(first, wide16, zero), jnp.where(first, zero, wide16)], axis=0)

    pairs = range(n_vheads // 2)
    pws, t_invs, rhss, where = [], [], [], []
    for ci in range(PREP_CHUNKS):
        rows = slice(ci * c, (ci + 1) * c)
        ba = ba_ref[rows, :]
        beta_all = jax.nn.sigmoid(ba)
        g_all = -jnp.exp(alog_ref[...]) * _softplus(ba + dtb_ref[...])
        gc_all = _dot_exact_lhs(lower, g_all)
        gct_all = _dot_exact_rhs(g_all.T, upper)
        for p in pairs:
            kh = conv_head(n_kheads + p, ci, 1.0)
            qh = conv_head(p, ci, scale)
            kh_t = kh.T
            kh16 = kh.astype(BF16)
            kk16 = jnp.concatenate([kh16, kh16], axis=0)
            both = _dot_nt(jnp.concatenate([kh16, qh.astype(BF16)], axis=0), kk16)
            gram, qk = both[:c], both[c:]
            hv = (2 * p, 2 * p + 1)
            beta = [jnp.broadcast_to(beta_all[:, h:h + 1], (c, 2 * c)) for h in hv]
            gcol = [jnp.broadcast_to(gc_all[:, n_vheads + h:n_vheads + h + 1], (c, 2 * c)) for h in hv]
            grow = [gct_all[n_vheads + h:n_vheads + h + 1, :] for h in hv]
            grow_w = jnp.concatenate(grow, axis=1)
            gcol_w = jnp.where(first, gcol[0], gcol[1])
            decay_w = jnp.where(tril_w, jnp.exp(jnp.where(tril_w, gcol_w - grow_w, 0.0)), 0.0)
            a_w = jnp.where(strict_w, -(jnp.where(first, beta[0], beta[1]) * gram * decay_w), 0.0)
            intra_ref[ci, p] = jnp.where(tril_w, qk * decay_w, 0.0).astype(BF16)
            rhs = []
            for i, h in enumerate(hv):
                eg = jnp.exp(gcol[i])
                g_last = grow[i][:, c - 1:c]
                sl = slice(h * GDN_DV, (h + 1) * GDN_DV)
                qd_ref[rows, sl] = (qh * eg).astype(BF16)
                kdt_ref[ci, h] = (kh_t * jnp.exp(g_last - grow[i])).astype(BF16)
                dl_ref[ci, h] = jnp.broadcast_to(jnp.exp(g_last), (1, 128))
                vh = conv_head(2 * n_kheads + h, ci, None)
                rhs.append(jnp.concatenate([vh * beta[i], kh * (beta[i] * eg)], axis=-1).astype(BF16))
            pws.append(a_w)
            t_invs.append(eye_w + a_w)
            rhss.append(jnp.concatenate(rhs, axis=0))
            where.append((rows, p))
    n_levels = int(math.log2(c)) - 1
    pws = [_dot(a.astype(BF16), block_diag(a.astype(BF16))) for a in pws]
    for level in range(n_levels):
        new_t, new_pw = [], []
        for t, pw in zip(t_invs, pws):
            pw16 = pw.astype(BF16)
            if level == n_levels - 1:
                new_t.append(t + _dot(t.astype(BF16), block_diag(pw16)))
            else:
                both = _dot(jnp.concatenate([t.astype(BF16), pw16], axis=0), block_diag(pw16))
                new_t.append(t + both[:c])
                new_pw.append(both[c:])
        t_invs, pws = new_t, new_pw
    for (rows, p), t_inv, rhs in zip(where, t_invs, rhss):
        t16 = t_inv.astype(BF16)
        zero = jnp.zeros_like(t16)
        lhs = jnp.concatenate([jnp.where(first, t16, zero), jnp.where(first, zero, t16)], axis=0)
        sol = _dot(lhs, rhs)
        for i, h in enumerate((2 * p, 2 * p + 1)):
            sl = slice(h * GDN_DV, (h + 1) * GDN_DV)
            u_ref[rows, sl] = sol[i * c:(i + 1) * c, :GDN_DV]
            w_ref[rows, sl] = sol[i * c:(i + 1) * c, GDN_DV:].astype(BF16)


def gdn_prepare(p_main, conv_w, ba, a_log, dt_bias, n_kheads, n_vheads):
    s_len = p_main.shape[0]
    nc = PREP_CHUNKS
    c = CHUNK * nc
    n = s_len // c
    v_w = n_vheads * GDN_DV
    conv_width = 2 * n_kheads * GDN_DK + v_w
    assert 2 * n_vheads <= 128 and n_vheads == 2 * n_kheads and c % HALO == 0
    pad = jnp.zeros((128 - 2 * n_vheads,), F32)
    alog_row = jnp.concatenate([jnp.zeros((n_vheads,), F32), a_log, pad]).reshape(1, 128)
    dtb_row = jnp.concatenate([jnp.zeros((n_vheads,), F32), dt_bias, pad]).reshape(1, 128)
    return pl.pallas_call(
        functools.partial(_gdn_prep_kernel, n_vheads=n_vheads),
        grid=(n,),
        in_specs=[
            pl.BlockSpec((c, conv_width), lambda i: (i, 0)),
            pl.BlockSpec((HALO, conv_width), lambda i: (jnp.maximum(i * (c // HALO) - 1, 0), 0)),
            pl.BlockSpec((CONV_K, conv_width), lambda i: (0, 0)),
            pl.BlockSpec((c, 128), lambda i: (i, 0)),
            pl.BlockSpec((1, 128), lambda i: (0, 0)),
            pl.BlockSpec((1, 128), lambda i: (0, 0)),
        ],
        out_specs=[
            pl.BlockSpec((c, v_w), lambda i: (i, 0)),
            pl.BlockSpec((c, v_w), lambda i: (i, 0)),
            pl.BlockSpec((c, v_w), lambda i: (i, 0)),
            pl.BlockSpec((nc, n_vheads, GDN_DK, CHUNK), lambda i: (i, 0, 0, 0)),
            pl.BlockSpec((nc, n_vheads // 2, CHUNK, 2 * CHUNK), lambda i: (i, 0, 0, 0)),
            pl.BlockSpec((nc, n_vheads, 1, 128), lambda i: (i, 0, 0, 0)),
        ],
        out_shape=[
            jax.ShapeDtypeStruct((s_len, v_w), F32),
            jax.ShapeDtypeStruct((s_len, v_w), BF16),
            jax.ShapeDtypeStruct((s_len, v_w), BF16),
            jax.ShapeDtypeStruct((n * nc, n_vheads, GDN_DK, CHUNK), BF16),
            jax.ShapeDtypeStruct((n * nc, n_vheads // 2, CHUNK, 2 * CHUNK), BF16),
            jax.ShapeDtypeStruct((n * nc, n_vheads, 1, 128), F32),
        ],
        scratch_shapes=[pltpu.VMEM((conv_width // 128, HALO + c, 128), F32)],
        compiler_params=_params("parallel"),
        name="gdn_prepare",
    )(p_main, p_main, conv_w, ba, alog_row, dtb_row)


SCAN_HG = 16
SCAN_CHUNKS = 4


def _gdn_scan_kernel(u_ref, w_ref, qd_ref, kdt_ref, intra_ref, dl_ref, z_ref, nw_ref, o_ref, state_ref):
    @pl.when(pl.program_id(1) == 0)
    def _():
        state_ref[...] = jnp.zeros_like(state_ref)

    nw = nw_ref[...]
    heads = range(SCAN_HG)
    sls = [slice(j * GDN_DV, (j + 1) * GDN_DV) for j in heads]
    first = lax.broadcasted_iota(I32, (CHUNK, 2 * CHUNK), 1) < CHUNK
    states = [state_ref[j] for j in heads]
    for ci in range(SCAN_CHUNKS):
        rows = slice(ci * CHUNK, (ci + 1) * CHUNK)
        s16 = [s.astype(BF16) for s in states]
        ws = [_dot(w_ref[rows, sls[j]], s16[j]) for j in heads]
        qs = [_dot(qd_ref[rows, sls[j]], s16[j]) for j in heads]
        v16 = [(u_ref[rows, sls[j]] - ws[j]).astype(BF16) for j in heads]
        upd = [_dot(kdt_ref[ci, j], v16[j]) for j in heads]
        outs = []
        for p in range(SCAN_HG // 2):
            pair = intra_ref[ci, p]
            zero = jnp.zeros_like(pair)
            v_pair = jnp.concatenate([v16[2 * p], v16[2 * p + 1]], axis=0)
            outs.append(qs[2 * p] + _dot(jnp.where(first, pair, zero), v_pair))
            outs.append(qs[2 * p + 1] + _dot(jnp.where(first, zero, pair), v_pair))
        states = [states[j] * dl_ref[ci, j] + upd[j] for j in heads]
        for j in heads:
            z = z_ref[rows, sls[j]].astype(F32)
            o_ref[rows, sls[j]] = (_rms(outs[j], nw) * (z * jax.nn.sigmoid(z))).astype(o_ref.dtype)
    for j in heads:
        state_ref[j] = states[j]


def gdn_scan(u, w, qd, kdt, intra, dl, p_main, norm_w, z_col_block):
    s_len, v_w = u.shape
    nc = SCAN_CHUNKS
    c = CHUNK * nc
    n = s_len // c
    hg = SCAN_HG
    gw = hg * GDN_DV
    return pl.pallas_call(
        _gdn_scan_kernel,
        grid=(v_w // gw, n),
        in_specs=[
            pl.BlockSpec((c, gw), lambda g, i: (i, g)),
            pl.BlockSpec((c, gw), lambda g, i: (i, g)),
            pl.BlockSpec((c, gw), lambda g, i: (i, g)),
            pl.BlockSpec((nc, hg, GDN_DK, CHUNK), lambda g, i: (i, g, 0, 0)),
            pl.BlockSpec((nc, hg // 2, CHUNK, 2 * CHUNK), lambda g, i: (i, g, 0, 0)),
            pl.BlockSpec((nc, hg, 1, 128), lambda g, i: (i, g, 0, 0)),
            pl.BlockSpec((c, gw), lambda g, i: (i, z_col_block + g)),
            pl.BlockSpec((1, GDN_DV), lambda g, i: (0, 0)),
        ],
        out_specs=pl.BlockSpec((c, gw), lambda g, i: (i, g)),
        out_shape=jax.ShapeDtypeStruct((s_len, v_w), BF16),
        scratch_shapes=[pltpu.VMEM((hg, GDN_DK, GDN_DV), F32)],
        compiler_params=_params("parallel", "arbitrary"),
        name="gdn_scan",
    )(u, w, qd, kdt, intra, dl, p_main, norm_w.reshape(1, GDN_DV))


PROJ_TM = 1024
PROJ_TN = 1024
FFN_TM = 512
FFN_TF = 512
OUT_TM = 512
OUT_TK_MAX = 1408


def _out_tk(k):
    return max(c for c in range(128, OUT_TK_MAX + 1, 128) if k % c == 0)


def _even_main_width(d):
    return 3 * (d // 2) + N_IDX_HEADS * IDX_HEAD_DIM


def _odd_main_width(d):
    n_kheads = d // 128
    return 2 * n_kheads * GDN_DK + 2 * (2 * n_kheads) * GDN_DV


def _even_mixer(x, g_pre, g_post, w_in16, w_rest, w_o16, layer, pool_w, pool_scale, bias_tables):
    s_len, d = x.shape
    attn_w = d // 2
    n_heads = attn_w // ATTN_HEAD_DIM
    idx_w = N_IDX_HEADS * IDX_HEAD_DIM
    pool_width = len(POOL_WINDOWS) * POOL_G
    main_w = _even_main_width(d)
    assert main_w == 3 * attn_w + idx_w == 4096 and pool_width == 1024
    small = IDX_HEAD_DIM + N_IDX_HEADS
    w_aux = jnp.concatenate([w_rest[:, small:], w_rest[:, :small],
                             jnp.zeros((d, 128 - small), F32)], axis=1).astype(BF16)
    p_main = norm_matmul(x, g_pre, w_in16, layer, main_w, BF16, PROJ_TM, PROJ_TN)
    p_aux = norm_matmul(x, g_pre, w_aux[None], 0, w_aux.shape[1], F32, PROJ_TM, w_aux.shape[1])
    k_idx_t = p_aux[:, pool_width:pool_width + IDX_HEAD_DIM].astype(BF16).T
    kt2 = jnp.concatenate([k_idx_t, k_idx_t], axis=0)
    mask = indexer_mask(p_main, p_aux, kt2, min(TOPK_MAX, s_len // 4))
    attn = masked_attention(p_main, mask, bias_tables, n_heads)
    pooled = multiscale_pool(p_aux, pool_w.astype(BF16), pool_scale)
    return mm_norm_res([attn, pooled], w_o16, layer, g_post, x, OUT_TM, _out_tk(attn.shape[1]))


def _odd_mixer(x, g_pre, g_post, w_in16, w_rest, w_o16, layer, conv_w, a_log, dt_bias, norm_w):
    s_len, d = x.shape
    n_kheads = d // 128
    n_vheads = 2 * n_kheads
    qk_w = n_kheads * GDN_DK
    v_w = n_vheads * GDN_DV
    conv_width = 2 * qk_w + v_w
    main_w = _odd_main_width(d)
    assert main_w == conv_width + v_w
    w_ba = jnp.concatenate([w_rest, jnp.zeros((d, 128 - 2 * n_vheads), F32)], axis=1).astype(BF16)
    p_main = norm_matmul(x, g_pre, w_in16, layer, main_w, BF16, PROJ_TM, PROJ_TN)
    ba = norm_matmul(x, g_pre, w_ba[None], 0, 128, F32, PROJ_TM, 128)
    u, w, qd, kdt, intra, dl = gdn_prepare(p_main, conv_w, ba, a_log, dt_bias, n_kheads, n_vheads)
    o = gdn_scan(u, w, qd, kdt, intra, dl, p_main, norm_w, conv_width // (SCAN_HG * GDN_DV))
    return mm_norm_res([o], w_o16, layer, g_post, x, OUT_TM, _out_tk(v_w))


def _ffn(x, g_pre, g_post, w_gate16, w_up16, w_down16, layer):
    return ffn(x, g_pre, g_post, w_gate16, w_up16, w_down16, layer, FFN_TM, FFN_TF)


def kernel(x, rel_bias, even_w_in, even_w_o, pool_w, pool_scale, odd_w_in, conv_w, a_log, dt_bias,
           gdn_norm_w, odd_w_o, ffn_w_gate, ffn_w_up, ffn_w_down, mix_pre_g, mix_post_g, ffn_pre_g, ffn_post_g):
    b, s_len, d = x.shape
    depth = ffn_w_gate.shape[0]
    bias_tables = _bias_tables(rel_bias)
    even_in16 = even_w_in[:, :, :_even_main_width(d)].astype(BF16)
    odd_in16 = odd_w_in[:, :, :_odd_main_width(d)].astype(BF16)
    even_o16, odd_o16 = cast_bf16(even_w_o), cast_bf16(odd_w_o)
    gate16, up16, down16 = cast_bf16(ffn_w_gate), cast_bf16(ffn_w_up), cast_bf16(ffn_w_down)
    xs = x.reshape(b * s_len, d)
    outs = []
    for bi in range(b):
        h = xs[bi * s_len:(bi + 1) * s_len]
        for layer in range(depth):
            i = layer // 2
            if layer % 2 == 0:
                h = _even_mixer(h, mix_pre_g[layer], mix_post_g[layer], even_in16, even_w_in[i, :, _even_main_width(d):], even_o16, i,
                                pool_w[i], pool_scale[i], bias_tables)
            else:
                h = _odd_mixer(h, mix_pre_g[layer], mix_post_g[layer], odd_in16, odd_w_in[i, :, _odd_main_width(d):], odd_o16, i,
                               conv_w[i], a_log[i], dt_bias[i], gdn_norm_w[i])
            h = _ffn(h, ffn_pre_g[layer], ffn_post_g[layer], gate16, up16, down16, layer)
        outs.append(h)
    return jnp.concatenate(outs, axis=0).reshape(b, s_len, d)
```

```python
import functools
import math

import jax
import jax.numpy as jnp
from jax import lax
from jax.experimental import pallas as pl
from jax.experimental.pallas import tpu as pltpu

F32 = jnp.float32
BF16 = jnp.bfloat16
I32 = jnp.int32

RMS_EPS = 1e-6
NEG_INF = -1e30
INT_MIN = -2 ** 31
LOG2E = 1.4426950408889634

VMEM_LIMIT_BYTES = 56 * 1024 * 1024

ATTN_HEAD_DIM = 128
N_IDX_HEADS = 16
IDX_HEAD_DIM = 64
TOPK_MAX = 256
N_BUCKETS = 32
MAX_DISTANCE = 128
POOL_WINDOWS = (2, 4, 8, 16)
GDN_DK = 128
GDN_DV = 128
CONV_K = 4
CHUNK = 64


def _params(*sem):
    return pltpu.CompilerParams(dimension_semantics=sem, vmem_limit_bytes=VMEM_LIMIT_BYTES)


def _dot(a, b):
    return jnp.dot(a, b, preferred_element_type=F32)


def _dot_nt(a, b):
    return lax.dot_general(a, b, (((1,), (1,)), ((), ())), preferred_element_type=F32)


def _to_u32(x):
    return lax.bitcast_convert_type(x ^ INT_MIN, jnp.uint32)


def _to_i32(x):
    return lax.bitcast_convert_type(x, I32)


def _rms(x, g):
    ms = jnp.mean(x * x, axis=-1, keepdims=True)
    return x * lax.rsqrt(ms + RMS_EPS) * g


def _norm_matmul_kernel(x_ref, g_ref, wt_ref, o_ref, hn_ref):
    @pl.when(pl.program_id(1) == 0)
    def _():
        hn_ref[...] = _rms(x_ref[...], g_ref[...]).astype(BF16)

    o_ref[...] = _dot_nt(hn_ref[...], wt_ref[...]).astype(o_ref.dtype)


def norm_matmul(x, g, wt, layer, n, out_dtype, tm, tn):
    m, k = x.shape
    assert m % tm == 0 and n % tn == 0 and n <= wt.shape[1] and wt.shape[2] == k
    return pl.pallas_call(
        _norm_matmul_kernel,
        grid=(m // tm, n // tn),
        in_specs=[
            pl.BlockSpec((tm, k), lambda i, j: (i, 0)),
            pl.BlockSpec((1, k), lambda i, j: (0, 0)),
            pl.BlockSpec((None, tn, k), lambda i, j: (layer, j, 0)),
        ],
        out_specs=pl.BlockSpec((tm, tn), lambda i, j: (i, j)),
        out_shape=jax.ShapeDtypeStruct((m, n), out_dtype),
        scratch_shapes=[pltpu.VMEM((tm, k), BF16)],
        compiler_params=_params("parallel", "arbitrary"),
        name="norm_matmul",
    )(x, g.reshape(1, k), wt)


def _norm_swiglu_kernel(x_ref, g_ref, wg_ref, wu_ref, o_ref, hn_ref):
    @pl.when(pl.program_id(1) == 0)
    def _():
        hn_ref[...] = _rms(x_ref[...], g_ref[...]).astype(BF16)

    hn = hn_ref[...]
    a = _dot(hn, wg_ref[...])
    b = _dot(hn, wu_ref[...])
    o_ref[...] = (a * jax.nn.sigmoid(a) * b).astype(o_ref.dtype)


def norm_swiglu(x, g, wg, wu, layer, tm, tn):
    m, k = x.shape
    n = wg.shape[2]
    assert m % tm == 0 and n % tn == 0
    return pl.pallas_call(
        _norm_swiglu_kernel,
        grid=(m // tm, n // tn),
        in_specs=[
            pl.BlockSpec((tm, k), lambda i, j: (i, 0)),
            pl.BlockSpec((1, k), lambda i, j: (0, 0)),
            pl.BlockSpec((None, k, tn), lambda i, j: (layer, 0, j)),
            pl.BlockSpec((None, k, tn), lambda i, j: (layer, 0, j)),
        ],
        out_specs=pl.BlockSpec((tm, tn), lambda i, j: (i, j)),
        out_shape=jax.ShapeDtypeStruct((m, n), BF16),
        scratch_shapes=[pltpu.VMEM((tm, k), BF16)],
        compiler_params=_params("parallel", "arbitrary"),
        name="norm_swiglu",
    )(x, g.reshape(1, k), wg, wu)


def _ffn_kernel(x_ref, g_ref, wg_ref, wu_ref, wd_ref, gp_ref, o_ref, hn_ref, acc_ref, *, nj):
    j = pl.program_id(1)

    @pl.when(j == 0)
    def _():
        hn_ref[...] = _rms(x_ref[...], g_ref[...]).astype(BF16)
        acc_ref[...] = jnp.zeros_like(acc_ref)

    hn = hn_ref[...]
    a = _dot(hn, wg_ref[...])
    b = _dot(hn, wu_ref[...])
    act = (a * jax.nn.sigmoid(a) * b).astype(BF16)
    acc_ref[...] += _dot(act, wd_ref[...])

    @pl.when(j == nj - 1)
    def _():
        o_ref[...] = x_ref[...] + _rms(acc_ref[...], gp_ref[...])


def ffn(x, g_pre, g_post, wg, wu, wd, layer, tm, tf):
    m, d = x.shape
    ff = wg.shape[2]
    assert m % tm == 0 and ff % tf == 0
    nj = ff // tf
    return pl.pallas_call(
        functools.partial(_ffn_kernel, nj=nj),
        grid=(m // tm, nj),
        in_specs=[
            pl.BlockSpec((tm, d), lambda i, j: (i, 0)),
            pl.BlockSpec((1, d), lambda i, j: (0, 0)),
            pl.BlockSpec((None, d, tf), lambda i, j: (layer, 0, j)),
            pl.BlockSpec((None, d, tf), lambda i, j: (layer, 0, j)),
            pl.BlockSpec((None, tf, d), lambda i, j: (layer, j, 0)),
            pl.BlockSpec((1, d), lambda i, j: (0, 0)),
        ],
        out_specs=pl.BlockSpec((tm, d), lambda i, j: (i, 0)),
        out_shape=jax.ShapeDtypeStruct((m, d), F32),
        scratch_shapes=[pltpu.VMEM((tm, d), BF16), pltpu.VMEM((tm, d), F32)],
        compiler_params=_params("parallel", "arbitrary"),
        name="ffn",
    )(x, g_pre.reshape(1, d), wg, wu, wd, g_post.reshape(1, d))


def _mm_norm_res_kernel(*refs, n_seg, nk):
    a_refs = refs[:n_seg]
    w_refs = refs[n_seg:2 * n_seg]
    g_ref, r_ref, o_ref, acc_ref = refs[2 * n_seg:]
    kk = pl.program_id(1)

    @pl.when(kk == 0)
    def _():
        acc_ref[...] = jnp.zeros_like(acc_ref)

    part = _dot(a_refs[0][...], w_refs[0][...])
    for s in range(1, n_seg):
        part = part + _dot(a_refs[s][...], w_refs[s][...])
    acc_ref[...] += part

    @pl.when(kk == nk - 1)
    def _():
        o_ref[...] = r_ref[...] + _rms(acc_ref[...], g_ref[...])


def mm_norm_res(a_list, w, layer, g, res, tm, tk):
    n_seg = len(a_list)
    m, ka = a_list[0].shape
    n = w.shape[2]
    assert w.shape[1] == n_seg * ka and m % tm == 0 and ka % tk == 0
    nk = ka // tk
    a_specs = [pl.BlockSpec((tm, tk), lambda i, k: (i, k)) for _ in range(n_seg)]
    w_specs = [pl.BlockSpec((None, tk, n), functools.partial(lambda i, k, s: (layer, k + s * nk, 0), s=s))
               for s in range(n_seg)]
    return pl.pallas_call(
        functools.partial(_mm_norm_res_kernel, n_seg=n_seg, nk=nk),
        grid=(m // tm, nk),
        in_specs=a_specs + w_specs + [
            pl.BlockSpec((1, n), lambda i, k: (0, 0)),
            pl.BlockSpec((tm, n), lambda i, k: (i, 0)),
        ],
        out_specs=pl.BlockSpec((tm, n), lambda i, k: (i, 0)),
        out_shape=jax.ShapeDtypeStruct((m, n), F32),
        scratch_shapes=[pltpu.VMEM((tm, n), F32)],
        compiler_params=_params("parallel", "arbitrary"),
        name="mm_norm_res",
    )(*a_list, *([w] * n_seg), g.reshape(1, n), res)


CAST_BLOCK_BYTES = 4 * 1024 * 1024


def _cast_kernel(x_ref, o_ref):
    o_ref[...] = x_ref[...].astype(o_ref.dtype)


def cast_bf16(w, row0=0, height=None):
    layers, n, k = w.shape
    height = n - row0 if height is None else height
    assert row0 + height <= n
    tr = max(r for r in range(16, height + 1, 16)
             if height % r == 0 and row0 % r == 0 and r * k * 4 <= CAST_BLOCK_BYTES)
    return pl.pallas_call(
        _cast_kernel,
        grid=(layers, height // tr),
        in_specs=[pl.BlockSpec((None, tr, k), lambda l, i: (l, row0 // tr + i, 0))],
        out_specs=pl.BlockSpec((None, tr, k), lambda l, i: (l, i, 0)),
        out_shape=jax.ShapeDtypeStruct((layers, height, k), BF16),
        compiler_params=_params("parallel", "parallel"),
        name="cast_bf16",
    )(w)


IDX_TQ = 128
IDX_TK = 512
IDX_ROWS = 128
IDX_STEPS_PER_CHECK = 4


def _indexer_kernel(qi_ref, aux_ref, kt_ref, mask_ref, key_ref, qh_ref, wb_ref, *, topk, w_scale):
    tq, tk, tr = IDX_TQ, IDX_TK, IDX_ROWS
    nj = tk // 128
    s_len = mask_ref.shape[1]
    qb = pl.program_id(0)
    n_chunks = ((qb + 1) * tq - 1) // tk + 1

    lane = lax.broadcasted_iota(I32, (tq, 128), 1)
    aux = aux_ref[...]
    for h in range(N_IDX_HEADS):
        pair = qi_ref[:, (h // 2) * 128:(h // 2 + 1) * 128]
        keep = (lane < IDX_HEAD_DIM) if h % 2 == 0 else (lane >= IDX_HEAD_DIM)
        qh_ref[h] = jnp.where(keep, pair, jnp.zeros_like(pair))
        wcol = aux[:, IDX_HEAD_DIM + h:IDX_HEAD_DIM + h + 1] * w_scale
        wb_ref[h] = jnp.broadcast_to(wcol, (tq, 128))

    lane_r = lax.broadcasted_iota(I32, (tr, 128), 1)
    row_r = lax.broadcasted_iota(I32, (tr, 128), 0)

    def score_chunk(c, tops):
        col0 = pl.multiple_of(c * tk, tk)
        kt = kt_ref[:, pl.ds(col0, tk)]
        new_tops = []
        for r in range(tq // tr):
            rows = slice(r * tr, (r + 1) * tr)
            accs = [jnp.zeros((tr, 128), F32) for _ in range(nj)]
            for h in range(N_IDX_HEADS):
                d = _dot(qh_ref[h, rows], kt)
                wb = wb_ref[h, rows]
                for j in range(nj):
                    accs[j] = accs[j] + wb * jnp.maximum(d[:, j * 128:(j + 1) * 128], 0.0)
            q_pos = qb * tq + r * tr + row_r
            m1, m2 = tops[r]
            for j in range(nj):
                bits = pltpu.bitcast(accs[j], I32)
                skey = bits ^ ((bits >> 31) & 0x7FFFFFFF)
                skey = jnp.where(col0 + j * 128 + lane_r <= q_pos, skey, INT_MIN)
                key_ref[rows, pl.ds(pl.multiple_of(col0 + j * 128, 128), 128)] = skey
                m2 = jnp.maximum(m2, jnp.minimum(m1, skey))
                m1 = jnp.maximum(m1, skey)
            new_tops.append((m1, m2))
        return tuple(new_tops)

    lowest = jnp.full((tr, 128), INT_MIN, I32)
    tops = lax.fori_loop(0, n_chunks, score_chunk, tuple((lowest, lowest) for _ in range(tq // tr)))
    m1 = jnp.concatenate([t[0] for t in tops], axis=0)
    m2 = jnp.concatenate([t[1] for t in tops], axis=0)
    assert topk <= 2 * 128 and 127 + s_len // 128 < topk
    top = jnp.max(m2, axis=-1, keepdims=True)
    n_top = jnp.sum(jnp.where(m2 == top, 1, 0), axis=-1, keepdims=True)
    below = jnp.max(jnp.where(m2 == top, INT_MIN, m2), axis=-1, keepdims=True)
    hi = _to_u32(jnp.where(n_top >= 2, top, below))
    lo = _to_u32(jnp.min(m2, axis=-1, keepdims=True))
    n_iter = 32 - jnp.max(lax.clz(_to_i32(hi - lo)))

    def count_ge(cand):
        cand_b = jnp.broadcast_to(cand, (tq, 128))

        def body(c, cnt):
            blk = key_ref[:, pl.ds(pl.multiple_of(c * tk, tk), tk)]
            for j in range(nj):
                cnt = cnt + jnp.where(blk[:, j * 128:(j + 1) * 128] >= cand_b, 1, 0)
            return cnt

        cnt = lax.fori_loop(0, n_chunks, body, jnp.zeros((tq, 128), I32))
        return jnp.sum(cnt, axis=-1, keepdims=True)

    def not_settled(state):
        i, _, _, _, settled = state
        return jnp.logical_and(i < n_iter, settled == 0)

    def bisect(state):
        i, lo, hi, n_lo, _ = state
        for _ in range(IDX_STEPS_PER_CHECK):
            width = hi - lo
            mid = lo + (width >> 1) + (width & 1)
            n_mid = count_ge(_to_i32(mid) ^ INT_MIN)
            ok = n_mid >= topk
            lo = jnp.where(ok, mid, lo)
            hi = jnp.where(ok, hi, jnp.where(width > 0, mid - 1, hi))
            n_lo = jnp.where(ok, n_mid, n_lo)
        settled = jnp.min(jnp.where(jnp.logical_or(n_lo == topk, hi == lo), 1, 0))
        return i + IDX_STEPS_PER_CHECK, lo, hi, n_lo, settled

    unknown = jnp.full((tq, 1), s_len + 1, I32)
    _, tau_u, _, _, _ = lax.while_loop(not_settled, bisect, (jnp.int32(0), lo, hi, unknown, jnp.int32(0)))
    tau_u = _to_i32(tau_u)
    tau = jnp.broadcast_to(jnp.maximum(tau_u ^ INT_MIN, INT_MIN + 1), (tq, 128))

    neg = jnp.full((tq, tk), NEG_INF, F32).astype(BF16)

    def write_chunk(c, carry):
        col0 = pl.multiple_of(c * tk, tk)

        @pl.when(c < n_chunks)
        def _():
            blk = key_ref[:, pl.ds(col0, tk)]
            parts = [jnp.where(blk[:, j * 128:(j + 1) * 128] >= tau, 0.0, NEG_INF) for j in range(nj)]
            mask_ref[:, pl.ds(col0, tk)] = jnp.concatenate(parts, axis=-1).astype(BF16)

        @pl.when(c >= n_chunks)
        def _():
            mask_ref[:, pl.ds(col0, tk)] = neg

        return carry

    lax.fori_loop(0, s_len // tk, write_chunk, 0)


def indexer_mask(p_main, p_aux, kt2, topk):
    s_len = p_main.shape[0]
    tq = IDX_TQ
    w_scale = (N_IDX_HEADS ** -0.5) * (IDX_HEAD_DIM ** -0.5)
    return pl.pallas_call(
        functools.partial(_indexer_kernel, topk=topk, w_scale=w_scale),
        grid=(s_len // tq,),
        in_specs=[
            pl.BlockSpec((tq, 1024), lambda i: (i, 3)),
            pl.BlockSpec((tq, 128), lambda i: (i, 8)),
            pl.BlockSpec((128, s_len), lambda i: (0, 0)),
        ],
        out_specs=pl.BlockSpec((tq, s_len), lambda i: (i, 0)),
        out_shape=jax.ShapeDtypeStruct((s_len, s_len), BF16),
        scratch_shapes=[
            pltpu.VMEM((tq, s_len), I32),
            pltpu.VMEM((N_IDX_HEADS, tq, 128), BF16),
            pltpu.VMEM((N_IDX_HEADS, tq, 128), F32),
        ],
        compiler_params=_params("parallel"),
        name="indexer_mask",
    )(p_main, p_aux, kt2)


ATT_T = 512
ATT_RC = 32
ATT_SUB = 128


def _attention_kernel(qb_tab, kb_tab, q_ref, k_ref, v_ref, mask_ref, tz_ref, o_ref,
                      qs_ref, mf_ref, p_ref, m_ref, l_ref, acc_ref, *, n_heads, scale):
    step = pl.program_id(0)
    qb = qb_tab[step]
    kb = kb_tab[step]
    dh, t, rc, sub = ATTN_HEAD_DIM, ATT_T, ATT_RC, ATT_SUB
    nsub = t // sub

    @pl.when(kb == 0)
    def _():
        qs_ref[...] = (q_ref[...].astype(F32) * (scale * LOG2E)).astype(BF16)
        m_ref[...] = jnp.full_like(m_ref, NEG_INF)
        l_ref[...] = jnp.zeros_like(l_ref)
        acc_ref[...] = jnp.zeros_like(acc_ref)

    mf_ref[...] = mask_ref[...].astype(F32)
    prev_flag = jnp.where(kb == qb - 1, 1.0, 0.0).astype(F32)

    def run(diag):
        sls = [slice(h * dh, (h + 1) * dh) for h in range(n_heads)]
        s_next = _dot_nt(qs_ref[:, sls[0]], k_ref[:, sls[0]])
        for h in range(n_heads):
            sl = sls[h]
            s = s_next
            if h + 1 < n_heads:
                s_next = _dot_nt(qs_ref[:, sls[h + 1]], k_ref[:, sls[h + 1]])
            alphas = []
            for r in range(t // rc):
                rows = slice(r * rc, (r + 1) * rc)
                blk, r0 = divmod(r * rc, sub)
                sr = s[rows] + mf_ref[rows, :]
                parts = [sr[:, b * sub:(b + 1) * sub] for b in range(nsub)]
                if diag:
                    parts[blk] = parts[blk] + tz_ref[0, h, r0:r0 + rc, :]
                    if blk >= 1:
                        parts[blk - 1] = parts[blk - 1] + tz_ref[1, h, r0:r0 + rc, :]
                elif blk == 0:
                    parts[nsub - 1] = parts[nsub - 1] + prev_flag * tz_ref[1, h, r0:r0 + rc, :]
                m_prev = m_ref[h, rows]
                m_new = jnp.maximum(m_prev, jnp.max(functools.reduce(jnp.maximum, parts), axis=-1, keepdims=True))
                alpha = jnp.exp2(m_prev - m_new)
                ps = [jnp.exp2(pb - m_new) for pb in parts]
                l_ref[h, rows] = alpha * l_ref[h, rows] + jnp.sum(functools.reduce(jnp.add, ps), axis=-1, keepdims=True)
                m_ref[h, rows] = m_new
                p_ref[h, rows, :] = jnp.concatenate(ps, axis=-1).astype(BF16)
                alphas.append(alpha)
            pv = _dot(p_ref[h], v_ref[:, sl])
            for r in range(t // rc):
                rows = slice(r * rc, (r + 1) * rc)
                acc_ref[h, rows] = alphas[r] * acc_ref[h, rows] + pv[rows]

    @pl.when(kb != qb)
    def _():
        run(False)

    @pl.when(kb == qb)
    def _():
        run(True)
        for h in range(n_heads):
            o_ref[:, h * dh:(h + 1) * dh] = (acc_ref[h] / l_ref[h]).astype(o_ref.dtype)


def masked_attention(p_main, mask, bias_tables, n_heads):
    s_len = p_main.shape[0]
    t = ATT_T
    nb = s_len // t
    pairs = [(q, k) for q in range(nb) for k in range(q + 1)]
    qb_tab = jnp.array([p[0] for p in pairs], I32)
    kb_tab = jnp.array([p[1] for p in pairs], I32)
    width = n_heads * ATTN_HEAD_DIM
    grid_spec = pltpu.PrefetchScalarGridSpec(
        num_scalar_prefetch=2,
        grid=(len(pairs),),
        in_specs=[
            pl.BlockSpec((t, width), lambda s, qt, kt: (qt[s], 0)),
            pl.BlockSpec((t, width), lambda s, qt, kt: (kt[s], 1)),
            pl.BlockSpec((t, width), lambda s, qt, kt: (kt[s], 2)),
            pl.BlockSpec((t, t), lambda s, qt, kt: (qt[s], kt[s])),
            pl.BlockSpec((2, n_heads, ATT_SUB, ATT_SUB), lambda s, qt, kt: (0, 0, 0, 0)),
        ],
        out_specs=pl.BlockSpec((t, width), lambda s, qt, kt: (qt[s], 0)),
        scratch_shapes=[
            pltpu.VMEM((t, width), BF16),
            pltpu.VMEM((t, t), F32),
            pltpu.VMEM((n_heads, t, t), BF16),
            pltpu.VMEM((n_heads, t, 128), F32),
            pltpu.VMEM((n_heads, t, 128), F32),
            pltpu.VMEM((n_heads, t, ATTN_HEAD_DIM), F32),
        ],
    )
    return pl.pallas_call(
        functools.partial(_attention_kernel, n_heads=n_heads, scale=ATTN_HEAD_DIM ** -0.5),
        grid_spec=grid_spec,
        out_shape=jax.ShapeDtypeStruct((s_len, width), BF16),
        compiler_params=_params("arbitrary"),
        name="masked_attention",
    )(qb_tab, kb_tab, p_main, p_main, p_main, mask, bias_tables)


def _t5_bucket(dist):
    max_exact = N_BUCKETS // 2
    d = jnp.maximum(dist, 0)
    log_ratio = jnp.log(jnp.maximum(d, 1).astype(F32) / max_exact) / math.log(MAX_DISTANCE / max_exact)
    large = jnp.minimum(max_exact + (log_ratio * (N_BUCKETS - max_exact)).astype(I32), N_BUCKETS - 1)
    return jnp.where(d < max_exact, d, large)


def _bias_tables(rel_bias):
    assert MAX_DISTANCE <= ATT_SUB
    i = jnp.arange(ATT_SUB, dtype=I32)[:, None]
    j = jnp.arange(ATT_SUB, dtype=I32)[None, :]
    bucket = _t5_bucket(jnp.stack([i - j, i - j + ATT_SUB]))[:, None]
    rel = rel_bias.astype(F32)
    far = rel[N_BUCKETS - 1]
    tables = jnp.zeros((2, rel_bias.shape[1], ATT_SUB, ATT_SUB), F32)
    for b in range(N_BUCKETS):
        tables = jnp.where(bucket == b, (rel[b] - far)[None, :, None, None], tables)
    return tables * LOG2E


POOL_TR = 512
POOL_G = 256
HALO = 16


def _pool_kernel(x_ref, prev_ref, w_ref, sc_ref, o_ref):
    i = pl.program_id(0)
    g = pl.program_id(1)
    tr = POOL_TR
    cur = x_ref[...]
    prev = jnp.where(i > 0, prev_ref[...], 0.0)
    ext = jnp.concatenate([prev, cur], axis=0)
    sums = []
    run = ext
    for shift in (1, 2, 4, 8):
        run = run + pltpu.roll(run, shift, 0)
        sums.append(run[HALO:])
    win_sum = jnp.where(g == 0, sums[0], jnp.where(g == 1, sums[1], jnp.where(g == 2, sums[2], sums[3])))
    win = jnp.left_shift(jnp.int32(2), g)
    t_pos = i * tr + lax.broadcasted_iota(I32, (tr, 1), 0)
    count = jnp.minimum(t_pos + 1, win).astype(F32)
    pooled = win_sum / count - cur
    o_ref[...] = (_dot(pooled.astype(BF16), w_ref[0]) * sc_ref[...]).astype(o_ref.dtype)


def multiscale_pool(p_aux, pool_w, pool_scale):
    s_len = p_aux.shape[0]
    tr, gw = POOL_TR, POOL_G
    ng = len(POOL_WINDOWS)
    return pl.pallas_call(
        _pool_kernel,
        grid=(s_len // tr, ng),
        in_specs=[
            pl.BlockSpec((tr, gw), lambda i, g: (i, g)),
            pl.BlockSpec((HALO, gw), lambda i, g: (jnp.maximum(i * (tr // HALO) - 1, 0), g)),
            pl.BlockSpec((1, gw, gw), lambda i, g: (g, 0, 0)),
            pl.BlockSpec((1, gw), lambda i, g: (0, g)),
        ],
        out_specs=pl.BlockSpec((tr, gw), lambda i, g: (i, g)),
        out_shape=jax.ShapeDtypeStruct((s_len, ng * gw), BF16),
        compiler_params=_params("parallel", "parallel"),
        name="multiscale_pool",
    )(p_aux, p_aux, pool_w, pool_scale.reshape(1, ng * gw))


def _split3(x):
    hi = x.astype(BF16)
    r1 = x - hi.astype(F32)
    mid = r1.astype(BF16)
    lo = (r1 - mid.astype(F32)).astype(BF16)
    return hi, mid, lo


def _dot_exact_lhs(a_bf16, x):
    hi, mid, lo = _split3(x)
    return _dot(a_bf16, lo) + _dot(a_bf16, mid) + _dot(a_bf16, hi)


def _dot_exact_rhs(x, b_bf16):
    hi, mid, lo = _split3(x)
    return _dot(lo, b_bf16) + _dot(mid, b_bf16) + _dot(hi, b_bf16)


def _softplus(x):
    return jnp.maximum(x, 0.0) + jnp.log(1.0 + jnp.exp(-jnp.abs(x)))


PREP_CHUNKS = 2


def _gdn_prep_kernel(x_ref, prev_ref, cw_ref, ba_ref, alog_ref, dtb_ref,
                     u_ref, w_ref, qd_ref, kdt_ref, intra_ref, dl_ref, ext_ref, *, n_vheads):
    c = CHUNK
    dk = GDN_DK
    assert 2 * c == 128 and dk == 128 and GDN_DV == 128
    scale = dk ** -0.5
    n_kheads = n_vheads // 2
    step = pl.program_id(0)
    for sb in range(2 * n_kheads + n_vheads):
        cols = slice(sb * 128, (sb + 1) * 128)
        ext_ref[sb, :HALO, :] = jnp.where(step > 0, prev_ref[:, cols].astype(F32), 0.0)
        ext_ref[sb, HALO:, :] = x_ref[:, cols].astype(F32)
    cw = cw_ref[...]

    def conv_head(sb, ci, out_scale):
        cols = slice(sb * 128, (sb + 1) * 128)
        base = HALO + ci * c
        y = ext_ref[sb, base:base + c, :] * cw[CONV_K - 1:CONV_K, cols]
        for tap in range(CONV_K - 1):
            lo = base - (CONV_K - 1 - tap)
            y = y + ext_ref[sb, lo:lo + c, :] * cw[tap:tap + 1, cols]
        y = y / (1.0 + jnp.exp2(y * (-LOG2E)))
        if out_scale is not None:
            y = y * (lax.rsqrt(jnp.sum(y * y, axis=-1, keepdims=True) + RMS_EPS) * out_scale)
        return y

    row = lax.broadcasted_iota(I32, (c, c), 0)
    col = lax.broadcasted_iota(I32, (c, c), 1)
    lower = jnp.where(row >= col, 1.0, 0.0).astype(BF16)
    upper = jnp.where(row <= col, 1.0, 0.0).astype(BF16)
    row_w = lax.broadcasted_iota(I32, (c, 2 * c), 0)
    lane_w = lax.broadcasted_iota(I32, (c, 2 * c), 1)
    first = lane_w < c
    col_w = jnp.where(first, lane_w, lane_w - c)
    tril_w = row_w >= col_w
    strict_w = row_w > col_w
    eye_w = jnp.where(row_w == col_w, 1.0, 0.0).astype(F32)

    def block_diag(wide16):
        zero = jnp.zeros_like(wide16)
        return jnp.concatenate([jnp.where(first, wide16, zero), jnp.where(first, zero, wide16)], axis=0)

    pairs = range(n_vheads // 2)
    pws, t_invs, rhss, where = [], [], [], []
    for ci in range(PREP_CHUNKS):
        rows = slice(ci * c, (ci + 1) * c)
        ba = ba_ref[rows, :]
        beta_all = jax.nn.sigmoid(ba)
        g_all = -jnp.exp(alog_ref[...]) * _softplus(ba + dtb_ref[...])
        gc_all = _dot_exact_lhs(lower, g_all)
        gct_all = _dot_exact_rhs(g_all.T, upper)
        for p in pairs:
            kh = conv_head(n_kheads + p, ci, 1.0)
            qh = conv_head(p, ci, scale)
            kh_t = kh.T
            kh16 = kh.astype(BF16)
            kk16 = jnp.concatenate([kh16, kh16], axis=0)
            both = _dot_nt(jnp.concatenate([kh16, qh.astype(BF16)], axis=0), kk16)
            gram, qk = both[:c], both[c:]
            hv = (2 * p, 2 * p + 1)
            beta = [jnp.broadcast_to(beta_all[:, h:h + 1], (c, 2 * c)) for h in hv]
            gcol = [jnp.broadcast_to(gc_all[:, n_vheads + h:n_vheads + h + 1], (c, 2 * c)) for h in hv]
            grow = [gct_all[n_vheads + h:n_vheads + h + 1, :] for h in hv]
            grow_w = jnp.concatenate(grow, axis=1)
            gcol_w = jnp.where(first, gcol[0], gcol[1])
            decay_w = jnp.where(tril_w, jnp.exp(jnp.where(tril_w, gcol_w - grow_w, 0.0)), 0.0)
            a_w = jnp.where(strict_w, -(jnp.where(first, beta[0], beta[1]) * gram * decay_w), 0.0)
            intra_ref[ci, p] = jnp.where(tril_w, qk * decay_w, 0.0).astype(BF16)
            rhs = []
            for i, h in enumerate(hv):
                eg = jnp.exp(gcol[i])
                g_last = grow[i][:, c - 1:c]
                sl = slice(h * GDN_DV, (h + 1) * GDN_DV)
                qd_ref[rows, sl] = (qh * eg).astype(BF16)
                kdt_ref[ci, h] = (kh_t * jnp.exp(g_last - grow[i])).astype(BF16)
                dl_ref[ci, h] = jnp.broadcast_to(jnp.exp(g_last), (1, 128))
                vh = conv_head(2 * n_kheads + h, ci, None)
                rhs.append(jnp.concatenate([vh * beta[i], kh * (beta[i] * eg)], axis=-1).astype(BF16))
            pws.append(a_w)
            t_invs.append(eye_w + a_w)
            rhss.append(jnp.concatenate(rhs, axis=0))
            where.append((rows, p))
    n_levels = int(math.log2(c)) - 1
    pws = [_dot(a.astype(BF16), block_diag(a.astype(BF16))) for a in pws]
    for level in range(n_levels):
        new_t, new_pw = [], []
        for t, pw in zip(t_invs, pws):
            pw16 = pw.astype(BF16)
            if level == n_levels - 1:
                new_t.append(t + _dot(t.astype(BF16), block_diag(pw16)))
            else:
                both = _dot(jnp.concatenate([t.astype(BF16), pw16], axis=0), block_diag(pw16))
                new_t.append(t + both[:c])
                new_pw.append(both[c:])
        t_invs, pws = new_t, new_pw
    for (rows, p), t_inv, rhs in zip(where, t_invs, rhss):
        t16 = t_inv.astype(BF16)
        zero = jnp.zeros_like(t16)
        lhs = jnp.concatenate([jnp.where(first, t16, zero), jnp.where(first, zero, t16)], axis=0)
        sol = _dot(lhs, rhs)
        for i, h in enumerate((2 * p, 2 * p + 1)):
            sl = slice(h * GDN_DV, (h + 1) * GDN_DV)
            u_ref[rows, sl] = sol[i * c:(i + 1) * c, :GDN_DV]
            w_ref[rows, sl] = sol[i * c:(i + 1) * c, GDN_DV:].astype(BF16)


def gdn_prepare(p_main, conv_w, ba, a_log, dt_bias, n_kheads, n_vheads):
    s_len = p_main.shape[0]
    nc = PREP_CHUNKS
    c = CHUNK * nc
    n = s_len // c
    v_w = n_vheads * GDN_DV
    conv_width = 2 * n_kheads * GDN_DK + v_w
    assert 2 * n_vheads <= 128 and n_vheads == 2 * n_kheads and c % HALO == 0
    pad = jnp.zeros((128 - 2 * n_vheads,), F32)
    alog_row = jnp.concatenate([jnp.zeros((n_vheads,), F32), a_log, pad]).reshape(1, 128)
    dtb_row = jnp.concatenate([jnp.zeros((n_vheads,), F32), dt_bias, pad]).reshape(1, 128)
    return pl.pallas_call(
        functools.partial(_gdn_prep_kernel, n_vheads=n_vheads),
        grid=(n,),
        in_specs=[
            pl.BlockSpec((c, conv_width), lambda i: (i, 0)),
            pl.BlockSpec((HALO, conv_width), lambda i: (jnp.maximum(i * (c // HALO) - 1, 0), 0)),
            pl.BlockSpec((CONV_K, conv_width), lambda i: (0, 0)),
            pl.BlockSpec((c, 128), lambda i: (i, 0)),
            pl.BlockSpec((1, 128), lambda i: (0, 0)),
            pl.BlockSpec((1, 128), lambda i: (0, 0)),
        ],
        out_specs=[
            pl.BlockSpec((c, v_w), lambda i: (i, 0)),
            pl.BlockSpec((c, v_w), lambda i: (i, 0)),
            pl.BlockSpec((c, v_w), lambda i: (i, 0)),
            pl.BlockSpec((nc, n_vheads, GDN_DK, CHUNK), lambda i: (i, 0, 0, 0)),
            pl.BlockSpec((nc, n_vheads // 2, CHUNK, 2 * CHUNK), lambda i: (i, 0, 0, 0)),
            pl.BlockSpec((nc, n_vheads, 1, 128), lambda i: (i, 0, 0, 0)),
        ],
        out_shape=[
            jax.ShapeDtypeStruct((s_len, v_w), F32),
            jax.ShapeDtypeStruct((s_len, v_w), BF16),
            jax.ShapeDtypeStruct((s_len, v_w), BF16),
            jax.ShapeDtypeStruct((n * nc, n_vheads, GDN_DK, CHUNK), BF16),
            jax.ShapeDtypeStruct((n * nc, n_vheads // 2, CHUNK, 2 * CHUNK), BF16),
            jax.ShapeDtypeStruct((n * nc, n_vheads, 1, 128), F32),
        ],
        scratch_shapes=[pltpu.VMEM((conv_width // 128, HALO + c, 128), F32)],
        compiler_params=_params("parallel"),
        name="gdn_prepare",
    )(p_main, p_main, conv_w, ba, alog_row, dtb_row)


SCAN_HG = 16
SCAN_CHUNKS = 4


def _gdn_scan_kernel(u_ref, w_ref, qd_ref, kdt_ref, intra_ref, dl_ref, z_ref, nw_ref, o_ref, state_ref):
    @pl.when(pl.program_id(1) == 0)
    def _():
        state_ref[...] = jnp.zeros_like(state_ref)

    nw = nw_ref[...]
    heads = range(SCAN_HG)
    sls = [slice(j * GDN_DV, (j + 1) * GDN_DV) for j in heads]
    first = lax.broadcasted_iota(I32, (CHUNK, 2 * CHUNK), 1) < CHUNK
    states = [state_ref[j] for j in heads]
    for ci in range(SCAN_CHUNKS):
        rows = slice(ci * CHUNK, (ci + 1) * CHUNK)
        s16 = [s.astype(BF16) for s in states]
        ws = [_dot(w_ref[rows, sls[j]], s16[j]) for j in heads]
        qs = [_dot(qd_ref[rows, sls[j]], s16[j]) for j in heads]
        v16 = [(u_ref[rows, sls[j]] - ws[j]).astype(BF16) for j in heads]
        upd = [_dot(kdt_ref[ci, j], v16[j]) for j in heads]
        outs = []
        for p in range(SCAN_HG // 2):
            pair = intra_ref[ci, p]
            zero = jnp.zeros_like(pair)
            v_pair = jnp.concatenate([v16[2 * p], v16[2 * p + 1]], axis=0)
            outs.append(qs[2 * p] + _dot(jnp.where(first, pair, zero), v_pair))
            outs.append(qs[2 * p + 1] + _dot(jnp.where(first, zero, pair), v_pair))
        states = [states[j] * dl_ref[ci, j] + upd[j] for j in heads]
        for j in heads:
            z = z_ref[rows, sls[j]].astype(F32)
            o_ref[rows, sls[j]] = (_rms(outs[j], nw) * (z * jax.nn.sigmoid(z))).astype(o_ref.dtype)
    for j in heads:
        state_ref[j] = states[j]


def gdn_scan(u, w, qd, kdt, intra, dl, p_main, norm_w, z_col_block):
    s_len, v_w = u.shape
    nc = SCAN_CHUNKS
    c = CHUNK * nc
    n = s_len // c
    hg = SCAN_HG
    gw = hg * GDN_DV
    return pl.pallas_call(
        _gdn_scan_kernel,
        grid=(v_w // gw, n),
        in_specs=[
            pl.BlockSpec((c, gw), lambda g, i: (i, g)),
            pl.BlockSpec((c, gw), lambda g, i: (i, g)),
            pl.BlockSpec((c, gw), lambda g, i: (i, g)),
            pl.BlockSpec((nc, hg, GDN_DK, CHUNK), lambda g, i: (i, g, 0, 0)),
            pl.BlockSpec((nc, hg // 2, CHUNK, 2 * CHUNK), lambda g, i: (i, g, 0, 0)),
            pl.BlockSpec((nc, hg, 1, 128), lambda g, i: (i, g, 0, 0)),
            pl.BlockSpec((c, gw), lambda g, i: (i, z_col_block + g)),
            pl.BlockSpec((1, GDN_DV), lambda g, i: (0, 0)),
        ],
        out_specs=pl.BlockSpec((c, gw), lambda g, i: (i, g)),
        out_shape=jax.ShapeDtypeStruct((s_len, v_w), BF16),
        scratch_shapes=[pltpu.VMEM((hg, GDN_DK, GDN_DV), F32)],
        compiler_params=_params("parallel", "arbitrary"),
        name="gdn_scan",
    )(u, w, qd, kdt, intra, dl, p_main, norm_w.reshape(1, GDN_DV))


PROJ_TM = 1024
PROJ_TN = 1024
FFN_TM = 512
FFN_TF = 512
OUT_TM = 512
OUT_TK_MAX = 2048


def _out_tk(k):
    return max(c for c in range(128, OUT_TK_MAX + 1, 128) if k % c == 0)


def _even_main_width(d):
    return 3 * (d // 2) + N_IDX_HEADS * IDX_HEAD_DIM


def _odd_main_width(d):
    n_kheads = d // 128
    return 2 * n_kheads * GDN_DK + 2 * (2 * n_kheads) * GDN_DV


def _even_mixer(x, g_pre, g_post, w_in16t, w_rest, w_o16, layer, pool_w, pool_scale, bias_tables):
    s_len, d = x.shape
    attn_w = d // 2
    n_heads = attn_w // ATTN_HEAD_DIM
    idx_w = N_IDX_HEADS * IDX_HEAD_DIM
    pool_width = len(POOL_WINDOWS) * POOL_G
    main_w = _even_main_width(d)
    assert main_w == 3 * attn_w + idx_w == 4096 and pool_width == 1024
    small = IDX_HEAD_DIM + N_IDX_HEADS
    assert w_rest.shape[0] == small + pool_width
    w_aux = jnp.concatenate([w_rest[small:], w_rest[:small], jnp.zeros((128 - small, d), BF16)], axis=0)
    p_main = norm_matmul(x, g_pre, w_in16t, layer, main_w, BF16, PROJ_TM, PROJ_TN)
    p_aux = norm_matmul(x, g_pre, w_aux[None], 0, w_aux.shape[0], F32, PROJ_TM, w_aux.shape[0])
    k_idx_t = p_aux[:, pool_width:pool_width + IDX_HEAD_DIM].astype(BF16).T
    kt2 = jnp.concatenate([k_idx_t, k_idx_t], axis=0)
    mask = indexer_mask(p_main, p_aux, kt2, min(TOPK_MAX, s_len // 4))
    attn = masked_attention(p_main, mask, bias_tables, n_heads)
    pooled = multiscale_pool(p_aux, pool_w.astype(BF16), pool_scale)
    return mm_norm_res([attn, pooled], w_o16, layer, g_post, x, OUT_TM, _out_tk(attn.shape[1]))


def _odd_mixer(x, g_pre, g_post, w_in16t, w_rest, w_o16, layer, conv_w, a_log, dt_bias, norm_w):
    s_len, d = x.shape
    n_kheads = d // 128
    n_vheads = 2 * n_kheads
    qk_w = n_kheads * GDN_DK
    v_w = n_vheads * GDN_DV
    conv_width = 2 * qk_w + v_w
    main_w = _odd_main_width(d)
    assert main_w == conv_width + v_w
    w_ba = jnp.concatenate([w_rest, jnp.zeros((128 - 2 * n_vheads, d), BF16)], axis=0)
    p_main = norm_matmul(x, g_pre, w_in16t, layer, main_w, BF16, PROJ_TM, PROJ_TN)
    ba = norm_matmul(x, g_pre, w_ba[None], 0, 128, F32, PROJ_TM, 128)
    u, w, qd, kdt, intra, dl = gdn_prepare(p_main, conv_w, ba, a_log, dt_bias, n_kheads, n_vheads)
    o = gdn_scan(u, w, qd, kdt, intra, dl, p_main, norm_w, conv_width // (SCAN_HG * GDN_DV))
    return mm_norm_res([o], w_o16, layer, g_post, x, OUT_TM, _out_tk(v_w))


def _ffn(x, g_pre, g_post, w_gate16, w_up16, w_down16, layer):
    return ffn(x, g_pre, g_post, w_gate16, w_up16, w_down16, layer, FFN_TM, FFN_TF)


def kernel(x, rel_bias, even_w_in, even_w_o, pool_w, pool_scale, odd_w_in, conv_w, a_log, dt_bias,
           gdn_norm_w, odd_w_o, ffn_w_gate, ffn_w_up, ffn_w_down, mix_pre_g, mix_post_g, ffn_pre_g, ffn_post_g):
    b, s_len, d = x.shape
    depth = ffn_w_gate.shape[0]
    bias_tables = _bias_tables(rel_bias)
    em, om = _even_main_width(d), _odd_main_width(d)
    even_t, odd_t = jnp.swapaxes(even_w_in, 1, 2), jnp.swapaxes(odd_w_in, 1, 2)
    even_in16, odd_in16 = cast_bf16(even_t, 0, em), cast_bf16(odd_t, 0, om)
    even_rest16, odd_rest16 = cast_bf16(even_t, em), cast_bf16(odd_t, om)
    even_o16, odd_o16 = cast_bf16(even_w_o), cast_bf16(odd_w_o)
    gate16, up16, down16 = cast_bf16(ffn_w_gate), cast_bf16(ffn_w_up), cast_bf16(ffn_w_down)
    xs = x.reshape(b * s_len, d)
    outs = []
    for bi in range(b):
        h = xs[bi * s_len:(bi + 1) * s_len]
        for layer in range(depth):
            i = layer // 2
            if layer % 2 == 0:
                h = _even_mixer(h, mix_pre_g[layer], mix_post_g[layer], even_in16, even_rest16[i], even_o16, i,
                                pool_w[i], pool_scale[i], bias_tables)
            else:
                h = _odd_mixer(h, mix_pre_g[layer], mix_post_g[layer], odd_in16, odd_rest16[i], odd_o16, i,
                               conv_w[i], a_log[i], dt_bias[i], gdn_norm_w[i])
            h = _ffn(h, ffn_pre_g[layer], ffn_post_g[layer], gate16, up16, down16, layer)
        outs.append(h)
    return jnp.concatenate(outs, axis=0).reshape(b, s_len, d)
```

```python
import functools
import math

import jax
import jax.numpy as jnp
from jax import lax
from jax.experimental import pallas as pl
from jax.experimental.pallas import tpu as pltpu

F32 = jnp.float32
BF16 = jnp.bfloat16
I32 = jnp.int32

RMS_EPS = 1e-6
NEG_INF = -1e30
INT_MIN = -2 ** 31
LOG2E = 1.4426950408889634

VMEM_LIMIT_BYTES = 56 * 1024 * 1024

ATTN_HEAD_DIM = 128
N_IDX_HEADS = 16
IDX_HEAD_DIM = 64
TOPK_MAX = 256
N_BUCKETS = 32
MAX_DISTANCE = 128
POOL_WINDOWS = (2, 4, 8, 16)
GDN_DK = 128
GDN_DV = 128
CONV_K = 4
CHUNK = 64


def _params(*sem):
    return pltpu.CompilerParams(dimension_semantics=sem, vmem_limit_bytes=VMEM_LIMIT_BYTES)


def _dot(a, b):
    return jnp.dot(a, b, preferred_element_type=F32)


def _dot_nt(a, b):
    return lax.dot_general(a, b, (((1,), (1,)), ((), ())), preferred_element_type=F32)


def _to_u32(x):
    return lax.bitcast_convert_type(x ^ INT_MIN, jnp.uint32)


def _to_i32(x):
    return lax.bitcast_convert_type(x, I32)


def _rms(x, g):
    ms = jnp.mean(x * x, axis=-1, keepdims=True)
    return x * lax.rsqrt(ms + RMS_EPS) * g


def _norm_matmul_kernel(x_ref, g_ref, wt_ref, o_ref, hn_ref):
    @pl.when(pl.program_id(1) == 0)
    def _():
        hn_ref[...] = _rms(x_ref[...], g_ref[...]).astype(BF16)

    o_ref[...] = _dot_nt(hn_ref[...], wt_ref[...]).astype(o_ref.dtype)


def norm_matmul(x, g, wt, layer, n, out_dtype, tm, tn):
    m, k = x.shape
    assert m % tm == 0 and n % tn == 0 and n <= wt.shape[1] and wt.shape[2] == k
    return pl.pallas_call(
        _norm_matmul_kernel,
        grid=(m // tm, n // tn),
        in_specs=[
            pl.BlockSpec((tm, k), lambda i, j: (i, 0)),
            pl.BlockSpec((1, k), lambda i, j: (0, 0)),
            pl.BlockSpec((None, tn, k), lambda i, j: (layer, j, 0)),
        ],
        out_specs=pl.BlockSpec((tm, tn), lambda i, j: (i, j)),
        out_shape=jax.ShapeDtypeStruct((m, n), out_dtype),
        scratch_shapes=[pltpu.VMEM((tm, k), BF16)],
        compiler_params=_params("parallel", "arbitrary"),
        name="norm_matmul",
    )(x, g.reshape(1, k), wt)


def _ffn_kernel(x_ref, g_ref, wg_ref, wu_ref, wd_ref, gp_ref, o_ref, hn_ref, acc_ref, *, nj):
    j = pl.program_id(1)

    @pl.when(j == 0)
    def _():
        hn_ref[...] = _rms(x_ref[...], g_ref[...]).astype(BF16)
        acc_ref[...] = jnp.zeros_like(acc_ref)

    hn = hn_ref[...]
    a = _dot(hn, wg_ref[...])
    b = _dot(hn, wu_ref[...])
    act = (a * jax.nn.sigmoid(a) * b).astype(BF16)
    acc_ref[...] += _dot(act, wd_ref[...])

    @pl.when(j == nj - 1)
    def _():
        o_ref[...] = x_ref[...] + _rms(acc_ref[...], gp_ref[...])


def ffn(x, g_pre, g_post, wg, wu, wd, layer, tm, tf):
    m, d = x.shape
    ff = wg.shape[2]
    assert m % tm == 0 and ff % tf == 0
    nj = ff // tf
    return pl.pallas_call(
        functools.partial(_ffn_kernel, nj=nj),
        grid=(m // tm, nj),
        in_specs=[
            pl.BlockSpec((tm, d), lambda i, j: (i, 0)),
            pl.BlockSpec((1, d), lambda i, j: (0, 0)),
            pl.BlockSpec((None, d, tf), lambda i, j: (layer, 0, j)),
            pl.BlockSpec((None, d, tf), lambda i, j: (layer, 0, j)),
            pl.BlockSpec((None, tf, d), lambda i, j: (layer, j, 0)),
            pl.BlockSpec((1, d), lambda i, j: (0, 0)),
        ],
        out_specs=pl.BlockSpec((tm, d), lambda i, j: (i, 0)),
        out_shape=jax.ShapeDtypeStruct((m, d), F32),
        scratch_shapes=[pltpu.VMEM((tm, d), BF16), pltpu.VMEM((tm, d), F32)],
        compiler_params=_params("parallel", "arbitrary"),
        name="ffn",
    )(x, g_pre.reshape(1, d), wg, wu, wd, g_post.reshape(1, d))


def _mm_norm_res_kernel(*refs, n_seg, nk):
    a_refs = refs[:n_seg]
    w_refs = refs[n_seg:2 * n_seg]
    g_ref, r_ref, o_ref, acc_ref = refs[2 * n_seg:]
    kk = pl.program_id(1)

    @pl.when(kk == 0)
    def _():
        acc_ref[...] = jnp.zeros_like(acc_ref)

    part = _dot(a_refs[0][...], w_refs[0][...])
    for s in range(1, n_seg):
        part = part + _dot(a_refs[s][...], w_refs[s][...])
    acc_ref[...] += part

    @pl.when(kk == nk - 1)
    def _():
        o_ref[...] = r_ref[...] + _rms(acc_ref[...], g_ref[...])


def mm_norm_res(a_list, w, layer, g, res, tm, tk):
    n_seg = len(a_list)
    m, ka = a_list[0].shape
    n = w.shape[2]
    assert w.shape[1] == n_seg * ka and m % tm == 0 and ka % tk == 0
    nk = ka // tk
    a_specs = [pl.BlockSpec((tm, tk), lambda i, k: (i, k)) for _ in range(n_seg)]
    w_specs = [pl.BlockSpec((None, tk, n), functools.partial(lambda i, k, s: (layer, k + s * nk, 0), s=s))
               for s in range(n_seg)]
    return pl.pallas_call(
        functools.partial(_mm_norm_res_kernel, n_seg=n_seg, nk=nk),
        grid=(m // tm, nk),
        in_specs=a_specs + w_specs + [
            pl.BlockSpec((1, n), lambda i, k: (0, 0)),
            pl.BlockSpec((tm, n), lambda i, k: (i, 0)),
        ],
        out_specs=pl.BlockSpec((tm, n), lambda i, k: (i, 0)),
        out_shape=jax.ShapeDtypeStruct((m, n), F32),
        scratch_shapes=[pltpu.VMEM((tm, n), F32)],
        compiler_params=_params("parallel", "arbitrary"),
        name="mm_norm_res",
    )(*a_list, *([w] * n_seg), g.reshape(1, n), res)


CAST_BLOCK_BYTES = 4 * 1024 * 1024


def _cast_kernel(x_ref, o_ref):
    o_ref[...] = x_ref[...].astype(o_ref.dtype)


def cast_bf16(w, row0=0, height=None):
    layers, n, k = w.shape
    height = n - row0 if height is None else height
    assert row0 + height <= n
    tr = max(r for r in range(16, height + 1, 16)
             if height % r == 0 and row0 % r == 0 and r * k * 4 <= CAST_BLOCK_BYTES)
    return pl.pallas_call(
        _cast_kernel,
        grid=(layers, height // tr),
        in_specs=[pl.BlockSpec((None, tr, k), lambda l, i: (l, row0 // tr + i, 0))],
        out_specs=pl.BlockSpec((None, tr, k), lambda l, i: (l, i, 0)),
        out_shape=jax.ShapeDtypeStruct((layers, height, k), BF16),
        compiler_params=_params("parallel", "parallel"),
        name="cast_bf16",
    )(w)


IDX_TQ = 128
IDX_TK = 512
IDX_ROWS = 128
IDX_STEPS_PER_CHECK = 4


def _indexer_kernel(qi_ref, aux_ref, kt_ref, mask_ref, key_ref, qh_ref, wb_ref, *, topk, w_scale):
    tq, tk, tr = IDX_TQ, IDX_TK, IDX_ROWS
    nj = tk // 128
    s_len = mask_ref.shape[1]
    qb = pl.program_id(0)
    n_chunks = ((qb + 1) * tq - 1) // tk + 1

    lane = lax.broadcasted_iota(I32, (tq, 128), 1)
    aux = aux_ref[...]
    for h in range(N_IDX_HEADS):
        pair = qi_ref[:, (h // 2) * 128:(h // 2 + 1) * 128]
        keep = (lane < IDX_HEAD_DIM) if h % 2 == 0 else (lane >= IDX_HEAD_DIM)
        qh_ref[h] = jnp.where(keep, pair, jnp.zeros_like(pair))
        wcol = aux[:, IDX_HEAD_DIM + h:IDX_HEAD_DIM + h + 1] * w_scale
        wb_ref[h] = jnp.broadcast_to(wcol, (tq, 128))

    lane_r = lax.broadcasted_iota(I32, (tr, 128), 1)
    row_r = lax.broadcasted_iota(I32, (tr, 128), 0)

    def score_chunk(c, tops):
        col0 = pl.multiple_of(c * tk, tk)
        kt = kt_ref[:, pl.ds(col0, tk)]
        new_tops = []
        for r in range(tq // tr):
            rows = slice(r * tr, (r + 1) * tr)
            accs = [jnp.zeros((tr, 128), F32) for _ in range(nj)]
            for h in range(N_IDX_HEADS):
                d = _dot(qh_ref[h, rows], kt)
                wb = wb_ref[h, rows]
                for j in range(nj):
                    accs[j] = accs[j] + wb * jnp.maximum(d[:, j * 128:(j + 1) * 128], 0.0)
            q_pos = qb * tq + r * tr + row_r
            m1, m2 = tops[r]
            for j in range(nj):
                bits = pltpu.bitcast(accs[j], I32)
                skey = bits ^ ((bits >> 31) & 0x7FFFFFFF)
                skey = jnp.where(col0 + j * 128 + lane_r <= q_pos, skey, INT_MIN)
                key_ref[rows, pl.ds(pl.multiple_of(col0 + j * 128, 128), 128)] = skey
                m2 = jnp.maximum(m2, jnp.minimum(m1, skey))
                m1 = jnp.maximum(m1, skey)
            new_tops.append((m1, m2))
        return tuple(new_tops)

    lowest = jnp.full((tr, 128), INT_MIN, I32)
    tops = lax.fori_loop(0, n_chunks, score_chunk, tuple((lowest, lowest) for _ in range(tq // tr)))
    m1 = jnp.concatenate([t[0] for t in tops], axis=0)
    m2 = jnp.concatenate([t[1] for t in tops], axis=0)
    assert topk <= 2 * 128 and 127 + s_len // 128 < topk
    top = jnp.max(m2, axis=-1, keepdims=True)
    n_top = jnp.sum(jnp.where(m2 == top, 1, 0), axis=-1, keepdims=True)
    below = jnp.max(jnp.where(m2 == top, INT_MIN, m2), axis=-1, keepdims=True)
    hi = _to_u32(jnp.where(n_top >= 2, top, below))
    lo = _to_u32(jnp.min(m2, axis=-1, keepdims=True))
    n_iter = 32 - jnp.max(lax.clz(_to_i32(hi - lo)))

    def count_ge(cand):
        cand_b = jnp.broadcast_to(cand, (tq, 128))

        def body(c, cnt):
            blk = key_ref[:, pl.ds(pl.multiple_of(c * tk, tk), tk)]
            for j in range(nj):
                cnt = cnt + jnp.where(blk[:, j * 128:(j + 1) * 128] >= cand_b, 1, 0)
            return cnt

        cnt = lax.fori_loop(0, n_chunks, body, jnp.zeros((tq, 128), I32))
        return jnp.sum(cnt, axis=-1, keepdims=True)

    def not_settled(state):
        i, _, _, _, settled = state
        return jnp.logical_and(i < n_iter, settled == 0)

    def bisect(state):
        i, lo, hi, n_lo, _ = state
        for _ in range(IDX_STEPS_PER_CHECK):
            width = hi - lo
            mid = lo + (width >> 1) + (width & 1)
            n_mid = count_ge(_to_i32(mid) ^ INT_MIN)
            ok = n_mid >= topk
            lo = jnp.where(ok, mid, lo)
            hi = jnp.where(ok, hi, jnp.where(width > 0, mid - 1, hi))
            n_lo = jnp.where(ok, n_mid, n_lo)
        settled = jnp.min(jnp.where(jnp.logical_or(n_lo == topk, hi == lo), 1, 0))
        return i + IDX_STEPS_PER_CHECK, lo, hi, n_lo, settled

    unknown = jnp.full((tq, 1), s_len + 1, I32)
    _, tau_u, _, _, _ = lax.while_loop(not_settled, bisect, (jnp.int32(0), lo, hi, unknown, jnp.int32(0)))
    tau_u = _to_i32(tau_u)
    tau = jnp.broadcast_to(jnp.maximum(tau_u ^ INT_MIN, INT_MIN + 1), (tq, 128))

    neg = jnp.full((tq, tk), NEG_INF, F32).astype(BF16)

    def write_chunk(c, carry):
        col0 = pl.multiple_of(c * tk, tk)

        @pl.when(c < n_chunks)
        def _():
            blk = key_ref[:, pl.ds(col0, tk)]
            parts = [jnp.where(blk[:, j * 128:(j + 1) * 128] >= tau, 0.0, NEG_INF) for j in range(nj)]
            mask_ref[:, pl.ds(col0, tk)] = jnp.concatenate(parts, axis=-1).astype(BF16)

        @pl.when(c >= n_chunks)
        def _():
            mask_ref[:, pl.ds(col0, tk)] = neg

        return carry

    lax.fori_loop(0, s_len // tk, write_chunk, 0)


def indexer_mask(p_main, p_aux, kt2, topk):
    s_len = p_main.shape[0]
    tq = IDX_TQ
    w_scale = (N_IDX_HEADS ** -0.5) * (IDX_HEAD_DIM ** -0.5)
    return pl.pallas_call(
        functools.partial(_indexer_kernel, topk=topk, w_scale=w_scale),
        grid=(s_len // tq,),
        in_specs=[
            pl.BlockSpec((tq, 1024), lambda i: (i, 3)),
            pl.BlockSpec((tq, 128), lambda i: (i, 8)),
            pl.BlockSpec((128, s_len), lambda i: (0, 0)),
        ],
        out_specs=pl.BlockSpec((tq, s_len), lambda i: (i, 0)),
        out_shape=jax.ShapeDtypeStruct((s_len, s_len), BF16),
        scratch_shapes=[
            pltpu.VMEM((tq, s_len), I32),
            pltpu.VMEM((N_IDX_HEADS, tq, 128), BF16),
            pltpu.VMEM((N_IDX_HEADS, tq, 128), F32),
        ],
        compiler_params=_params("parallel"),
        name="indexer_mask",
    )(p_main, p_aux, kt2)


ATT_T = 512
ATT_RC = 32
ATT_SUB = 128


def _attention_kernel(qb_tab, kb_tab, q_ref, k_ref, v_ref, mask_ref, tz_ref, o_ref,
                      qs_ref, mf_ref, p_ref, m_ref, l_ref, acc_ref, *, n_heads, scale):
    step = pl.program_id(0)
    qb = qb_tab[step]
    kb = kb_tab[step]
    dh, t, rc, sub = ATTN_HEAD_DIM, ATT_T, ATT_RC, ATT_SUB
    nsub = t // sub

    @pl.when(kb == 0)
    def _():
        qs_ref[...] = (q_ref[...].astype(F32) * (scale * LOG2E)).astype(BF16)
        m_ref[...] = jnp.full_like(m_ref, NEG_INF)
        l_ref[...] = jnp.zeros_like(l_ref)
        acc_ref[...] = jnp.zeros_like(acc_ref)

    mf_ref[...] = mask_ref[...].astype(F32)
    prev_flag = jnp.where(kb == qb - 1, 1.0, 0.0).astype(F32)

    def run(diag):
        sls = [slice(h * dh, (h + 1) * dh) for h in range(n_heads)]
        s_next = _dot_nt(qs_ref[:, sls[0]], k_ref[:, sls[0]])
        for h in range(n_heads):
            sl = sls[h]
            s = s_next
            if h + 1 < n_heads:
                s_next = _dot_nt(qs_ref[:, sls[h + 1]], k_ref[:, sls[h + 1]])
            alphas = []
            for r in range(t // rc):
                rows = slice(r * rc, (r + 1) * rc)
                blk, r0 = divmod(r * rc, sub)
                sr = s[rows] + mf_ref[rows, :]
                parts = [sr[:, b * sub:(b + 1) * sub] for b in range(nsub)]
                if diag:
                    parts[blk] = parts[blk] + tz_ref[0, h, r0:r0 + rc, :]
                    if blk >= 1:
                        parts[blk - 1] = parts[blk - 1] + tz_ref[1, h, r0:r0 + rc, :]
                elif blk == 0:
                    parts[nsub - 1] = parts[nsub - 1] + prev_flag * tz_ref[1, h, r0:r0 + rc, :]
                m_prev = m_ref[h, rows]
                m_new = jnp.maximum(m_prev, jnp.max(functools.reduce(jnp.maximum, parts), axis=-1, keepdims=True))
                alpha = jnp.exp2(m_prev - m_new)
                ps = [jnp.exp2(pb - m_new) for pb in parts]
                l_ref[h, rows] = alpha * l_ref[h, rows] + jnp.sum(functools.reduce(jnp.add, ps), axis=-1, keepdims=True)
                m_ref[h, rows] = m_new
                p_ref[h, rows, :] = jnp.concatenate(ps, axis=-1).astype(BF16)
                alphas.append(alpha)
            pv = _dot(p_ref[h], v_ref[:, sl])
            for r in range(t // rc):
                rows = slice(r * rc, (r + 1) * rc)
                acc_ref[h, rows] = alphas[r] * acc_ref[h, rows] + pv[rows]

    @pl.when(kb != qb)
    def _():
        run(False)

    @pl.when(kb == qb)
    def _():
        run(True)
        for h in range(n_heads):
            o_ref[:, h * dh:(h + 1) * dh] = (acc_ref[h] / l_ref[h]).astype(o_ref.dtype)


def masked_attention(p_main, mask, bias_tables, n_heads):
    s_len = p_main.shape[0]
    t = ATT_T
    nb = s_len // t
    pairs = [(q, k) for q in range(nb) for k in range(q + 1)]
    qb_tab = jnp.array([p[0] for p in pairs], I32)
    kb_tab = jnp.array([p[1] for p in pairs], I32)
    width = n_heads * ATTN_HEAD_DIM
    grid_spec = pltpu.PrefetchScalarGridSpec(
        num_scalar_prefetch=2,
        grid=(len(pairs),),
        in_specs=[
            pl.BlockSpec((t, width), lambda s, qt, kt: (qt[s], 0)),
            pl.BlockSpec((t, width), lambda s, qt, kt: (kt[s], 1)),
            pl.BlockSpec((t, width), lambda s, qt, kt: (kt[s], 2)),
            pl.BlockSpec((t, t), lambda s, qt, kt: (qt[s], kt[s])),
            pl.BlockSpec((2, n_heads, ATT_SUB, ATT_SUB), lambda s, qt, kt: (0, 0, 0, 0)),
        ],
        out_specs=pl.BlockSpec((t, width), lambda s, qt, kt: (qt[s], 0)),
        scratch_shapes=[
            pltpu.VMEM((t, width), BF16),
            pltpu.VMEM((t, t), F32),
            pltpu.VMEM((n_heads, t, t), BF16),
            pltpu.VMEM((n_heads, t, 128), F32),
            pltpu.VMEM((n_heads, t, 128), F32),
            pltpu.VMEM((n_heads, t, ATTN_HEAD_DIM), F32),
        ],
    )
    return pl.pallas_call(
        functools.partial(_attention_kernel, n_heads=n_heads, scale=ATTN_HEAD_DIM ** -0.5),
        grid_spec=grid_spec,
        out_shape=jax.ShapeDtypeStruct((s_len, width), BF16),
        compiler_params=_params("arbitrary"),
        name="masked_attention",
    )(qb_tab, kb_tab, p_main, p_main, p_main, mask, bias_tables)


def _t5_bucket(dist):
    max_exact = N_BUCKETS // 2
    d = jnp.maximum(dist, 0)
    log_ratio = jnp.log(jnp.maximum(d, 1).astype(F32) / max_exact) / math.log(MAX_DISTANCE / max_exact)
    large = jnp.minimum(max_exact + (log_ratio * (N_BUCKETS - max_exact)).astype(I32), N_BUCKETS - 1)
    return jnp.where(d < max_exact, d, large)


def _bias_tables(rel_bias):
    assert MAX_DISTANCE <= ATT_SUB
    i = jnp.arange(ATT_SUB, dtype=I32)[:, None]
    j = jnp.arange(ATT_SUB, dtype=I32)[None, :]
    bucket = _t5_bucket(jnp.stack([i - j, i - j + ATT_SUB]))[:, None]
    rel = rel_bias.astype(F32)
    far = rel[N_BUCKETS - 1]
    tables = jnp.zeros((2, rel_bias.shape[1], ATT_SUB, ATT_SUB), F32)
    for b in range(N_BUCKETS):
        tables = jnp.where(bucket == b, (rel[b] - far)[None, :, None, None], tables)
    return tables * LOG2E


POOL_TR = 512
POOL_G = 256
HALO = 16


def _pool_kernel(x_ref, prev_ref, w_ref, sc_ref, o_ref):
    i = pl.program_id(0)
    g = pl.program_id(1)
    tr = POOL_TR
    cur = x_ref[...]
    prev = jnp.where(i > 0, prev_ref[...], 0.0)
    ext = jnp.concatenate([prev, cur], axis=0)
    sums = []
    run = ext
    for shift in (1, 2, 4, 8):
        run = run + pltpu.roll(run, shift, 0)
        sums.append(run[HALO:])
    win_sum = jnp.where(g == 0, sums[0], jnp.where(g == 1, sums[1], jnp.where(g == 2, sums[2], sums[3])))
    win = jnp.left_shift(jnp.int32(2), g)
    t_pos = i * tr + lax.broadcasted_iota(I32, (tr, 1), 0)
    count = jnp.minimum(t_pos + 1, win).astype(F32)
    pooled = win_sum / count - cur
    o_ref[...] = (_dot(pooled.astype(BF16), w_ref[0]) * sc_ref[...]).astype(o_ref.dtype)


def multiscale_pool(p_aux, pool_w, pool_scale):
    s_len = p_aux.shape[0]
    tr, gw = POOL_TR, POOL_G
    ng = len(POOL_WINDOWS)
    return pl.pallas_call(
        _pool_kernel,
        grid=(s_len // tr, ng),
        in_specs=[
            pl.BlockSpec((tr, gw), lambda i, g: (i, g)),
            pl.BlockSpec((HALO, gw), lambda i, g: (jnp.maximum(i * (tr // HALO) - 1, 0), g)),
            pl.BlockSpec((1, gw, gw), lambda i, g: (g, 0, 0)),
            pl.BlockSpec((1, gw), lambda i, g: (0, g)),
        ],
        out_specs=pl.BlockSpec((tr, gw), lambda i, g: (i, g)),
        out_shape=jax.ShapeDtypeStruct((s_len, ng * gw), BF16),
        compiler_params=_params("parallel", "parallel"),
        name="multiscale_pool",
    )(p_aux, p_aux, pool_w, pool_scale.reshape(1, ng * gw))


def _split3(x):
    hi = x.astype(BF16)
    r1 = x - hi.astype(F32)
    mid = r1.astype(BF16)
    lo = (r1 - mid.astype(F32)).astype(BF16)
    return hi, mid, lo


def _dot_exact_lhs(a_bf16, x):
    hi, mid, lo = _split3(x)
    return _dot(a_bf16, lo) + _dot(a_bf16, mid) + _dot(a_bf16, hi)


def _dot_exact_rhs(x, b_bf16):
    hi, mid, lo = _split3(x)
    return _dot(lo, b_bf16) + _dot(mid, b_bf16) + _dot(hi, b_bf16)


def _softplus(x):
    return jnp.maximum(x, 0.0) + jnp.log(1.0 + jnp.exp(-jnp.abs(x)))


PREP_CHUNKS = 2


def _gdn_prep_kernel(x_ref, prev_ref, cw_ref, ba_ref, alog_ref, dtb_ref,
                     u_ref, w_ref, qd_ref, kdt_ref, intra_ref, dl_ref, ext_ref, *, n_vheads):
    c = CHUNK
    dk = GDN_DK
    assert 2 * c == 128 and dk == 128 and GDN_DV == 128
    scale = dk ** -0.5
    n_kheads = n_vheads // 2
    step = pl.program_id(0)
    for sb in range(2 * n_kheads + n_vheads):
        cols = slice(sb * 128, (sb + 1) * 128)
        ext_ref[sb, :HALO, :] = jnp.where(step > 0, prev_ref[:, cols].astype(F32), 0.0)
        ext_ref[sb, HALO:, :] = x_ref[:, cols].astype(F32)
    cw = cw_ref[...]

    def conv_head(sb, ci, out_scale):
        cols = slice(sb * 128, (sb + 1) * 128)
        base = HALO + ci * c
        y = ext_ref[sb, base:base + c, :] * cw[CONV_K - 1:CONV_K, cols]
        for tap in range(CONV_K - 1):
            lo = base - (CONV_K - 1 - tap)
            y = y + ext_ref[sb, lo:lo + c, :] * cw[tap:tap + 1, cols]
        y = y / (1.0 + jnp.exp2(y * (-LOG2E)))
        if out_scale is not None:
            y = y * (lax.rsqrt(jnp.sum(y * y, axis=-1, keepdims=True) + RMS_EPS) * out_scale)
        return y

    row = lax.broadcasted_iota(I32, (c, c), 0)
    col = lax.broadcasted_iota(I32, (c, c), 1)
    lower = jnp.where(row >= col, 1.0, 0.0).astype(BF16)
    upper = jnp.where(row <= col, 1.0, 0.0).astype(BF16)
    row_w = lax.broadcasted_iota(I32, (c, 2 * c), 0)
    lane_w = lax.broadcasted_iota(I32, (c, 2 * c), 1)
    first = lane_w < c
    col_w = jnp.where(first, lane_w, lane_w - c)
    tril_w = row_w >= col_w
    strict_w = row_w > col_w
    eye_w = jnp.where(row_w == col_w, 1.0, 0.0).astype(F32)

    def block_diag(wide16):
        zero = jnp.zeros_like(wide16)
        return jnp.concatenate([jnp.where(first, wide16, zero), jnp.where(first, zero, wide16)], axis=0)

    pairs = range(n_vheads // 2)
    pws, t_invs, rhss, where = [], [], [], []
    for ci in range(PREP_CHUNKS):
        rows = slice(ci * c, (ci + 1) * c)
        ba = ba_ref[rows, :]
        beta_all = jax.nn.sigmoid(ba)
        g_all = -jnp.exp(alog_ref[...]) * _softplus(ba + dtb_ref[...])
        gc_all = _dot_exact_lhs(lower, g_all)
        gct_all = _dot_exact_rhs(g_all.T, upper)
        for p in pairs:
            kh = conv_head(n_kheads + p, ci, 1.0)
            qh = conv_head(p, ci, scale)
            kh_t = kh.T
            kh16 = kh.astype(BF16)
            kk16 = jnp.concatenate([kh16, kh16], axis=0)
            both = _dot_nt(jnp.concatenate([kh16, qh.astype(BF16)], axis=0), kk16)
            gram, qk = both[:c], both[c:]
            hv = (2 * p, 2 * p + 1)
            beta = [jnp.broadcast_to(beta_all[:, h:h + 1], (c, 2 * c)) for h in hv]
            gcol = [jnp.broadcast_to(gc_all[:, n_vheads + h:n_vheads + h + 1], (c, 2 * c)) for h in hv]
            grow = [gct_all[n_vheads + h:n_vheads + h + 1, :] for h in hv]
            grow_w = jnp.concatenate(grow, axis=1)
            gcol_w = jnp.where(first, gcol[0], gcol[1])
            decay_w = jnp.where(tril_w, jnp.exp(jnp.where(tril_w, gcol_w - grow_w, 0.0)), 0.0)
            a_w = jnp.where(strict_w, -(jnp.where(first, beta[0], beta[1]) * gram * decay_w), 0.0)
            intra_ref[ci, p] = jnp.where(tril_w, qk * decay_w, 0.0).astype(BF16)
            rhs = []
            for i, h in enumerate(hv):
                eg = jnp.exp(gcol[i])
                g_last = grow[i][:, c - 1:c]
                sl = slice(h * GDN_DV, (h + 1) * GDN_DV)
                qd_ref[rows, sl] = (qh * eg).astype(BF16)
                kdt_ref[ci, h] = (kh_t * jnp.exp(g_last - grow[i])).astype(BF16)
                dl_ref[ci, h] = jnp.broadcast_to(jnp.exp(g_last), (1, 128))
                vh = conv_head(2 * n_kheads + h, ci, None)
                rhs.append(jnp.concatenate([vh * beta[i], kh * (beta[i] * eg)], axis=-1).astype(BF16))
            pws.append(a_w)
            t_invs.append(eye_w + a_w)
            rhss.append(jnp.concatenate(rhs, axis=0))
            where.append((rows, p))
    n_levels = int(math.log2(c)) - 1
    pws = [_dot(a.astype(BF16), block_diag(a.astype(BF16))) for a in pws]
    for level in range(n_levels):
        new_t, new_pw = [], []
        for t, pw in zip(t_invs, pws):
            pw16 = pw.astype(BF16)
            if level == n_levels - 1:
                new_t.append(t + _dot(t.astype(BF16), block_diag(pw16)))
            else:
                both = _dot(jnp.concatenate([t.astype(BF16), pw16], axis=0), block_diag(pw16))
                new_t.append(t + both[:c])
                new_pw.append(both[c:])
        t_invs, pws = new_t, new_pw
    for (rows, p), t_inv, rhs in zip(where, t_invs, rhss):
        t16 = t_inv.astype(BF16)
        zero = jnp.zeros_like(t16)
        lhs = jnp.concatenate([jnp.where(first, t16, zero), jnp.where(first, zero, t16)], axis=0)
        sol = _dot(lhs, rhs)
        for i, h in enumerate((2 * p, 2 * p + 1)):
            sl = slice(h * GDN_DV, (h + 1) * GDN_DV)
            u_ref[rows, sl] = sol[i * c:(i + 1) * c, :GDN_DV]
            w_ref[rows, sl] = sol[i * c:(i + 1) * c, GDN_DV:].astype(BF16)


def gdn_prepare(p_main, conv_w, ba, a_log, dt_bias, n_kheads, n_vheads):
    s_len = p_main.shape[0]
    nc = PREP_CHUNKS
    c = CHUNK * nc
    n = s_len // c
    v_w = n_vheads * GDN_DV
    conv_width = 2 * n_kheads * GDN_DK + v_w
    assert 2 * n_vheads <= 128 and n_vheads == 2 * n_kheads and c % HALO == 0
    pad = jnp.zeros((128 - 2 * n_vheads,), F32)
    alog_row = jnp.concatenate([jnp.zeros((n_vheads,), F32), a_log, pad]).reshape(1, 128)
    dtb_row = jnp.concatenate([jnp.zeros((n_vheads,), F32), dt_bias, pad]).reshape(1, 128)
    return pl.pallas_call(
        functools.partial(_gdn_prep_kernel, n_vheads=n_vheads),
        grid=(n,),
        in_specs=[
            pl.BlockSpec((c, conv_width), lambda i: (i, 0)),
            pl.BlockSpec((HALO, conv_width), lambda i: (jnp.maximum(i * (c // HALO) - 1, 0), 0)),
            pl.BlockSpec((CONV_K, conv_width), lambda i: (0, 0)),
            pl.BlockSpec((c, 128), lambda i: (i, 0)),
            pl.BlockSpec((1, 128), lambda i: (0, 0)),
            pl.BlockSpec((1, 128), lambda i: (0, 0)),
        ],
        out_specs=[
            pl.BlockSpec((c, v_w), lambda i: (i, 0)),
            pl.BlockSpec((c, v_w), lambda i: (i, 0)),
            pl.BlockSpec((c, v_w), lambda i: (i, 0)),
            pl.BlockSpec((nc, n_vheads, GDN_DK, CHUNK), lambda i: (i, 0, 0, 0)),
            pl.BlockSpec((nc, n_vheads // 2, CHUNK, 2 * CHUNK), lambda i: (i, 0, 0, 0)),
            pl.BlockSpec((nc, n_vheads, 1, 128), lambda i: (i, 0, 0, 0)),
        ],
        out_shape=[
            jax.ShapeDtypeStruct((s_len, v_w), F32),
            jax.ShapeDtypeStruct((s_len, v_w), BF16),
            jax.ShapeDtypeStruct((s_len, v_w), BF16),
            jax.ShapeDtypeStruct((n * nc, n_vheads, GDN_DK, CHUNK), BF16),
            jax.ShapeDtypeStruct((n * nc, n_vheads // 2, CHUNK, 2 * CHUNK), BF16),
            jax.ShapeDtypeStruct((n * nc, n_vheads, 1, 128), F32),
        ],
        scratch_shapes=[pltpu.VMEM((conv_width // 128, HALO + c, 128), F32)],
        compiler_params=_params("parallel"),
        name="gdn_prepare",
    )(p_main, p_main, conv_w, ba, alog_row, dtb_row)


SCAN_HG = 16
SCAN_CHUNKS = 8


def _gdn_scan_kernel(u_ref, w_ref, qd_ref, kdt_ref, intra_ref, dl_ref, z_ref, nw_ref, o_ref, state_ref):
    @pl.when(pl.program_id(1) == 0)
    def _():
        state_ref[...] = jnp.zeros_like(state_ref)

    nw = nw_ref[...]
    heads = range(SCAN_HG)
    sls = [slice(j * GDN_DV, (j + 1) * GDN_DV) for j in heads]
    first = lax.broadcasted_iota(I32, (CHUNK, 2 * CHUNK), 1) < CHUNK
    states = [state_ref[j] for j in heads]
    for ci in range(SCAN_CHUNKS):
        rows = slice(ci * CHUNK, (ci + 1) * CHUNK)
        s16 = [s.astype(BF16) for s in states]
        ws = [_dot(w_ref[rows, sls[j]], s16[j]) for j in heads]
        qs = [_dot(qd_ref[rows, sls[j]], s16[j]) for j in heads]
        v16 = [(u_ref[rows, sls[j]] - ws[j]).astype(BF16) for j in heads]
        upd = [_dot(kdt_ref[ci, j], v16[j]) for j in heads]
        outs = []
        for p in range(SCAN_HG // 2):
            pair = intra_ref[ci, p]
            zero = jnp.zeros_like(pair)
            v_pair = jnp.concatenate([v16[2 * p], v16[2 * p + 1]], axis=0)
            outs.append(qs[2 * p] + _dot(jnp.where(first, pair, zero), v_pair))
            outs.append(qs[2 * p + 1] + _dot(jnp.where(first, zero, pair), v_pair))
        states = [states[j] * dl_ref[ci, j] + upd[j] for j in heads]
        for j in heads:
            z = z_ref[rows, sls[j]].astype(F32)
            o_ref[rows, sls[j]] = (_rms(outs[j], nw) * (z * jax.nn.sigmoid(z))).astype(o_ref.dtype)
    for j in heads:
        state_ref[j] = states[j]


def gdn_scan(u, w, qd, kdt, intra, dl, p_main, norm_w, z_col_block):
    s_len, v_w = u.shape
    nc = SCAN_CHUNKS
    c = CHUNK * nc
    n = s_len // c
    hg = SCAN_HG
    gw = hg * GDN_DV
    return pl.pallas_call(
        _gdn_scan_kernel,
        grid=(v_w // gw, n),
        in_specs=[
            pl.BlockSpec((c, gw), lambda g, i: (i, g)),
            pl.BlockSpec((c, gw), lambda g, i: (i, g)),
            pl.BlockSpec((c, gw), lambda g, i: (i, g)),
            pl.BlockSpec((nc, hg, GDN_DK, CHUNK), lambda g, i: (i, g, 0, 0)),
            pl.BlockSpec((nc, hg // 2, CHUNK, 2 * CHUNK), lambda g, i: (i, g, 0, 0)),
            pl.BlockSpec((nc, hg, 1, 128), lambda g, i: (i, g, 0, 0)),
            pl.BlockSpec((c, gw), lambda g, i: (i, z_col_block + g)),
            pl.BlockSpec((1, GDN_DV), lambda g, i: (0, 0)),
        ],
        out_specs=pl.BlockSpec((c, gw), lambda g, i: (i, g)),
        out_shape=jax.ShapeDtypeStruct((s_len, v_w), BF16),
        scratch_shapes=[pltpu.VMEM((hg, GDN_DK, GDN_DV), F32)],
        compiler_params=_params("parallel", "arbitrary"),
        name="gdn_scan",
    )(u, w, qd, kdt, intra, dl, p_main, norm_w.reshape(1, GDN_DV))


PROJ_TM = 1024
PROJ_TN = 1024
FFN_TM = 512
FFN_TF = 512
OUT_TM = 512
OUT_TK_MAX = 2048


def _out_tk(k):
    return max(c for c in range(128, OUT_TK_MAX + 1, 128) if k % c == 0)


def _even_main_width(d):
    return 3 * (d // 2) + N_IDX_HEADS * IDX_HEAD_DIM


def _odd_main_width(d):
    n_kheads = d // 128
    return 2 * n_kheads * GDN_DK + 2 * (2 * n_kheads) * GDN_DV


def _even_mixer(x, g_pre, g_post, w_in16t, w_rest, w_o16, layer, pool_w, pool_scale, bias_tables):
    s_len, d = x.shape
    attn_w = d // 2
    n_heads = attn_w // ATTN_HEAD_DIM
    idx_w = N_IDX_HEADS * IDX_HEAD_DIM
    pool_width = len(POOL_WINDOWS) * POOL_G
    main_w = _even_main_width(d)
    assert main_w == 3 * attn_w + idx_w == 4096 and pool_width == 1024
    small = IDX_HEAD_DIM + N_IDX_HEADS
    assert w_rest.shape[0] == small + pool_width
    w_aux = jnp.concatenate([w_rest[small:], w_rest[:small], jnp.zeros((128 - small, d), BF16)], axis=0)
    p_main = norm_matmul(x, g_pre, w_in16t, layer, main_w, BF16, PROJ_TM, PROJ_TN)
    p_aux = norm_matmul(x, g_pre, w_aux[None], 0, w_aux.shape[0], F32, PROJ_TM, w_aux.shape[0])
    k_idx_t = p_aux[:, pool_width:pool_width + IDX_HEAD_DIM].astype(BF16).T
    kt2 = jnp.concatenate([k_idx_t, k_idx_t], axis=0)
    mask = indexer_mask(p_main, p_aux, kt2, min(TOPK_MAX, s_len // 4))
    attn = masked_attention(p_main, mask, bias_tables, n_heads)
    pooled = multiscale_pool(p_aux, pool_w.astype(BF16), pool_scale)
    return mm_norm_res([attn, pooled], w_o16, layer, g_post, x, OUT_TM, _out_tk(attn.shape[1]))


def _odd_mixer(x, g_pre, g_post, w_in16t, w_rest, w_o16, layer, conv_w, a_log, dt_bias, norm_w):
    s_len, d = x.shape
    n_kheads = d // 128
    n_vheads = 2 * n_kheads
    qk_w = n_kheads * GDN_DK
    v_w = n_vheads * GDN_DV
    conv_width = 2 * qk_w + v_w
    main_w = _odd_main_width(d)
    assert main_w == conv_width + v_w
    w_ba = jnp.concatenate([w_rest, jnp.zeros((128 - 2 * n_vheads, d), BF16)], axis=0)
    p_main = norm_matmul(x, g_pre, w_in16t, layer, main_w, BF16, PROJ_TM, PROJ_TN)
    ba = norm_matmul(x, g_pre, w_ba[None], 0, 128, F32, PROJ_TM, 128)
    u, w, qd, kdt, intra, dl = gdn_prepare(p_main, conv_w, ba, a_log, dt_bias, n_kheads, n_vheads)
    o = gdn_scan(u, w, qd, kdt, intra, dl, p_main, norm_w, conv_width // (SCAN_HG * GDN_DV))
    return mm_norm_res([o], w_o16, layer, g_post, x, OUT_TM, _out_tk(v_w))


def _ffn(x, g_pre, g_post, w_gate16, w_up16, w_down16, layer):
    return ffn(x, g_pre, g_post, w_gate16, w_up16, w_down16, layer, FFN_TM, FFN_TF)


def kernel(x, rel_bias, even_w_in, even_w_o, pool_w, pool_scale, odd_w_in, conv_w, a_log, dt_bias,
           gdn_norm_w, odd_w_o, ffn_w_gate, ffn_w_up, ffn_w_down, mix_pre_g, mix_post_g, ffn_pre_g, ffn_post_g):
    b, s_len, d = x.shape
    depth = ffn_w_gate.shape[0]
    bias_tables = _bias_tables(rel_bias)
    em, om = _even_main_width(d), _odd_main_width(d)
    even_t, odd_t = jnp.swapaxes(even_w_in, 1, 2), jnp.swapaxes(odd_w_in, 1, 2)
    even_in16, odd_in16 = cast_bf16(even_t, 0, em), cast_bf16(odd_t, 0, om)
    even_rest16, odd_rest16 = even_t[:, em:].astype(BF16), odd_t[:, om:].astype(BF16)
    even_o16, odd_o16 = cast_bf16(even_w_o), cast_bf16(odd_w_o)
    gate16, up16, down16 = cast_bf16(ffn_w_gate), cast_bf16(ffn_w_up), cast_bf16(ffn_w_down)
    xs = x.reshape(b * s_len, d)
    outs = []
    for bi in range(b):
        h = xs[bi * s_len:(bi + 1) * s_len]
        for layer in range(depth):
            i = layer // 2
            if layer % 2 == 0:
                h = _even_mixer(h, mix_pre_g[layer], mix_post_g[layer], even_in16, even_rest16[i], even_o16, i,
                                pool_w[i], pool_scale[i], bias_tables)
            else:
                h = _odd_mixer(h, mix_pre_g[layer], mix_post_g[layer], odd_in16, odd_rest16[i], odd_o16, i,
                               conv_w[i], a_log[i], dt_bias[i], gdn_norm_w[i])
            h = _ffn(h, ffn_pre_g[layer], ffn_post_g[layer], gate16, up16, down16, layer)
        outs.append(h)
    return jnp.concatenate(outs, axis=0).reshape(b, s_len, d)
```

```python
import functools
import math

import jax
import jax.numpy as jnp
from jax import lax
from jax.experimental import pallas as pl
from jax.experimental.pallas import tpu as pltpu

F32 = jnp.float32
BF16 = jnp.bfloat16
I32 = jnp.int32

RMS_EPS = 1e-6
NEG_INF = -1e30
INT_MIN = -2 ** 31
LOG2E = 1.4426950408889634

VMEM_LIMIT_BYTES = 56 * 1024 * 1024

ATTN_HEAD_DIM = 128
N_IDX_HEADS = 16
IDX_HEAD_DIM = 64
TOPK_MAX = 256
N_BUCKETS = 32
MAX_DISTANCE = 128
POOL_WINDOWS = (2, 4, 8, 16)
GDN_DK = 128
GDN_DV = 128
CONV_K = 4
CHUNK = 64


def _params(*sem):
    return pltpu.CompilerParams(dimension_semantics=sem, vmem_limit_bytes=VMEM_LIMIT_BYTES)


def _dot(a, b):
    return jnp.dot(a, b, preferred_element_type=F32)


def _dot_nt(a, b):
    return lax.dot_general(a, b, (((1,), (1,)), ((), ())), preferred_element_type=F32)


def _to_u32(x):
    return lax.bitcast_convert_type(x ^ INT_MIN, jnp.uint32)


def _to_i32(x):
    return lax.bitcast_convert_type(x, I32)


def _rms(x, g):
    ms = jnp.mean(x * x, axis=-1, keepdims=True)
    return x * lax.rsqrt(ms + RMS_EPS) * g


def _norm_matmul_kernel(x_ref, g_ref, wt_ref, o_ref, hn_ref):
    @pl.when(pl.program_id(1) == 0)
    def _():
        hn_ref[...] = _rms(x_ref[...], g_ref[...]).astype(BF16)

    o_ref[...] = _dot_nt(hn_ref[...], wt_ref[...]).astype(o_ref.dtype)


def norm_matmul(x, g, wt, layer, n, out_dtype, tm, tn):
    m, k = x.shape
    assert m % tm == 0 and n % tn == 0 and n <= wt.shape[1] and wt.shape[2] == k
    return pl.pallas_call(
        _norm_matmul_kernel,
        grid=(m // tm, n // tn),
        in_specs=[
            pl.BlockSpec((tm, k), lambda i, j: (i, 0)),
            pl.BlockSpec((1, k), lambda i, j: (0, 0)),
            pl.BlockSpec((None, tn, k), lambda i, j: (layer, j, 0)),
        ],
        out_specs=pl.BlockSpec((tm, tn), lambda i, j: (i, j)),
        out_shape=jax.ShapeDtypeStruct((m, n), out_dtype),
        scratch_shapes=[pltpu.VMEM((tm, k), BF16)],
        compiler_params=_params("parallel", "arbitrary"),
        name="norm_matmul",
    )(x, g.reshape(1, k), wt)


def _ffn_kernel(x_ref, g_ref, wg_ref, wu_ref, wd_ref, gp_ref, o_ref, hn_ref, acc_ref, *, nj):
    j = pl.program_id(1)

    @pl.when(j == 0)
    def _():
        hn_ref[...] = _rms(x_ref[...], g_ref[...]).astype(BF16)
        acc_ref[...] = jnp.zeros_like(acc_ref)

    hn = hn_ref[...]
    a = _dot(hn, wg_ref[...])
    b = _dot(hn, wu_ref[...])
    act = (a * jax.nn.sigmoid(a) * b).astype(BF16)
    acc_ref[...] += _dot(act, wd_ref[...])

    @pl.when(j == nj - 1)
    def _():
        o_ref[...] = x_ref[...] + _rms(acc_ref[...], gp_ref[...])


def ffn(x, g_pre, g_post, wg, wu, wd, layer, tm, tf):
    m, d = x.shape
    ff = wg.shape[2]
    assert m % tm == 0 and ff % tf == 0
    nj = ff // tf
    return pl.pallas_call(
        functools.partial(_ffn_kernel, nj=nj),
        grid=(m // tm, nj),
        in_specs=[
            pl.BlockSpec((tm, d), lambda i, j: (i, 0)),
            pl.BlockSpec((1, d), lambda i, j: (0, 0)),
            pl.BlockSpec((None, d, tf), lambda i, j: (layer, 0, j)),
            pl.BlockSpec((None, d, tf), lambda i, j: (layer, 0, j)),
            pl.BlockSpec((None, tf, d), lambda i, j: (layer, j, 0)),
            pl.BlockSpec((1, d), lambda i, j: (0, 0)),
        ],
        out_specs=pl.BlockSpec((tm, d), lambda i, j: (i, 0)),
        out_shape=jax.ShapeDtypeStruct((m, d), F32),
        scratch_shapes=[pltpu.VMEM((tm, d), BF16), pltpu.VMEM((tm, d), F32)],
        compiler_params=_params("parallel", "arbitrary"),
        name="ffn",
    )(x, g_pre.reshape(1, d), wg, wu, wd, g_post.reshape(1, d))


def _mm_norm_res_kernel(*refs, n_seg, nk):
    a_refs = refs[:n_seg]
    w_refs = refs[n_seg:2 * n_seg]
    g_ref, r_ref, o_ref, acc_ref = refs[2 * n_seg:]
    kk = pl.program_id(1)

    @pl.when(kk == 0)
    def _():
        acc_ref[...] = jnp.zeros_like(acc_ref)

    part = _dot(a_refs[0][...], w_refs[0][...])
    for s in range(1, n_seg):
        part = part + _dot(a_refs[s][...], w_refs[s][...])
    acc_ref[...] += part

    @pl.when(kk == nk - 1)
    def _():
        o_ref[...] = r_ref[...] + _rms(acc_ref[...], g_ref[...])


def mm_norm_res(a_list, w, layer, g, res, tm, tk):
    n_seg = len(a_list)
    m, ka = a_list[0].shape
    n = w.shape[2]
    assert w.shape[1] == n_seg * ka and m % tm == 0 and ka % tk == 0
    nk = ka // tk
    a_specs = [pl.BlockSpec((tm, tk), lambda i, k: (i, k)) for _ in range(n_seg)]
    w_specs = [pl.BlockSpec((None, tk, n), functools.partial(lambda i, k, s: (layer, k + s * nk, 0), s=s))
               for s in range(n_seg)]
    return pl.pallas_call(
        functools.partial(_mm_norm_res_kernel, n_seg=n_seg, nk=nk),
        grid=(m // tm, nk),
        in_specs=a_specs + w_specs + [
            pl.BlockSpec((1, n), lambda i, k: (0, 0)),
            pl.BlockSpec((tm, n), lambda i, k: (i, 0)),
        ],
        out_specs=pl.BlockSpec((tm, n), lambda i, k: (i, 0)),
        out_shape=jax.ShapeDtypeStruct((m, n), F32),
        scratch_shapes=[pltpu.VMEM((tm, n), F32)],
        compiler_params=_params("parallel", "arbitrary"),
        name="mm_norm_res",
    )(*a_list, *([w] * n_seg), g.reshape(1, n), res)


CAST_BLOCK_BYTES = 4 * 1024 * 1024


def _cast_kernel(x_ref, o_ref):
    o_ref[...] = x_ref[...].astype(o_ref.dtype)


def cast_bf16(w, row0=0, height=None):
    layers, n, k = w.shape
    height = n - row0 if height is None else height
    assert row0 + height <= n
    tr = max(r for r in range(16, height + 1, 16)
             if height % r == 0 and row0 % r == 0 and r * k * 4 <= CAST_BLOCK_BYTES)
    return pl.pallas_call(
        _cast_kernel,
        grid=(layers, height // tr),
        in_specs=[pl.BlockSpec((None, tr, k), lambda l, i: (l, row0 // tr + i, 0))],
        out_specs=pl.BlockSpec((None, tr, k), lambda l, i: (l, i, 0)),
        out_shape=jax.ShapeDtypeStruct((layers, height, k), BF16),
        compiler_params=_params("parallel", "parallel"),
        name="cast_bf16",
    )(w)


IDX_TQ = 128
IDX_TK = 512
IDX_ROWS = 128
IDX_STEPS_PER_CHECK = 4


def _indexer_kernel(qi_ref, aux_ref, kt_ref, mask_ref, key_ref, qh_ref, wb_ref, *, topk, w_scale):
    tq, tk, tr = IDX_TQ, IDX_TK, IDX_ROWS
    nj = tk // 128
    s_len = mask_ref.shape[1]
    qb = pl.program_id(0)
    n_chunks = ((qb + 1) * tq - 1) // tk + 1

    lane = lax.broadcasted_iota(I32, (tq, 128), 1)
    aux = aux_ref[...]
    for h in range(N_IDX_HEADS):
        pair = qi_ref[:, (h // 2) * 128:(h // 2 + 1) * 128]
        keep = (lane < IDX_HEAD_DIM) if h % 2 == 0 else (lane >= IDX_HEAD_DIM)
        qh_ref[h] = jnp.where(keep, pair, jnp.zeros_like(pair))
        wcol = aux[:, IDX_HEAD_DIM + h:IDX_HEAD_DIM + h + 1] * w_scale
        wb_ref[h] = jnp.broadcast_to(wcol, (tq, 128))

    lane_r = lax.broadcasted_iota(I32, (tr, 128), 1)
    row_r = lax.broadcasted_iota(I32, (tr, 128), 0)

    def score_chunk(c, tops):
        col0 = pl.multiple_of(c * tk, tk)
        kt = kt_ref[:, pl.ds(col0, tk)]
        new_tops = []
        for r in range(tq // tr):
            rows = slice(r * tr, (r + 1) * tr)
            accs = [jnp.zeros((tr, 128), F32) for _ in range(nj)]
            for h in range(N_IDX_HEADS):
                d = _dot(qh_ref[h, rows], kt)
                wb = wb_ref[h, rows]
                for j in range(nj):
                    accs[j] = accs[j] + wb * jnp.maximum(d[:, j * 128:(j + 1) * 128], 0.0)
            q_pos = qb * tq + r * tr + row_r
            m1, m2 = tops[r]
            for j in range(nj):
                bits = pltpu.bitcast(accs[j], I32)
                skey = bits ^ ((bits >> 31) & 0x7FFFFFFF)
                skey = jnp.where(col0 + j * 128 + lane_r <= q_pos, skey, INT_MIN)
                key_ref[rows, pl.ds(pl.multiple_of(col0 + j * 128, 128), 128)] = skey
                m2 = jnp.maximum(m2, jnp.minimum(m1, skey))
                m1 = jnp.maximum(m1, skey)
            new_tops.append((m1, m2))
        return tuple(new_tops)

    lowest = jnp.full((tr, 128), INT_MIN, I32)
    tops = lax.fori_loop(0, n_chunks, score_chunk, tuple((lowest, lowest) for _ in range(tq // tr)))
    m1 = jnp.concatenate([t[0] for t in tops], axis=0)
    m2 = jnp.concatenate([t[1] for t in tops], axis=0)
    assert topk <= 2 * 128 and 127 + s_len // 128 < topk
    top = jnp.max(m2, axis=-1, keepdims=True)
    n_top = jnp.sum(jnp.where(m2 == top, 1, 0), axis=-1, keepdims=True)
    below = jnp.max(jnp.where(m2 == top, INT_MIN, m2), axis=-1, keepdims=True)
    hi = _to_u32(jnp.where(n_top >= 2, top, below))
    lo = _to_u32(jnp.min(m2, axis=-1, keepdims=True))
    n_iter = 32 - jnp.max(lax.clz(_to_i32(hi - lo)))

    def count_ge(cand):
        cand_b = jnp.broadcast_to(cand, (tq, 128))

        def body(c, cnt):
            blk = key_ref[:, pl.ds(pl.multiple_of(c * tk, tk), tk)]
            for j in range(nj):
                cnt = cnt + jnp.where(blk[:, j * 128:(j + 1) * 128] >= cand_b, 1, 0)
            return cnt

        cnt = lax.fori_loop(0, n_chunks, body, jnp.zeros((tq, 128), I32))
        return jnp.sum(cnt, axis=-1, keepdims=True)

    def not_settled(state):
        i, _, _, _, settled = state
        return jnp.logical_and(i < n_iter, settled == 0)

    def bisect(state):
        i, lo, hi, n_lo, _ = state
        for _ in range(IDX_STEPS_PER_CHECK):
            width = hi - lo
            mid = lo + (width >> 1) + (width & 1)
            n_mid = count_ge(_to_i32(mid) ^ INT_MIN)
            ok = n_mid >= topk
            lo = jnp.where(ok, mid, lo)
            hi = jnp.where(ok, hi, jnp.where(width > 0, mid - 1, hi))
            n_lo = jnp.where(ok, n_mid, n_lo)
        settled = jnp.min(jnp.where(jnp.logical_or(n_lo == topk, hi == lo), 1, 0))
        return i + IDX_STEPS_PER_CHECK, lo, hi, n_lo, settled

    unknown = jnp.full((tq, 1), s_len + 1, I32)
    _, tau_u, _, _, _ = lax.while_loop(not_settled, bisect, (jnp.int32(0), lo, hi, unknown, jnp.int32(0)))
    tau_u = _to_i32(tau_u)
    tau = jnp.broadcast_to(jnp.maximum(tau_u ^ INT_MIN, INT_MIN + 1), (tq, 128))

    neg = jnp.full((tq, tk), NEG_INF, F32).astype(BF16)

    def write_chunk(c, carry):
        col0 = pl.multiple_of(c * tk, tk)

        @pl.when(c < n_chunks)
        def _():
            blk = key_ref[:, pl.ds(col0, tk)]
            parts = [jnp.where(blk[:, j * 128:(j + 1) * 128] >= tau, 0.0, NEG_INF) for j in range(nj)]
            mask_ref[:, pl.ds(col0, tk)] = jnp.concatenate(parts, axis=-1).astype(BF16)

        @pl.when(c >= n_chunks)
        def _():
            mask_ref[:, pl.ds(col0, tk)] = neg

        return carry

    lax.fori_loop(0, s_len // tk, write_chunk, 0)


def indexer_mask(p_main, p_aux, kt2, topk):
    s_len = p_main.shape[0]
    tq = IDX_TQ
    w_scale = (N_IDX_HEADS ** -0.5) * (IDX_HEAD_DIM ** -0.5)
    return pl.pallas_call(
        functools.partial(_indexer_kernel, topk=topk, w_scale=w_scale),
        grid=(s_len // tq,),
        in_specs=[
            pl.BlockSpec((tq, 1024), lambda i: (i, 3)),
            pl.BlockSpec((tq, 128), lambda i: (i, 8)),
            pl.BlockSpec((128, s_len), lambda i: (0, 0)),
        ],
        out_specs=pl.BlockSpec((tq, s_len), lambda i: (i, 0)),
        out_shape=jax.ShapeDtypeStruct((s_len, s_len), BF16),
        scratch_shapes=[
            pltpu.VMEM((tq, s_len), I32),
            pltpu.VMEM((N_IDX_HEADS, tq, 128), BF16),
            pltpu.VMEM((N_IDX_HEADS, tq, 128), F32),
        ],
        compiler_params=_params("parallel"),
        name="indexer_mask",
    )(p_main, p_aux, kt2)


ATT_T = 512
ATT_RC = 32
ATT_SUB = 128


def _attention_kernel(qb_tab, kb_tab, q_ref, k_ref, v_ref, mask_ref, tz_ref, o_ref,
                      qs_ref, mf_ref, p_ref, m_ref, l_ref, acc_ref, *, n_heads, scale):
    step = pl.program_id(0)
    qb = qb_tab[step]
    kb = kb_tab[step]
    dh, t, rc, sub = ATTN_HEAD_DIM, ATT_T, ATT_RC, ATT_SUB
    nsub = t // sub

    @pl.when(kb == 0)
    def _():
        qs_ref[...] = (q_ref[...].astype(F32) * (scale * LOG2E)).astype(BF16)
        m_ref[...] = jnp.full_like(m_ref, NEG_INF)
        l_ref[...] = jnp.zeros_like(l_ref)
        acc_ref[...] = jnp.zeros_like(acc_ref)

    mf_ref[...] = mask_ref[...].astype(F32)
    prev_flag = jnp.where(kb == qb - 1, 1.0, 0.0).astype(F32)

    def run(diag):
        sls = [slice(h * dh, (h + 1) * dh) for h in range(n_heads)]
        s_next = _dot_nt(qs_ref[:, sls[0]], k_ref[:, sls[0]])
        for h in range(n_heads):
            sl = sls[h]
            s = s_next
            if h + 1 < n_heads:
                s_next = _dot_nt(qs_ref[:, sls[h + 1]], k_ref[:, sls[h + 1]])
            alphas = []
            for r in range(t // rc):
                rows = slice(r * rc, (r + 1) * rc)
                blk, r0 = divmod(r * rc, sub)
                sr = s[rows] + mf_ref[rows, :]
                parts = [sr[:, b * sub:(b + 1) * sub] for b in range(nsub)]
                if diag:
                    parts[blk] = parts[blk] + tz_ref[0, h, r0:r0 + rc, :]
                    if blk >= 1:
                        parts[blk - 1] = parts[blk - 1] + tz_ref[1, h, r0:r0 + rc, :]
                elif blk == 0:
                    parts[nsub - 1] = parts[nsub - 1] + prev_flag * tz_ref[1, h, r0:r0 + rc, :]
                m_prev = m_ref[h, rows]
                m_new = jnp.maximum(m_prev, jnp.max(functools.reduce(jnp.maximum, parts), axis=-1, keepdims=True))
                alpha = jnp.exp2(m_prev - m_new)
                ps = [jnp.exp2(pb - m_new) for pb in parts]
                l_ref[h, rows] = alpha * l_ref[h, rows] + jnp.sum(functools.reduce(jnp.add, ps), axis=-1, keepdims=True)
                m_ref[h, rows] = m_new
                p_ref[h, rows, :] = jnp.concatenate(ps, axis=-1).astype(BF16)
                alphas.append(alpha)
            pv = _dot(p_ref[h], v_ref[:, sl])
            for r in range(t // rc):
                rows = slice(r * rc, (r + 1) * rc)
                acc_ref[h, rows] = alphas[r] * acc_ref[h, rows] + pv[rows]

    @pl.when(kb != qb)
    def _():
        run(False)

    @pl.when(kb == qb)
    def _():
        run(True)
        for h in range(n_heads):
            o_ref[:, h * dh:(h + 1) * dh] = (acc_ref[h] / l_ref[h]).astype(o_ref.dtype)


def masked_attention(p_main, mask, bias_tables, n_heads):
    s_len = p_main.shape[0]
    t = ATT_T
    nb = s_len // t
    pairs = [(q, k) for q in range(nb) for k in range(q + 1)]
    qb_tab = jnp.array([p[0] for p in pairs], I32)
    kb_tab = jnp.array([p[1] for p in pairs], I32)
    width = n_heads * ATTN_HEAD_DIM
    grid_spec = pltpu.PrefetchScalarGridSpec(
        num_scalar_prefetch=2,
        grid=(len(pairs),),
        in_specs=[
            pl.BlockSpec((t, width), lambda s, qt, kt: (qt[s], 0)),
            pl.BlockSpec((t, width), lambda s, qt, kt: (kt[s], 1)),
            pl.BlockSpec((t, width), lambda s, qt, kt: (kt[s], 2)),
            pl.BlockSpec((t, t), lambda s, qt, kt: (qt[s], kt[s])),
            pl.BlockSpec((2, n_heads, ATT_SUB, ATT_SUB), lambda s, qt, kt: (0, 0, 0, 0)),
        ],
        out_specs=pl.BlockSpec((t, width), lambda s, qt, kt: (qt[s], 0)),
        scratch_shapes=[
            pltpu.VMEM((t, width), BF16),
            pltpu.VMEM((t, t), F32),
            pltpu.VMEM((n_heads, t, t), BF16),
            pltpu.VMEM((n_heads, t, 128), F32),
            pltpu.VMEM((n_heads, t, 128), F32),
            pltpu.VMEM((n_heads, t, ATTN_HEAD_DIM), F32),
        ],
    )
    return pl.pallas_call(
        functools.partial(_attention_kernel, n_heads=n_heads, scale=ATTN_HEAD_DIM ** -0.5),
        grid_spec=grid_spec,
        out_shape=jax.ShapeDtypeStruct((s_len, width), BF16),
        compiler_params=_params("arbitrary"),
        name="masked_attention",
    )(qb_tab, kb_tab, p_main, p_main, p_main, mask, bias_tables)


def _t5_bucket(dist):
    max_exact = N_BUCKETS // 2
    d = jnp.maximum(dist, 0)
    log_ratio = jnp.log(jnp.maximum(d, 1).astype(F32) / max_exact) / math.log(MAX_DISTANCE / max_exact)
    large = jnp.minimum(max_exact + (log_ratio * (N_BUCKETS - max_exact)).astype(I32), N_BUCKETS - 1)
    return jnp.where(d < max_exact, d, large)


def _bias_tables(rel_bias):
    assert MAX_DISTANCE <= ATT_SUB
    i = jnp.arange(ATT_SUB, dtype=I32)[:, None]
    j = jnp.arange(ATT_SUB, dtype=I32)[None, :]
    bucket = _t5_bucket(jnp.stack([i - j, i - j + ATT_SUB]))[:, None]
    rel = rel_bias.astype(F32)
    far = rel[N_BUCKETS - 1]
    tables = jnp.zeros((2, rel_bias.shape[1], ATT_SUB, ATT_SUB), F32)
    for b in range(N_BUCKETS):
        tables = jnp.where(bucket == b, (rel[b] - far)[None, :, None, None], tables)
    return tables * LOG2E


POOL_TR = 512
POOL_G = 256
HALO = 16


def _pool_kernel(x_ref, prev_ref, w_ref, sc_ref, o_ref):
    i = pl.program_id(0)
    g = pl.program_id(1)
    tr = POOL_TR
    cur = x_ref[...]
    prev = jnp.where(i > 0, prev_ref[...], 0.0)
    ext = jnp.concatenate([prev, cur], axis=0)
    sums = []
    run = ext
    for shift in (1, 2, 4, 8):
        run = run + pltpu.roll(run, shift, 0)
        sums.append(run[HALO:])
    win_sum = jnp.where(g == 0, sums[0], jnp.where(g == 1, sums[1], jnp.where(g == 2, sums[2], sums[3])))
    win = jnp.left_shift(jnp.int32(2), g)
    t_pos = i * tr + lax.broadcasted_iota(I32, (tr, 1), 0)
    count = jnp.minimum(t_pos + 1, win).astype(F32)
    pooled = win_sum / count - cur
    o_ref[...] = (_dot(pooled.astype(BF16), w_ref[0]) * sc_ref[...]).astype(o_ref.dtype)


def multiscale_pool(p_aux, pool_w, pool_scale):
    s_len = p_aux.shape[0]
    tr, gw = POOL_TR, POOL_G
    ng = len(POOL_WINDOWS)
    return pl.pallas_call(
        _pool_kernel,
        grid=(s_len // tr, ng),
        in_specs=[
            pl.BlockSpec((tr, gw), lambda i, g: (i, g)),
            pl.BlockSpec((HALO, gw), lambda i, g: (jnp.maximum(i * (tr // HALO) - 1, 0), g)),
            pl.BlockSpec((1, gw, gw), lambda i, g: (g, 0, 0)),
            pl.BlockSpec((1, gw), lambda i, g: (0, g)),
        ],
        out_specs=pl.BlockSpec((tr, gw), lambda i, g: (i, g)),
        out_shape=jax.ShapeDtypeStruct((s_len, ng * gw), BF16),
        compiler_params=_params("parallel", "parallel"),
        name="multiscale_pool",
    )(p_aux, p_aux, pool_w, pool_scale.reshape(1, ng * gw))


def _split3(x):
    hi = x.astype(BF16)
    r1 = x - hi.astype(F32)
    mid = r1.astype(BF16)
    lo = (r1 - mid.astype(F32)).astype(BF16)
    return hi, mid, lo


def _dot_exact_lhs(a_bf16, x):
    hi, mid, lo = _split3(x)
    return _dot(a_bf16, lo) + _dot(a_bf16, mid) + _dot(a_bf16, hi)


def _dot_exact_rhs(x, b_bf16):
    hi, mid, lo = _split3(x)
    return _dot(lo, b_bf16) + _dot(mid, b_bf16) + _dot(hi, b_bf16)


def _softplus(x):
    return jnp.maximum(x, 0.0) + jnp.log(1.0 + jnp.exp(-jnp.abs(x)))


PREP_CHUNKS = 2


def _gdn_prep_kernel(x_ref, prev_ref, cw_ref, ba_ref, alog_ref, dtb_ref,
                     u_ref, w_ref, qd_ref, kdt_ref, intra_ref, dl_ref, ext_ref, *, n_vheads):
    c = CHUNK
    dk = GDN_DK
    assert 2 * c == 128 and dk == 128 and GDN_DV == 128
    scale = dk ** -0.5
    n_kheads = n_vheads // 2
    step = pl.program_id(0)
    for sb in range(2 * n_kheads + n_vheads):
        cols = slice(sb * 128, (sb + 1) * 128)
        ext_ref[sb, :HALO, :] = jnp.where(step > 0, prev_ref[:, cols].astype(F32), 0.0)
        ext_ref[sb, HALO:, :] = x_ref[:, cols].astype(F32)
    cw = cw_ref[...]

    def conv_head(sb, ci, out_scale):
        cols = slice(sb * 128, (sb + 1) * 128)
        base = HALO + ci * c
        y = ext_ref[sb, base:base + c, :] * cw[CONV_K - 1:CONV_K, cols]
        for tap in range(CONV_K - 1):
            lo = base - (CONV_K - 1 - tap)
            y = y + ext_ref[sb, lo:lo + c, :] * cw[tap:tap + 1, cols]
        y = y / (1.0 + jnp.exp2(y * (-LOG2E)))
        if out_scale is not None:
            y = y * (lax.rsqrt(jnp.sum(y * y, axis=-1, keepdims=True) + RMS_EPS) * out_scale)
        return y

    row = lax.broadcasted_iota(I32, (c, c), 0)
    col = lax.broadcasted_iota(I32, (c, c), 1)
    lower = jnp.where(row >= col, 1.0, 0.0).astype(BF16)
    upper = jnp.where(row <= col, 1.0, 0.0).astype(BF16)
    row_w = lax.broadcasted_iota(I32, (c, 2 * c), 0)
    lane_w = lax.broadcasted_iota(I32, (c, 2 * c), 1)
    first = lane_w < c
    col_w = jnp.where(first, lane_w, lane_w - c)
    tril_w = row_w >= col_w
    strict_w = row_w > col_w
    eye_w = jnp.where(row_w == col_w, 1.0, 0.0).astype(F32)

    def block_diag(wide16):
        zero = jnp.zeros_like(wide16)
        return jnp.concatenate([jnp.where(first, wide16, zero), jnp.where(first, zero, wide16)], axis=0)

    pairs = range(n_vheads // 2)
    pws, t_invs, rhss, where = [], [], [], []
    for ci in range(PREP_CHUNKS):
        rows = slice(ci * c, (ci + 1) * c)
        ba = ba_ref[rows, :]
        beta_all = jax.nn.sigmoid(ba)
        g_all = -jnp.exp(alog_ref[...]) * _softplus(ba + dtb_ref[...])
        gc_all = _dot_exact_lhs(lower, g_all)
        gct_all = _dot_exact_rhs(g_all.T, upper)
        for p in pairs:
            kh = conv_head(n_kheads + p, ci, 1.0)
            qh = conv_head(p, ci, scale)
            kh_t = kh.T
            kh16 = kh.astype(BF16)
            kk16 = jnp.concatenate([kh16, kh16], axis=0)
            both = _dot_nt(jnp.concatenate([kh16, qh.astype(BF16)], axis=0), kk16)
            gram, qk = both[:c], both[c:]
            hv = (2 * p, 2 * p + 1)
            beta = [jnp.broadcast_to(beta_all[:, h:h + 1], (c, 2 * c)) for h in hv]
            gcol = [jnp.broadcast_to(gc_all[:, n_vheads + h:n_vheads + h + 1], (c, 2 * c)) for h in hv]
            grow = [gct_all[n_vheads + h:n_vheads + h + 1, :] for h in hv]
            grow_w = jnp.concatenate(grow, axis=1)
            gcol_w = jnp.where(first, gcol[0], gcol[1])
            decay_w = jnp.where(tril_w, jnp.exp(jnp.where(tril_w, gcol_w - grow_w, 0.0)), 0.0)
            a_w = jnp.where(strict_w, -(jnp.where(first, beta[0], beta[1]) * gram * decay_w), 0.0)
            intra_ref[ci, p] = jnp.where(tril_w, qk * decay_w, 0.0).astype(BF16)
            rhs = []
            for i, h in enumerate(hv):
                eg = jnp.exp(gcol[i])
                g_last = grow[i][:, c - 1:c]
                sl = slice(h * GDN_DV, (h + 1) * GDN_DV)
                qd_ref[rows, sl] = (qh * eg).astype(BF16)
                kdt_ref[ci, h] = (kh_t * jnp.exp(g_last - grow[i])).astype(BF16)
                dl_ref[ci, h] = jnp.broadcast_to(jnp.exp(g_last), (1, 128))
                vh = conv_head(2 * n_kheads + h, ci, None)
                rhs.append(jnp.concatenate([vh * beta[i], kh * (beta[i] * eg)], axis=-1).astype(BF16))
            pws.append(a_w)
            t_invs.append(eye_w + a_w)
            rhss.append(jnp.concatenate(rhs, axis=0))
            where.append((rows, p))
    n_levels = int(math.log2(c)) - 1
    pws = [_dot(a.astype(BF16), block_diag(a.astype(BF16))) for a in pws]
    for level in range(n_levels):
        new_t, new_pw = [], []
        for t, pw in zip(t_invs, pws):
            pw16 = pw.astype(BF16)
            if level == n_levels - 1:
                new_t.append(t + _dot(t.astype(BF16), block_diag(pw16)))
            else:
                both = _dot(jnp.concatenate([t.astype(BF16), pw16], axis=0), block_diag(pw16))
                new_t.append(t + both[:c])
                new_pw.append(both[c:])
        t_invs, pws = new_t, new_pw
    for (rows, p), t_inv, rhs in zip(where, t_invs, rhss):
        t16 = t_inv.astype(BF16)
        zero = jnp.zeros_like(t16)
        lhs = jnp.concatenate([jnp.where(first, t16, zero), jnp.where(first, zero, t16)], axis=0)
        sol = _dot(lhs, rhs)
        for i, h in enumerate((2 * p, 2 * p + 1)):
            sl = slice(h * GDN_DV, (h + 1) * GDN_DV)
            u_ref[rows, sl] = sol[i * c:(i + 1) * c, :GDN_DV]
            w_ref[rows, sl] = sol[i * c:(i + 1) * c, GDN_DV:].astype(BF16)


def gdn_prepare(p_main, conv_w, ba, a_log, dt_bias, n_kheads, n_vheads):
    s_len = p_main.shape[0]
    nc = PREP_CHUNKS
    c = CHUNK * nc
    n = s_len // c
    v_w = n_vheads * GDN_DV
    conv_width = 2 * n_kheads * GDN_DK + v_w
    assert 2 * n_vheads <= 128 and n_vheads == 2 * n_kheads and c % HALO == 0
    pad = jnp.zeros((128 - 2 * n_vheads,), F32)
    alog_row = jnp.concatenate([jnp.zeros((n_vheads,), F32), a_log, pad]).reshape(1, 128)
    dtb_row = jnp.concatenate([jnp.zeros((n_vheads,), F32), dt_bias, pad]).reshape(1, 128)
    return pl.pallas_call(
        functools.partial(_gdn_prep_kernel, n_vheads=n_vheads),
        grid=(n,),
        in_specs=[
            pl.BlockSpec((c, conv_width), lambda i: (i, 0)),
            pl.BlockSpec((HALO, conv_width), lambda i: (jnp.maximum(i * (c // HALO) - 1, 0), 0)),
            pl.BlockSpec((CONV_K, conv_width), lambda i: (0, 0)),
            pl.BlockSpec((c, 128), lambda i: (i, 0)),
            pl.BlockSpec((1, 128), lambda i: (0, 0)),
            pl.BlockSpec((1, 128), lambda i: (0, 0)),
        ],
        out_specs=[
            pl.BlockSpec((c, v_w), lambda i: (i, 0)),
            pl.BlockSpec((c, v_w), lambda i: (i, 0)),
            pl.BlockSpec((c, v_w), lambda i: (i, 0)),
            pl.BlockSpec((nc, n_vheads, GDN_DK, CHUNK), lambda i: (i, 0, 0, 0)),
            pl.BlockSpec((nc, n_vheads // 2, CHUNK, 2 * CHUNK), lambda i: (i, 0, 0, 0)),
            pl.BlockSpec((nc, n_vheads, 1, 128), lambda i: (i, 0, 0, 0)),
        ],
        out_shape=[
            jax.ShapeDtypeStruct((s_len, v_w), F32),
            jax.ShapeDtypeStruct((s_len, v_w), BF16),
            jax.ShapeDtypeStruct((s_len, v_w), BF16),
            jax.ShapeDtypeStruct((n * nc, n_vheads, GDN_DK, CHUNK), BF16),
            jax.ShapeDtypeStruct((n * nc, n_vheads // 2, CHUNK, 2 * CHUNK), BF16),
            jax.ShapeDtypeStruct((n * nc, n_vheads, 1, 128), F32),
        ],
        scratch_shapes=[pltpu.VMEM((conv_width // 128, HALO + c, 128), F32)],
        compiler_params=_params("parallel"),
        name="gdn_prepare",
    )(p_main, p_main, conv_w, ba, alog_row, dtb_row)


SCAN_HG = 16
SCAN_CHUNKS = 8


def _gdn_scan_kernel(u_ref, w_ref, qd_ref, kdt_ref, intra_ref, dl_ref, z_ref, nw_ref, o_ref, state_ref):
    @pl.when(pl.program_id(1) == 0)
    def _():
        state_ref[...] = jnp.zeros_like(state_ref)

    nw = nw_ref[...]
    heads = range(SCAN_HG)
    sls = [slice(j * GDN_DV, (j + 1) * GDN_DV) for j in heads]
    first = lax.broadcasted_iota(I32, (CHUNK, 2 * CHUNK), 1) < CHUNK
    states = [state_ref[j] for j in heads]
    for ci in range(SCAN_CHUNKS):
        rows = slice(ci * CHUNK, (ci + 1) * CHUNK)
        s16 = [s.astype(BF16) for s in states]
        ws = [_dot(w_ref[rows, sls[j]], s16[j]) for j in heads]
        qs = [_dot(qd_ref[rows, sls[j]], s16[j]) for j in heads]
        v16 = [(u_ref[rows, sls[j]] - ws[j]).astype(BF16) for j in heads]
        upd = [_dot(kdt_ref[ci, j], v16[j]) for j in heads]
        outs = []
        for p in range(SCAN_HG // 2):
            pair = intra_ref[ci, p]
            zero = jnp.zeros_like(pair)
            v_pair = jnp.concatenate([v16[2 * p], v16[2 * p + 1]], axis=0)
            outs.append(qs[2 * p] + _dot(jnp.where(first, pair, zero), v_pair))
            outs.append(qs[2 * p + 1] + _dot(jnp.where(first, zero, pair), v_pair))
        states = [states[j] * dl_ref[ci, j] + upd[j] for j in heads]
        for j in heads:
            z = z_ref[rows, sls[j]].astype(F32)
            o_ref[rows, sls[j]] = (_rms(outs[j], nw) * (z * jax.nn.sigmoid(z))).astype(o_ref.dtype)
    for j in heads:
        state_ref[j] = states[j]


def gdn_scan(u, w, qd, kdt, intra, dl, p_main, norm_w, z_col_block):
    s_len, v_w = u.shape
    nc = SCAN_CHUNKS
    c = CHUNK * nc
    n = s_len // c
    hg = SCAN_HG
    gw = hg * GDN_DV
    return pl.pallas_call(
        _gdn_scan_kernel,
        grid=(v_w // gw, n),
        in_specs=[
            pl.BlockSpec((c, gw), lambda g, i: (i, g)),
            pl.BlockSpec((c, gw), lambda g, i: (i, g)),
            pl.BlockSpec((c, gw), lambda g, i: (i, g)),
            pl.BlockSpec((nc, hg, GDN_DK, CHUNK), lambda g, i: (i, g, 0, 0)),
            pl.BlockSpec((nc, hg // 2, CHUNK, 2 * CHUNK), lambda g, i: (i, g, 0, 0)),
            pl.BlockSpec((nc, hg, 1, 128), lambda g, i: (i, g, 0, 0)),
            pl.BlockSpec((c, gw), lambda g, i: (i, z_col_block + g)),
            pl.BlockSpec((1, GDN_DV), lambda g, i: (0, 0)),
        ],
        out_specs=pl.BlockSpec((c, gw), lambda g, i: (i, g)),
        out_shape=jax.ShapeDtypeStruct((s_len, v_w), BF16),
        scratch_shapes=[pltpu.VMEM((hg, GDN_DK, GDN_DV), F32)],
        compiler_params=_params("parallel", "arbitrary"),
        name="gdn_scan",
    )(u, w, qd, kdt, intra, dl, p_main, norm_w.reshape(1, GDN_DV))


PROJ_TM = 1024
PROJ_TN = 1024
FFN_TM = 512
FFN_TF = 512
OUT_TM = 512
OUT_TK_MAX = 2048


def _out_tk(k):
    return max(c for c in range(128, OUT_TK_MAX + 1, 128) if k % c == 0)


def _even_main_width(d):
    return 3 * (d // 2) + N_IDX_HEADS * IDX_HEAD_DIM


def _odd_main_width(d):
    n_kheads = d // 128
    return 2 * n_kheads * GDN_DK + 2 * (2 * n_kheads) * GDN_DV


def _even_mixer(x, g_pre, g_post, w_in16t, w_rest, w_o16, layer, pool_w, pool_scale, bias_tables):
    s_len, d = x.shape
    attn_w = d // 2
    n_heads = attn_w // ATTN_HEAD_DIM
    idx_w = N_IDX_HEADS * IDX_HEAD_DIM
    pool_width = len(POOL_WINDOWS) * POOL_G
    main_w = _even_main_width(d)
    assert main_w == 3 * attn_w + idx_w == 4096 and pool_width == 1024
    small = IDX_HEAD_DIM + N_IDX_HEADS
    assert w_rest.shape[0] == small + pool_width
    w_aux = jnp.concatenate([w_rest[small:], w_rest[:small], jnp.zeros((128 - small, d), BF16)], axis=0)
    p_main = norm_matmul(x, g_pre, w_in16t, layer, main_w, BF16, PROJ_TM, PROJ_TN)
    p_aux = norm_matmul(x, g_pre, w_aux[None], 0, w_aux.shape[0], F32, PROJ_TM, w_aux.shape[0])
    k_idx_t = p_aux[:, pool_width:pool_width + IDX_HEAD_DIM].astype(BF16).T
    kt2 = jnp.concatenate([k_idx_t, k_idx_t], axis=0)
    mask = indexer_mask(p_main, p_aux, kt2, min(TOPK_MAX, s_len // 4))
    attn = masked_attention(p_main, mask, bias_tables, n_heads)
    pooled = multiscale_pool(p_aux, pool_w.astype(BF16), pool_scale)
    return mm_norm_res([attn, pooled], w_o16, layer, g_post, x, OUT_TM, _out_tk(attn.shape[1]))


def _odd_mixer(x, g_pre, g_post, w_in16t, w_rest, w_o16, layer, conv_w, a_log, dt_bias, norm_w):
    s_len, d = x.shape
    n_kheads = d // 128
    n_vheads = 2 * n_kheads
    qk_w = n_kheads * GDN_DK
    v_w = n_vheads * GDN_DV
    conv_width = 2 * qk_w + v_w
    main_w = _odd_main_width(d)
    assert main_w == conv_width + v_w
    w_ba = jnp.concatenate([w_rest, jnp.zeros((128 - 2 * n_vheads, d), BF16)], axis=0)
    p_main = norm_matmul(x, g_pre, w_in16t, layer, main_w, BF16, PROJ_TM, PROJ_TN)
    ba = norm_matmul(x, g_pre, w_ba[None], 0, 128, F32, PROJ_TM, 128)
    u, w, qd, kdt, intra, dl = gdn_prepare(p_main, conv_w, ba, a_log, dt_bias, n_kheads, n_vheads)
    o = gdn_scan(u, w, qd, kdt, intra, dl, p_main, norm_w, conv_width // (SCAN_HG * GDN_DV))
    return mm_norm_res([o], w_o16, layer, g_post, x, OUT_TM, _out_tk(v_w))


def _ffn(x, g_pre, g_post, w_gate16, w_up16, w_down16, layer):
    return ffn(x, g_pre, g_post, w_gate16, w_up16, w_down16, layer, FFN_TM, FFN_TF)


def kernel(x, rel_bias, even_w_in, even_w_o, pool_w, pool_scale, odd_w_in, conv_w, a_log, dt_bias,
           gdn_norm_w, odd_w_o, ffn_w_gate, ffn_w_up, ffn_w_down, mix_pre_g, mix_post_g, ffn_pre_g, ffn_post_g):
    b, s_len, d = x.shape
    depth = ffn_w_gate.shape[0]
    bias_tables = _bias_tables(rel_bias)
    em, om = _even_main_width(d), _odd_main_width(d)
    even_t, odd_t = jnp.swapaxes(even_w_in, 1, 2), jnp.swapaxes(odd_w_in, 1, 2)
    even_in16 = cast_bf16(even_t)
    even_rest16 = even_in16[:, em:]
    odd_in16, odd_rest16 = cast_bf16(odd_t, 0, om), cast_bf16(odd_t, om)
    even_o16, odd_o16 = cast_bf16(even_w_o), cast_bf16(odd_w_o)
    gate16, up16, down16 = cast_bf16(ffn_w_gate), cast_bf16(ffn_w_up), cast_bf16(ffn_w_down)
    xs = x.reshape(b * s_len, d)
    outs = []
    for bi in range(b):
        h = xs[bi * s_len:(bi + 1) * s_len]
        for layer in range(depth):
            i = layer // 2
            if layer % 2 == 0:
                h = _even_mixer(h, mix_pre_g[layer], mix_post_g[layer], even_in16, even_rest16[i], even_o16, i,
                                pool_w[i], pool_scale[i], bias_tables)
            else:
                h = _odd_mixer(h, mix_pre_g[layer], mix_post_g[layer], odd_in16, odd_rest16[i], odd_o16, i,
                               conv_w[i], a_log[i], dt_bias[i], gdn_norm_w[i])
            h = _ffn(h, ffn_pre_g[layer], ffn_post_g[layer], gate16, up16, down16, layer)
        outs.append(h)
    return jnp.concatenate(outs, axis=0).reshape(b, s_len, d)
```

```python
import functools
import math

import jax
import jax.numpy as jnp
from jax import lax
from jax.experimental import pallas as pl
from jax.experimental.pallas import tpu as pltpu

F32 = jnp.float32
BF16 = jnp.bfloat16
I32 = jnp.int32

RMS_EPS = 1e-6
NEG_INF = -1e30
INT_MIN = -2 ** 31
LOG2E = 1.4426950408889634

VMEM_LIMIT_BYTES = 56 * 1024 * 1024

ATTN_HEAD_DIM = 128
N_IDX_HEADS = 16
IDX_HEAD_DIM = 64
TOPK_MAX = 256
N_BUCKETS = 32
MAX_DISTANCE = 128
POOL_WINDOWS = (2, 4, 8, 16)
GDN_DK = 128
GDN_DV = 128
CONV_K = 4
CHUNK = 64


def _params(*sem):
    return pltpu.CompilerParams(dimension_semantics=sem, vmem_limit_bytes=VMEM_LIMIT_BYTES)


def _dot(a, b):
    return jnp.dot(a, b, preferred_element_type=F32)


def _dot_nt(a, b):
    return lax.dot_general(a, b, (((1,), (1,)), ((), ())), preferred_element_type=F32)


def _to_u32(x):
    return lax.bitcast_convert_type(x ^ INT_MIN, jnp.uint32)


def _to_i32(x):
    return lax.bitcast_convert_type(x, I32)


def _rms(x, g):
    ms = jnp.mean(x * x, axis=-1, keepdims=True)
    return x * lax.rsqrt(ms + RMS_EPS) * g


def _norm_matmul_kernel(x_ref, g_ref, wt_ref, o_ref, hn_ref):
    @pl.when(pl.program_id(1) == 0)
    def _():
        hn_ref[...] = _rms(x_ref[...], g_ref[...]).astype(BF16)

    o_ref[...] = _dot_nt(hn_ref[...], wt_ref[...]).astype(o_ref.dtype)


def norm_matmul(x, g, wt, layer, n, out_dtype, tm, tn):
    m, k = x.shape
    assert m % tm == 0 and n % tn == 0 and n <= wt.shape[1] and wt.shape[2] == k
    return pl.pallas_call(
        _norm_matmul_kernel,
        grid=(m // tm, n // tn),
        in_specs=[
            pl.BlockSpec((tm, k), lambda i, j: (i, 0)),
            pl.BlockSpec((1, k), lambda i, j: (0, 0)),
            pl.BlockSpec((None, tn, k), lambda i, j: (layer, j, 0)),
        ],
        out_specs=pl.BlockSpec((tm, tn), lambda i, j: (i, j)),
        out_shape=jax.ShapeDtypeStruct((m, n), out_dtype),
        scratch_shapes=[pltpu.VMEM((tm, k), BF16)],
        compiler_params=_params("parallel", "arbitrary"),
        name="norm_matmul",
    )(x, g.reshape(1, k), wt)


def _ffn_kernel(x_ref, g_ref, wg_ref, wu_ref, wd_ref, gp_ref, o_ref, hn_ref, acc_ref, *, nj):
    j = pl.program_id(1)

    @pl.when(j == 0)
    def _():
        hn_ref[...] = _rms(x_ref[...], g_ref[...]).astype(BF16)
        acc_ref[...] = jnp.zeros_like(acc_ref)

    hn = hn_ref[...]
    a = _dot(hn, wg_ref[...])
    b = _dot(hn, wu_ref[...])
    act = (a * jax.nn.sigmoid(a) * b).astype(BF16)
    acc_ref[...] += _dot(act, wd_ref[...])

    @pl.when(j == nj - 1)
    def _():
        o_ref[...] = x_ref[...] + _rms(acc_ref[...], gp_ref[...])


def ffn(x, g_pre, g_post, wg, wu, wd, layer, tm, tf):
    m, d = x.shape
    ff = wg.shape[2]
    assert m % tm == 0 and ff % tf == 0
    nj = ff // tf
    return pl.pallas_call(
        functools.partial(_ffn_kernel, nj=nj),
        grid=(m // tm, nj),
        in_specs=[
            pl.BlockSpec((tm, d), lambda i, j: (i, 0)),
            pl.BlockSpec((1, d), lambda i, j: (0, 0)),
            pl.BlockSpec((None, d, tf), lambda i, j: (layer, 0, j)),
            pl.BlockSpec((None, d, tf), lambda i, j: (layer, 0, j)),
            pl.BlockSpec((None, tf, d), lambda i, j: (layer, j, 0)),
            pl.BlockSpec((1, d), lambda i, j: (0, 0)),
        ],
        out_specs=pl.BlockSpec((tm, d), lambda i, j: (i, 0)),
        out_shape=jax.ShapeDtypeStruct((m, d), F32),
        scratch_shapes=[pltpu.VMEM((tm, d), BF16), pltpu.VMEM((tm, d), F32)],
        compiler_params=_params("parallel", "arbitrary"),
        name="ffn",
    )(x, g_pre.reshape(1, d), wg, wu, wd, g_post.reshape(1, d))


def _mm_norm_res_kernel(*refs, n_seg, nk):
    a_refs = refs[:n_seg]
    w_refs = refs[n_seg:2 * n_seg]
    g_ref, r_ref, o_ref, acc_ref = refs[2 * n_seg:]
    kk = pl.program_id(1)

    @pl.when(kk == 0)
    def _():
        acc_ref[...] = jnp.zeros_like(acc_ref)

    part = _dot(a_refs[0][...], w_refs[0][...])
    for s in range(1, n_seg):
        part = part + _dot(a_refs[s][...], w_refs[s][...])
    acc_ref[...] += part

    @pl.when(kk == nk - 1)
    def _():
        o_ref[...] = r_ref[...] + _rms(acc_ref[...], g_ref[...])


def mm_norm_res(a_list, w, layer, g, res, tm, tk):
    n_seg = len(a_list)
    m, ka = a_list[0].shape
    n = w.shape[2]
    assert w.shape[1] == n_seg * ka and m % tm == 0 and ka % tk == 0
    nk = ka // tk
    a_specs = [pl.BlockSpec((tm, tk), lambda i, k: (i, k)) for _ in range(n_seg)]
    w_specs = [pl.BlockSpec((None, tk, n), functools.partial(lambda i, k, s: (layer, k + s * nk, 0), s=s))
               for s in range(n_seg)]
    return pl.pallas_call(
        functools.partial(_mm_norm_res_kernel, n_seg=n_seg, nk=nk),
        grid=(m // tm, nk),
        in_specs=a_specs + w_specs + [
            pl.BlockSpec((1, n), lambda i, k: (0, 0)),
            pl.BlockSpec((tm, n), lambda i, k: (i, 0)),
        ],
        out_specs=pl.BlockSpec((tm, n), lambda i, k: (i, 0)),
        out_shape=jax.ShapeDtypeStruct((m, n), F32),
        scratch_shapes=[pltpu.VMEM((tm, n), F32)],
        compiler_params=_params("parallel", "arbitrary"),
        name="mm_norm_res",
    )(*a_list, *([w] * n_seg), g.reshape(1, n), res)


CAST_BLOCK_BYTES = 4 * 1024 * 1024


def _cast_kernel(x_ref, o_ref):
    o_ref[...] = x_ref[...].astype(o_ref.dtype)


def cast_bf16(w, row0=0, height=None):
    layers, n, k = w.shape
    height = n - row0 if height is None else height
    assert row0 + height <= n
    tr = max(r for r in range(16, height + 1, 16)
             if height % r == 0 and row0 % r == 0 and r * k * 4 <= CAST_BLOCK_BYTES)
    return pl.pallas_call(
        _cast_kernel,
        grid=(layers, height // tr),
        in_specs=[pl.BlockSpec((None, tr, k), lambda l, i: (l, row0 // tr + i, 0))],
        out_specs=pl.BlockSpec((None, tr, k), lambda l, i: (l, i, 0)),
        out_shape=jax.ShapeDtypeStruct((layers, height, k), BF16),
        compiler_params=_params("parallel", "parallel"),
        name="cast_bf16",
    )(w)


IDX_TQ = 128
IDX_TK = 512
IDX_ROWS = 128
IDX_STEPS_PER_CHECK = 4


def _indexer_kernel(qi_ref, aux_ref, kt_ref, mask_ref, key_ref, qh_ref, wb_ref, *, topk, w_scale):
    tq, tk, tr = IDX_TQ, IDX_TK, IDX_ROWS
    nj = tk // 128
    s_len = mask_ref.shape[1]
    qb = pl.program_id(0)
    n_chunks = ((qb + 1) * tq - 1) // tk + 1

    lane = lax.broadcasted_iota(I32, (tq, 128), 1)
    aux = aux_ref[...]
    for h in range(N_IDX_HEADS):
        pair = qi_ref[:, (h // 2) * 128:(h // 2 + 1) * 128]
        keep = (lane < IDX_HEAD_DIM) if h % 2 == 0 else (lane >= IDX_HEAD_DIM)
        qh_ref[h] = jnp.where(keep, pair, jnp.zeros_like(pair))
        wcol = aux[:, IDX_HEAD_DIM + h:IDX_HEAD_DIM + h + 1] * w_scale
        wb_ref[h] = jnp.broadcast_to(wcol, (tq, 128))

    lane_r = lax.broadcasted_iota(I32, (tr, 128), 1)
    row_r = lax.broadcasted_iota(I32, (tr, 128), 0)

    def score_chunk(c, tops):
        col0 = pl.multiple_of(c * tk, tk)
        kt = kt_ref[:, pl.ds(col0, tk)]
        new_tops = []
        for r in range(tq // tr):
            rows = slice(r * tr, (r + 1) * tr)
            accs = [jnp.zeros((tr, 128), F32) for _ in range(nj)]
            for h in range(N_IDX_HEADS):
                d = _dot(qh_ref[h, rows], kt)
                wb = wb_ref[h, rows]
                for j in range(nj):
                    accs[j] = accs[j] + wb * jnp.maximum(d[:, j * 128:(j + 1) * 128], 0.0)
            q_pos = qb * tq + r * tr + row_r
            m1, m2 = tops[r]
            for j in range(nj):
                bits = pltpu.bitcast(accs[j], I32)
                skey = bits ^ ((bits >> 31) & 0x7FFFFFFF)
                skey = jnp.where(col0 + j * 128 + lane_r <= q_pos, skey, INT_MIN)
                key_ref[rows, pl.ds(pl.multiple_of(col0 + j * 128, 128), 128)] = skey
                m2 = jnp.maximum(m2, jnp.minimum(m1, skey))
                m1 = jnp.maximum(m1, skey)
            new_tops.append((m1, m2))
        return tuple(new_tops)

    lowest = jnp.full((tr, 128), INT_MIN, I32)
    tops = lax.fori_loop(0, n_chunks, score_chunk, tuple((lowest, lowest) for _ in range(tq // tr)))
    m1 = jnp.concatenate([t[0] for t in tops], axis=0)
    m2 = jnp.concatenate([t[1] for t in tops], axis=0)
    assert topk <= 2 * 128 and 127 + s_len // 128 < topk
    top = jnp.max(m2, axis=-1, keepdims=True)
    n_top = jnp.sum(jnp.where(m2 == top, 1, 0), axis=-1, keepdims=True)
    below = jnp.max(jnp.where(m2 == top, INT_MIN, m2), axis=-1, keepdims=True)
    hi = _to_u32(jnp.where(n_top >= 2, top, below))
    lo = _to_u32(jnp.min(m2, axis=-1, keepdims=True))
    n_iter = 32 - jnp.max(lax.clz(_to_i32(hi - lo)))

    def count_ge(cand):
        cand_b = jnp.broadcast_to(cand, (tq, 128))

        def body(c, cnt):
            blk = key_ref[:, pl.ds(pl.multiple_of(c * tk, tk), tk)]
            for j in range(nj):
                cnt = cnt + jnp.where(blk[:, j * 128:(j + 1) * 128] >= cand_b, 1, 0)
            return cnt

        cnt = lax.fori_loop(0, n_chunks, body, jnp.zeros((tq, 128), I32))
        return jnp.sum(cnt, axis=-1, keepdims=True)

    def not_settled(state):
        i, _, _, _, settled = state
        return jnp.logical_and(i < n_iter, settled == 0)

    def bisect(state):
        i, lo, hi, n_lo, _ = state
        for _ in range(IDX_STEPS_PER_CHECK):
            width = hi - lo
            mid = lo + (width >> 1) + (width & 1)
            n_mid = count_ge(_to_i32(mid) ^ INT_MIN)
            ok = n_mid >= topk
            lo = jnp.where(ok, mid, lo)
            hi = jnp.where(ok, hi, jnp.where(width > 0, mid - 1, hi))
            n_lo = jnp.where(ok, n_mid, n_lo)
        settled = jnp.min(jnp.where(jnp.logical_or(n_lo == topk, hi == lo), 1, 0))
        return i + IDX_STEPS_PER_CHECK, lo, hi, n_lo, settled

    unknown = jnp.full((tq, 1), s_len + 1, I32)
    _, tau_u, _, _, _ = lax.while_loop(not_settled, bisect, (jnp.int32(0), lo, hi, unknown, jnp.int32(0)))
    tau_u = _to_i32(tau_u)
    tau = jnp.broadcast_to(jnp.maximum(tau_u ^ INT_MIN, INT_MIN + 1), (tq, 128))

    neg = jnp.full((tq, tk), NEG_INF, F32).astype(BF16)

    def write_chunk(c, carry):
        col0 = pl.multiple_of(c * tk, tk)

        @pl.when(c < n_chunks)
        def _():
            blk = key_ref[:, pl.ds(col0, tk)]
            parts = [jnp.where(blk[:, j * 128:(j + 1) * 128] >= tau, 0.0, NEG_INF) for j in range(nj)]
            mask_ref[:, pl.ds(col0, tk)] = jnp.concatenate(parts, axis=-1).astype(BF16)

        @pl.when(c >= n_chunks)
        def _():
            mask_ref[:, pl.ds(col0, tk)] = neg

        return carry

    lax.fori_loop(0, s_len // tk, write_chunk, 0)


def indexer_mask(p_main, p_aux, kt2, topk):
    s_len = p_main.shape[0]
    tq = IDX_TQ
    w_scale = (N_IDX_HEADS ** -0.5) * (IDX_HEAD_DIM ** -0.5)
    return pl.pallas_call(
        functools.partial(_indexer_kernel, topk=topk, w_scale=w_scale),
        grid=(s_len // tq,),
        in_specs=[
            pl.BlockSpec((tq, 1024), lambda i: (i, 3)),
            pl.BlockSpec((tq, 128), lambda i: (i, 8)),
            pl.BlockSpec((128, s_len), lambda i: (0, 0)),
        ],
        out_specs=pl.BlockSpec((tq, s_len), lambda i: (i, 0)),
        out_shape=jax.ShapeDtypeStruct((s_len, s_len), BF16),
        scratch_shapes=[
            pltpu.VMEM((tq, s_len), I32),
            pltpu.VMEM((N_IDX_HEADS, tq, 128), BF16),
            pltpu.VMEM((N_IDX_HEADS, tq, 128), F32),
        ],
        compiler_params=_params("parallel"),
        name="indexer_mask",
    )(p_main, p_aux, kt2)


ATT_T = 512
ATT_RC = 32
ATT_SUB = 128


def _attention_kernel(qb_tab, kb_tab, q_ref, k_ref, v_ref, mask_ref, tz_ref, o_ref,
                      qs_ref, mf_ref, p_ref, m_ref, l_ref, acc_ref, *, n_heads, scale):
    step = pl.program_id(0)
    qb = qb_tab[step]
    kb = kb_tab[step]
    dh, t, rc, sub = ATTN_HEAD_DIM, ATT_T, ATT_RC, ATT_SUB
    nsub = t // sub

    @pl.when(kb == 0)
    def _():
        qs_ref[...] = (q_ref[...].astype(F32) * (scale * LOG2E)).astype(BF16)
        m_ref[...] = jnp.full_like(m_ref, NEG_INF)
        l_ref[...] = jnp.zeros_like(l_ref)
        acc_ref[...] = jnp.zeros_like(acc_ref)

    mf_ref[...] = mask_ref[...].astype(F32)
    prev_flag = jnp.where(kb == qb - 1, 1.0, 0.0).astype(F32)

    def run(diag):
        sls = [slice(h * dh, (h + 1) * dh) for h in range(n_heads)]

        def softmax(h, s):
            alphas = []
            for r in range(t // rc):
                rows = slice(r * rc, (r + 1) * rc)
                blk, r0 = divmod(r * rc, sub)
                sr = s[rows] + mf_ref[rows, :]
                parts = [sr[:, b * sub:(b + 1) * sub] for b in range(nsub)]
                if diag:
                    parts[blk] = parts[blk] + tz_ref[0, h, r0:r0 + rc, :]
                    if blk >= 1:
                        parts[blk - 1] = parts[blk - 1] + tz_ref[1, h, r0:r0 + rc, :]
                elif blk == 0:
                    parts[nsub - 1] = parts[nsub - 1] + prev_flag * tz_ref[1, h, r0:r0 + rc, :]
                if diag:
                    parts = parts[:blk + 1]
                m_prev = m_ref[h, rows]
                m_new = jnp.maximum(m_prev, jnp.max(functools.reduce(jnp.maximum, parts), axis=-1, keepdims=True))
                alpha = jnp.exp2(m_prev - m_new)
                ps = [jnp.exp2(pb - m_new) for pb in parts]
                l_ref[h, rows] = alpha * l_ref[h, rows] + jnp.sum(functools.reduce(jnp.add, ps), axis=-1, keepdims=True)
                m_ref[h, rows] = m_new
                ps = ps + [jnp.zeros_like(ps[0])] * (nsub - len(ps))
                p_ref[h, rows, :] = jnp.concatenate(ps, axis=-1).astype(BF16)
                alphas.append(alpha)
            return alphas

        def weighted_values(h, alphas):
            pv = _dot(p_ref[h], v_ref[:, sls[h]])
            for r in range(t // rc):
                rows = slice(r * rc, (r + 1) * rc)
                acc_ref[h, rows] = alphas[r] * acc_ref[h, rows] + pv[rows]

        s_next = _dot_nt(qs_ref[:, sls[0]], k_ref[:, sls[0]])
        pending = None
        for h in range(n_heads):
            s = s_next
            if h + 1 < n_heads:
                s_next = _dot_nt(qs_ref[:, sls[h + 1]], k_ref[:, sls[h + 1]])
            alphas = softmax(h, s)
            if pending is not None:
                weighted_values(*pending)
            pending = (h, alphas)
        weighted_values(*pending)

    @pl.when(kb != qb)
    def _():
        run(False)

    @pl.when(kb == qb)
    def _():
        run(True)
        for h in range(n_heads):
            o_ref[:, h * dh:(h + 1) * dh] = (acc_ref[h] / l_ref[h]).astype(o_ref.dtype)


def masked_attention(p_main, mask, bias_tables, n_heads):
    s_len = p_main.shape[0]
    t = ATT_T
    nb = s_len // t
    pairs = [(q, k) for q in range(nb) for k in range(q + 1)]
    qb_tab = jnp.array([p[0] for p in pairs], I32)
    kb_tab = jnp.array([p[1] for p in pairs], I32)
    width = n_heads * ATTN_HEAD_DIM
    grid_spec = pltpu.PrefetchScalarGridSpec(
        num_scalar_prefetch=2,
        grid=(len(pairs),),
        in_specs=[
            pl.BlockSpec((t, width), lambda s, qt, kt: (qt[s], 0)),
            pl.BlockSpec((t, width), lambda s, qt, kt: (kt[s], 1)),
            pl.BlockSpec((t, width), lambda s, qt, kt: (kt[s], 2)),
            pl.BlockSpec((t, t), lambda s, qt, kt: (qt[s], kt[s])),
            pl.BlockSpec((2, n_heads, ATT_SUB, ATT_SUB), lambda s, qt, kt: (0, 0, 0, 0)),
        ],
        out_specs=pl.BlockSpec((t, width), lambda s, qt, kt: (qt[s], 0)),
        scratch_shapes=[
            pltpu.VMEM((t, width), BF16),
            pltpu.VMEM((t, t), F32),
            pltpu.VMEM((n_heads, t, t), BF16),
            pltpu.VMEM((n_heads, t, 128), F32),
            pltpu.VMEM((n_heads, t, 128), F32),
            pltpu.VMEM((n_heads, t, ATTN_HEAD_DIM), F32),
        ],
    )
    return pl.pallas_call(
        functools.partial(_attention_kernel, n_heads=n_heads, scale=ATTN_HEAD_DIM ** -0.5),
        grid_spec=grid_spec,
        out_shape=jax.ShapeDtypeStruct((s_len, width), BF16),
        compiler_params=_params("arbitrary"),
        name="masked_attention",
    )(qb_tab, kb_tab, p_main, p_main, p_main, mask, bias_tables)


def _t5_bucket(dist):
    max_exact = N_BUCKETS // 2
    d = jnp.maximum(dist, 0)
    log_ratio = jnp.log(jnp.maximum(d, 1).astype(F32) / max_exact) / math.log(MAX_DISTANCE / max_exact)
    large = jnp.minimum(max_exact + (log_ratio * (N_BUCKETS - max_exact)).astype(I32), N_BUCKETS - 1)
    return jnp.where(d < max_exact, d, large)


def _bias_tables(rel_bias):
    assert MAX_DISTANCE <= ATT_SUB
    i = jnp.arange(ATT_SUB, dtype=I32)[:, None]
    j = jnp.arange(ATT_SUB, dtype=I32)[None, :]
    bucket = _t5_bucket(jnp.stack([i - j, i - j + ATT_SUB]))[:, None]
    rel = rel_bias.astype(F32)
    far = rel[N_BUCKETS - 1]
    tables = jnp.zeros((2, rel_bias.shape[1], ATT_SUB, ATT_SUB), F32)
    for b in range(N_BUCKETS):
        tables = jnp.where(bucket == b, (rel[b] - far)[None, :, None, None], tables)
    return tables * LOG2E


POOL_TR = 512
POOL_G = 256
HALO = 16


def _pool_kernel(x_ref, prev_ref, w_ref, sc_ref, o_ref):
    i = pl.program_id(0)
    g = pl.program_id(1)
    tr = POOL_TR
    cur = x_ref[...]
    prev = jnp.where(i > 0, prev_ref[...], 0.0)
    ext = jnp.concatenate([prev, cur], axis=0)
    sums = []
    run = ext
    for shift in (1, 2, 4, 8):
        run = run + pltpu.roll(run, shift, 0)
        sums.append(run[HALO:])
    win_sum = jnp.where(g == 0, sums[0], jnp.where(g == 1, sums[1], jnp.where(g == 2, sums[2], sums[3])))
    win = jnp.left_shift(jnp.int32(2), g)
    t_pos = i * tr + lax.broadcasted_iota(I32, (tr, 1), 0)
    count = jnp.minimum(t_pos + 1, win).astype(F32)
    pooled = win_sum / count - cur
    o_ref[...] = (_dot(pooled.astype(BF16), w_ref[0]) * sc_ref[...]).astype(o_ref.dtype)


def multiscale_pool(p_aux, pool_w, pool_scale):
    s_len = p_aux.shape[0]
    tr, gw = POOL_TR, POOL_G
    ng = len(POOL_WINDOWS)
    return pl.pallas_call(
        _pool_kernel,
        grid=(s_len // tr, ng),
        in_specs=[
            pl.BlockSpec((tr, gw), lambda i, g: (i, g)),
            pl.BlockSpec((HALO, gw), lambda i, g: (jnp.maximum(i * (tr // HALO) - 1, 0), g)),
            pl.BlockSpec((1, gw, gw), lambda i, g: (g, 0, 0)),
            pl.BlockSpec((1, gw), lambda i, g: (0, g)),
        ],
        out_specs=pl.BlockSpec((tr, gw), lambda i, g: (i, g)),
        out_shape=jax.ShapeDtypeStruct((s_len, ng * gw), BF16),
        compiler_params=_params("parallel", "parallel"),
        name="multiscale_pool",
    )(p_aux, p_aux, pool_w, pool_scale.reshape(1, ng * gw))


def _split3(x):
    hi = x.astype(BF16)
    r1 = x - hi.astype(F32)
    mid = r1.astype(BF16)
    lo = (r1 - mid.astype(F32)).astype(BF16)
    return hi, mid, lo


def _dot_exact_lhs(a_bf16, x):
    hi, mid, lo = _split3(x)
    return _dot(a_bf16, lo) + _dot(a_bf16, mid) + _dot(a_bf16, hi)


def _dot_exact_rhs(x, b_bf16):
    hi, mid, lo = _split3(x)
    return _dot(lo, b_bf16) + _dot(mid, b_bf16) + _dot(hi, b_bf16)


def _softplus(x):
    return jnp.maximum(x, 0.0) + jnp.log(1.0 + jnp.exp(-jnp.abs(x)))


PREP_CHUNKS = 2


def _gdn_prep_kernel(x_ref, prev_ref, cw_ref, ba_ref, alog_ref, dtb_ref,
                     u_ref, w_ref, qd_ref, kdt_ref, intra_ref, dl_ref, ext_ref, *, n_vheads):
    c = CHUNK
    dk = GDN_DK
    assert 2 * c == 128 and dk == 128 and GDN_DV == 128
    scale = dk ** -0.5
    n_kheads = n_vheads // 2
    step = pl.program_id(0)
    for sb in range(2 * n_kheads + n_vheads):
        cols = slice(sb * 128, (sb + 1) * 128)
        ext_ref[sb, :HALO, :] = jnp.where(step > 0, prev_ref[:, cols].astype(F32), 0.0)
        ext_ref[sb, HALO:, :] = x_ref[:, cols].astype(F32)
    cw = cw_ref[...]

    def conv_head(sb, ci, out_scale):
        cols = slice(sb * 128, (sb + 1) * 128)
        base = HALO + ci * c
        y = ext_ref[sb, base:base + c, :] * cw[CONV_K - 1:CONV_K, cols]
        for tap in range(CONV_K - 1):
            lo = base - (CONV_K - 1 - tap)
            y = y + ext_ref[sb, lo:lo + c, :] * cw[tap:tap + 1, cols]
        y = y / (1.0 + jnp.exp2(y * (-LOG2E)))
        if out_scale is not None:
            y = y * (lax.rsqrt(jnp.sum(y * y, axis=-1, keepdims=True) + RMS_EPS) * out_scale)
        return y

    row = lax.broadcasted_iota(I32, (c, c), 0)
    col = lax.broadcasted_iota(I32, (c, c), 1)
    lower = jnp.where(row >= col, 1.0, 0.0).astype(BF16)
    upper = jnp.where(row <= col, 1.0, 0.0).astype(BF16)
    row_w = lax.broadcasted_iota(I32, (c, 2 * c), 0)
    lane_w = lax.broadcasted_iota(I32, (c, 2 * c), 1)
    first = lane_w < c
    col_w = jnp.where(first, lane_w, lane_w - c)
    tril_w = row_w >= col_w
    strict_w = row_w > col_w
    eye_w = jnp.where(row_w == col_w, 1.0, 0.0).astype(F32)

    def block_diag(wide16):
        zero = jnp.zeros_like(wide16)
        return jnp.concatenate([jnp.where(first, wide16, zero), jnp.where(first, zero, wide16)], axis=0)

    pairs = range(n_vheads // 2)
    pws, t_invs, rhss, where = [], [], [], []
    for ci in range(PREP_CHUNKS):
        rows = slice(ci * c, (ci + 1) * c)
        ba = ba_ref[rows, :]
        beta_all = jax.nn.sigmoid(ba)
        g_all = -jnp.exp(alog_ref[...]) * _softplus(ba + dtb_ref[...])
        gc_all = _dot_exact_lhs(lower, g_all)
        gct_all = _dot_exact_rhs(g_all.T, upper)
        for p in pairs:
            kh = conv_head(n_kheads + p, ci, 1.0)
            qh = conv_head(p, ci, scale)
            kh_t = kh.T
            kh16 = kh.astype(BF16)
            kk16 = jnp.concatenate([kh16, kh16], axis=0)
            both = _dot_nt(jnp.concatenate([kh16, qh.astype(BF16)], axis=0), kk16)
            gram, qk = both[:c], both[c:]
            hv = (2 * p, 2 * p + 1)
            beta = [jnp.broadcast_to(beta_all[:, h:h + 1], (c, 2 * c)) for h in hv]
            gcol = [jnp.broadcast_to(gc_all[:, n_vheads + h:n_vheads + h + 1], (c, 2 * c)) for h in hv]
            grow = [gct_all[n_vheads + h:n_vheads + h + 1, :] for h in hv]
            grow_w = jnp.concatenate(grow, axis=1)
            gcol_w = jnp.where(first, gcol[0], gcol[1])
            decay_w = jnp.where(tril_w, jnp.exp(jnp.where(tril_w, gcol_w - grow_w, 0.0)), 0.0)
            a_w = jnp.where(strict_w, -(jnp.where(first, beta[0], beta[1]) * gram * decay_w), 0.0)
            intra_ref[ci, p] = jnp.where(tril_w, qk * decay_w, 0.0).astype(BF16)
            rhs = []
            for i, h in enumerate(hv):
                eg = jnp.exp(gcol[i])
                g_last = grow[i][:, c - 1:c]
                sl = slice(h * GDN_DV, (h + 1) * GDN_DV)
                qd_ref[rows, sl] = (qh * eg).astype(BF16)
                kdt_ref[ci, h] = (kh_t * jnp.exp(g_last - grow[i])).astype(BF16)
                dl_ref[ci, h] = jnp.broadcast_to(jnp.exp(g_last), (1, 128))
                vh = conv_head(2 * n_kheads + h, ci, None)
                rhs.append(jnp.concatenate([vh * beta[i], kh * (beta[i] * eg)], axis=-1).astype(BF16))
            pws.append(a_w)
            t_invs.append(eye_w + a_w)
            rhss.append(jnp.concatenate(rhs, axis=0))
            where.append((rows, p))
    n_levels = int(math.log2(c)) - 1
    pws = [_dot(a.astype(BF16), block_diag(a.astype(BF16))) for a in pws]
    for level in range(n_levels):
        new_t, new_pw = [], []
        for t, pw in zip(t_invs, pws):
            pw16 = pw.astype(BF16)
            if level == n_levels - 1:
                new_t.append(t + _dot(t.astype(BF16), block_diag(pw16)))
            else:
                both = _dot(jnp.concatenate([t.astype(BF16), pw16], axis=0), block_diag(pw16))
                new_t.append(t + both[:c])
                new_pw.append(both[c:])
        t_invs, pws = new_t, new_pw
    for (rows, p), t_inv, rhs in zip(where, t_invs, rhss):
        t16 = t_inv.astype(BF16)
        zero = jnp.zeros_like(t16)
        lhs = jnp.concatenate([jnp.where(first, t16, zero), jnp.where(first, zero, t16)], axis=0)
        sol = _dot(lhs, rhs)
        for i, h in enumerate((2 * p, 2 * p + 1)):
            sl = slice(h * GDN_DV, (h + 1) * GDN_DV)
            u_ref[rows, sl] = sol[i * c:(i + 1) * c, :GDN_DV]
            w_ref[rows, sl] = sol[i * c:(i + 1) * c, GDN_DV:].astype(BF16)


def gdn_prepare(p_main, conv_w, ba, a_log, dt_bias, n_kheads, n_vheads):
    s_len = p_main.shape[0]
    nc = PREP_CHUNKS
    c = CHUNK * nc
    n = s_len // c
    v_w = n_vheads * GDN_DV
    conv_width = 2 * n_kheads * GDN_DK + v_w
    assert 2 * n_vheads <= 128 and n_vheads == 2 * n_kheads and c % HALO == 0
    pad = jnp.zeros((128 - 2 * n_vheads,), F32)
    alog_row = jnp.concatenate([jnp.zeros((n_vheads,), F32), a_log, pad]).reshape(1, 128)
    dtb_row = jnp.concatenate([jnp.zeros((n_vheads,), F32), dt_bias, pad]).reshape(1, 128)
    return pl.pallas_call(
        functools.partial(_gdn_prep_kernel, n_vheads=n_vheads),
        grid=(n,),
        in_specs=[
            pl.BlockSpec((c, conv_width), lambda i: (i, 0)),
            pl.BlockSpec((HALO, conv_width), lambda i: (jnp.maximum(i * (c // HALO) - 1, 0), 0)),
            pl.BlockSpec((CONV_K, conv_width), lambda i: (0, 0)),
            pl.BlockSpec((c, 128), lambda i: (i, 0)),
            pl.BlockSpec((1, 128), lambda i: (0, 0)),
            pl.BlockSpec((1, 128), lambda i: (0, 0)),
        ],
        out_specs=[
            pl.BlockSpec((c, v_w), lambda i: (i, 0)),
            pl.BlockSpec((c, v_w), lambda i: (i, 0)),
            pl.BlockSpec((c, v_w), lambda i: (i, 0)),
            pl.BlockSpec((nc, n_vheads, GDN_DK, CHUNK), lambda i: (i, 0, 0, 0)),
            pl.BlockSpec((nc, n_vheads // 2, CHUNK, 2 * CHUNK), lambda i: (i, 0, 0, 0)),
            pl.BlockSpec((nc, n_vheads, 1, 128), lambda i: (i, 0, 0, 0)),
        ],
        out_shape=[
            jax.ShapeDtypeStruct((s_len, v_w), F32),
            jax.ShapeDtypeStruct((s_len, v_w), BF16),
            jax.ShapeDtypeStruct((s_len, v_w), BF16),
            jax.ShapeDtypeStruct((n * nc, n_vheads, GDN_DK, CHUNK), BF16),
            jax.ShapeDtypeStruct((n * nc, n_vheads // 2, CHUNK, 2 * CHUNK), BF16),
            jax.ShapeDtypeStruct((n * nc, n_vheads, 1, 128), F32),
        ],
        scratch_shapes=[pltpu.VMEM((conv_width // 128, HALO + c, 128), F32)],
        compiler_params=_params("parallel"),
        name="gdn_prepare",
    )(p_main, p_main, conv_w, ba, alog_row, dtb_row)


SCAN_HG = 16
SCAN_CHUNKS = 8


def _gdn_scan_kernel(u_ref, w_ref, qd_ref, kdt_ref, intra_ref, dl_ref, z_ref, nw_ref, o_ref, state_ref):
    @pl.when(pl.program_id(1) == 0)
    def _():
        state_ref[...] = jnp.zeros_like(state_ref)

    nw = nw_ref[...]
    heads = range(SCAN_HG)
    sls = [slice(j * GDN_DV, (j + 1) * GDN_DV) for j in heads]
    first = lax.broadcasted_iota(I32, (CHUNK, 2 * CHUNK), 1) < CHUNK
    states = [state_ref[j] for j in heads]
    for ci in range(SCAN_CHUNKS):
        rows = slice(ci * CHUNK, (ci + 1) * CHUNK)
        s16 = [s.astype(BF16) for s in states]
        ws = [_dot(w_ref[rows, sls[j]], s16[j]) for j in heads]
        qs = [_dot(qd_ref[rows, sls[j]], s16[j]) for j in heads]
        v16 = [(u_ref[rows, sls[j]] - ws[j]).astype(BF16) for j in heads]
        upd = [_dot(kdt_ref[ci, j], v16[j]) for j in heads]
        outs = []
        for p in range(SCAN_HG // 2):
            pair = intra_ref[ci, p]
            zero = jnp.zeros_like(pair)
            v_pair = jnp.concatenate([v16[2 * p], v16[2 * p + 1]], axis=0)
            outs.append(qs[2 * p] + _dot(jnp.where(first, pair, zero), v_pair))
            outs.append(qs[2 * p + 1] + _dot(jnp.where(first, zero, pair), v_pair))
        states = [states[j] * dl_ref[ci, j] + upd[j] for j in heads]
        for j in heads:
            z = z_ref[rows, sls[j]].astype(F32)
            o_ref[rows, sls[j]] = (_rms(outs[j], nw) * (z * jax.nn.sigmoid(z))).astype(o_ref.dtype)
    for j in heads:
        state_ref[j] = states[j]


def gdn_scan(u, w, qd, kdt, intra, dl, p_main, norm_w, z_col_block):
    s_len, v_w = u.shape
    nc = SCAN_CHUNKS
    c = CHUNK * nc
    n = s_len // c
    hg = SCAN_HG
    gw = hg * GDN_DV
    return pl.pallas_call(
        _gdn_scan_kernel,
        grid=(v_w // gw, n),
        in_specs=[
            pl.BlockSpec((c, gw), lambda g, i: (i, g)),
            pl.BlockSpec((c, gw), lambda g, i: (i, g)),
            pl.BlockSpec((c, gw), lambda g, i: (i, g)),
            pl.BlockSpec((nc, hg, GDN_DK, CHUNK), lambda g, i: (i, g, 0, 0)),
            pl.BlockSpec((nc, hg // 2, CHUNK, 2 * CHUNK), lambda g, i: (i, g, 0, 0)),
            pl.BlockSpec((nc, hg, 1, 128), lambda g, i: (i, g, 0, 0)),
            pl.BlockSpec((c, gw), lambda g, i: (i, z_col_block + g)),
            pl.BlockSpec((1, GDN_DV), lambda g, i: (0, 0)),
        ],
        out_specs=pl.BlockSpec((c, gw), lambda g, i: (i, g)),
        out_shape=jax.ShapeDtypeStruct((s_len, v_w), BF16),
        scratch_shapes=[pltpu.VMEM((hg, GDN_DK, GDN_DV), F32)],
        compiler_params=_params("parallel", "arbitrary"),
        name="gdn_scan",
    )(u, w, qd, kdt, intra, dl, p_main, norm_w.reshape(1, GDN_DV))


PROJ_TM = 1024
PROJ_TN = 1024
FFN_TM = 512
FFN_TF = 512
OUT_TM = 512
OUT_TK_MAX = 2048


def _out_tk(k):
    return max(c for c in range(128, OUT_TK_MAX + 1, 128) if k % c == 0)


def _even_main_width(d):
    return 3 * (d // 2) + N_IDX_HEADS * IDX_HEAD_DIM


def _odd_main_width(d):
    n_kheads = d // 128
    return 2 * n_kheads * GDN_DK + 2 * (2 * n_kheads) * GDN_DV


def _even_mixer(x, g_pre, g_post, w_in16t, w_rest, w_o16, layer, pool_w, pool_scale, bias_tables):
    s_len, d = x.shape
    attn_w = d // 2
    n_heads = attn_w // ATTN_HEAD_DIM
    idx_w = N_IDX_HEADS * IDX_HEAD_DIM
    pool_width = len(POOL_WINDOWS) * POOL_G
    main_w = _even_main_width(d)
    assert main_w == 3 * attn_w + idx_w == 4096 and pool_width == 1024
    small = IDX_HEAD_DIM + N_IDX_HEADS
    assert w_rest.shape[0] == small + pool_width
    w_aux = jnp.concatenate([w_rest[small:], w_rest[:small], jnp.zeros((128 - small, d), BF16)], axis=0)
    p_main = norm_matmul(x, g_pre, w_in16t, layer, main_w, BF16, PROJ_TM, PROJ_TN)
    p_aux = norm_matmul(x, g_pre, w_aux[None], 0, w_aux.shape[0], F32, PROJ_TM, w_aux.shape[0])
    k_idx_t = p_aux[:, pool_width:pool_width + IDX_HEAD_DIM].astype(BF16).T
    kt2 = jnp.concatenate([k_idx_t, k_idx_t], axis=0)
    mask = indexer_mask(p_main, p_aux, kt2, min(TOPK_MAX, s_len // 4))
    attn = masked_attention(p_main, mask, bias_tables, n_heads)
    pooled = multiscale_pool(p_aux, pool_w.astype(BF16), pool_scale)
    return mm_norm_res([attn, pooled], w_o16, layer, g_post, x, OUT_TM, _out_tk(attn.shape[1]))


def _odd_mixer(x, g_pre, g_post, w_in16t, w_rest, w_o16, layer, conv_w, a_log, dt_bias, norm_w):
    s_len, d = x.shape
    n_kheads = d // 128
    n_vheads = 2 * n_kheads
    qk_w = n_kheads * GDN_DK
    v_w = n_vheads * GDN_DV
    conv_width = 2 * qk_w + v_w
    main_w = _odd_main_width(d)
    assert main_w == conv_width + v_w
    w_ba = jnp.concatenate([w_rest, jnp.zeros((128 - 2 * n_vheads, d), BF16)], axis=0)
    p_main = norm_matmul(x, g_pre, w_in16t, layer, main_w, BF16, PROJ_TM, PROJ_TN)
    ba = norm_matmul(x, g_pre, w_ba[None], 0, 128, F32, PROJ_TM, 128)
    u, w, qd, kdt, intra, dl = gdn_prepare(p_main, conv_w, ba, a_log, dt_bias, n_kheads, n_vheads)
    o = gdn_scan(u, w, qd, kdt, intra, dl, p_main, norm_w, conv_width // (SCAN_HG * GDN_DV))
    return mm_norm_res([o], w_o16, layer, g_post, x, OUT_TM, _out_tk(v_w))


def _ffn(x, g_pre, g_post, w_gate16, w_up16, w_down16, layer):
    return ffn(x, g_pre, g_post, w_gate16, w_up16, w_down16, layer, FFN_TM, FFN_TF)


def kernel(x, rel_bias, even_w_in, even_w_o, pool_w, pool_scale, odd_w_in, conv_w, a_log, dt_bias,
           gdn_norm_w, odd_w_o, ffn_w_gate, ffn_w_up, ffn_w_down, mix_pre_g, mix_post_g, ffn_pre_g, ffn_post_g):
    b, s_len, d = x.shape
    depth = ffn_w_gate.shape[0]
    bias_tables = _bias_tables(rel_bias)
    em, om = _even_main_width(d), _odd_main_width(d)
    even_t, odd_t = jnp.swapaxes(even_w_in, 1, 2), jnp.swapaxes(odd_w_in, 1, 2)
    even_in16 = cast_bf16(even_t)
    even_rest16 = even_in16[:, em:]
    odd_in16, odd_rest16 = cast_bf16(odd_t, 0, om), cast_bf16(odd_t, om)
    even_o16, odd_o16 = cast_bf16(even_w_o), cast_bf16(odd_w_o)
    gate16, up16, down16 = cast_bf16(ffn_w_gate), cast_bf16(ffn_w_up), cast_bf16(ffn_w_down)
    xs = x.reshape(b * s_len, d)
    outs = []
    for bi in range(b):
        h = xs[bi * s_len:(bi + 1) * s_len]
        for layer in range(depth):
            i = layer // 2
            if layer % 2 == 0:
                h = _even_mixer(h, mix_pre_g[layer], mix_post_g[layer], even_in16, even_rest16[i], even_o16, i,
                                pool_w[i], pool_scale[i], bias_tables)
            else:
                h = _odd_mixer(h, mix_pre_g[layer], mix_post_g[layer], odd_in16, odd_rest16[i], odd_o16, i,
                               conv_w[i], a_log[i], dt_bias[i], gdn_norm_w[i])
            h = _ffn(h, ffn_pre_g[layer], ffn_post_g[layer], gate16, up16, down16, layer)
        outs.append(h)
    return jnp.concatenate(outs, axis=0).reshape(b, s_len, d)
```

```python
import functools
import math

import jax
import jax.numpy as jnp
from jax import lax
from jax.experimental import pallas as pl
from jax.experimental.pallas import tpu as pltpu

F32 = jnp.float32
BF16 = jnp.bfloat16
I32 = jnp.int32

RMS_EPS = 1e-6
NEG_INF = -1e30
INT_MIN = -2 ** 31
LOG2E = 1.4426950408889634

VMEM_LIMIT_BYTES = 56 * 1024 * 1024

ATTN_HEAD_DIM = 128
N_IDX_HEADS = 16
IDX_HEAD_DIM = 64
TOPK_MAX = 256
N_BUCKETS = 32
MAX_DISTANCE = 128
POOL_WINDOWS = (2, 4, 8, 16)
GDN_DK = 128
GDN_DV = 128
CONV_K = 4
CHUNK = 64


def _params(*sem):
    return pltpu.CompilerParams(dimension_semantics=sem, vmem_limit_bytes=VMEM_LIMIT_BYTES)


def _dot(a, b):
    return jnp.dot(a, b, preferred_element_type=F32)


def _dot_nt(a, b):
    return lax.dot_general(a, b, (((1,), (1,)), ((), ())), preferred_element_type=F32)


def _to_u32(x):
    return lax.bitcast_convert_type(x ^ INT_MIN, jnp.uint32)


def _to_i32(x):
    return lax.bitcast_convert_type(x, I32)


def _rms(x, g):
    ms = jnp.mean(x * x, axis=-1, keepdims=True)
    return x * lax.rsqrt(ms + RMS_EPS) * g


def _norm_matmul_kernel(x_ref, g_ref, wt_ref, o_ref, hn_ref):
    @pl.when(pl.program_id(1) == 0)
    def _():
        hn_ref[...] = _rms(x_ref[...], g_ref[...]).astype(BF16)

    o_ref[...] = _dot_nt(hn_ref[...], wt_ref[...]).astype(o_ref.dtype)


def norm_matmul(x, g, wt, layer, n, out_dtype, tm, tn):
    m, k = x.shape
    assert m % tm == 0 and n % tn == 0 and n <= wt.shape[1] and wt.shape[2] == k
    return pl.pallas_call(
        _norm_matmul_kernel,
        grid=(m // tm, n // tn),
        in_specs=[
            pl.BlockSpec((tm, k), lambda i, j: (i, 0)),
            pl.BlockSpec((1, k), lambda i, j: (0, 0)),
            pl.BlockSpec((None, tn, k), lambda i, j: (layer, j, 0)),
        ],
        out_specs=pl.BlockSpec((tm, tn), lambda i, j: (i, j)),
        out_shape=jax.ShapeDtypeStruct((m, n), out_dtype),
        scratch_shapes=[pltpu.VMEM((tm, k), BF16)],
        compiler_params=_params("parallel", "arbitrary"),
        name="norm_matmul",
    )(x, g.reshape(1, k), wt)


def _ffn_kernel(x_ref, g_ref, wg_ref, wu_ref, wd_ref, gp_ref, o_ref, hn_ref, acc_ref, *, nj):
    j = pl.program_id(1)

    @pl.when(j == 0)
    def _():
        hn_ref[...] = _rms(x_ref[...], g_ref[...]).astype(BF16)
        acc_ref[...] = jnp.zeros_like(acc_ref)

    hn = hn_ref[...]
    a = _dot(hn, wg_ref[...])
    b = _dot(hn, wu_ref[...])
    act = (a * jax.nn.sigmoid(a) * b).astype(BF16)
    acc_ref[...] += _dot(act, wd_ref[...])

    @pl.when(j == nj - 1)
    def _():
        o_ref[...] = x_ref[...] + _rms(acc_ref[...], gp_ref[...])


def ffn(x, g_pre, g_post, wg, wu, wd, layer, tm, tf):
    m, d = x.shape
    ff = wg.shape[2]
    assert m % tm == 0 and ff % tf == 0
    nj = ff // tf
    return pl.pallas_call(
        functools.partial(_ffn_kernel, nj=nj),
        grid=(m // tm, nj),
        in_specs=[
            pl.BlockSpec((tm, d), lambda i, j: (i, 0)),
            pl.BlockSpec((1, d), lambda i, j: (0, 0)),
            pl.BlockSpec((None, d, tf), lambda i, j: (layer, 0, j)),
            pl.BlockSpec((None, d, tf), lambda i, j: (layer, 0, j)),
            pl.BlockSpec((None, tf, d), lambda i, j: (layer, j, 0)),
            pl.BlockSpec((1, d), lambda i, j: (0, 0)),
        ],
        out_specs=pl.BlockSpec((tm, d), lambda i, j: (i, 0)),
        out_shape=jax.ShapeDtypeStruct((m, d), F32),
        scratch_shapes=[pltpu.VMEM((tm, d), BF16), pltpu.VMEM((tm, d), F32)],
        compiler_params=_params("parallel", "arbitrary"),
        name="ffn",
    )(x, g_pre.reshape(1, d), wg, wu, wd, g_post.reshape(1, d))


def _mm_norm_res_kernel(*refs, n_seg, nk):
    a_refs = refs[:n_seg]
    w_refs = refs[n_seg:2 * n_seg]
    g_ref, r_ref, o_ref, acc_ref = refs[2 * n_seg:]
    kk = pl.program_id(1)

    @pl.when(kk == 0)
    def _():
        acc_ref[...] = jnp.zeros_like(acc_ref)

    part = _dot(a_refs[0][...], w_refs[0][...])
    for s in range(1, n_seg):
        part = part + _dot(a_refs[s][...], w_refs[s][...])
    acc_ref[...] += part

    @pl.when(kk == nk - 1)
    def _():
        o_ref[...] = r_ref[...] + _rms(acc_ref[...], g_ref[...])


def mm_norm_res(a_list, w, layer, g, res, tm, tk):
    n_seg = len(a_list)
    m, ka = a_list[0].shape
    n = w.shape[2]
    assert w.shape[1] == n_seg * ka and m % tm == 0 and ka % tk == 0
    nk = ka // tk
    a_specs = [pl.BlockSpec((tm, tk), lambda i, k: (i, k)) for _ in range(n_seg)]
    w_specs = [pl.BlockSpec((None, tk, n), functools.partial(lambda i, k, s: (layer, k + s * nk, 0), s=s))
               for s in range(n_seg)]
    return pl.pallas_call(
        functools.partial(_mm_norm_res_kernel, n_seg=n_seg, nk=nk),
        grid=(m // tm, nk),
        in_specs=a_specs + w_specs + [
            pl.BlockSpec((1, n), lambda i, k: (0, 0)),
            pl.BlockSpec((tm, n), lambda i, k: (i, 0)),
        ],
        out_specs=pl.BlockSpec((tm, n), lambda i, k: (i, 0)),
        out_shape=jax.ShapeDtypeStruct((m, n), F32),
        scratch_shapes=[pltpu.VMEM((tm, n), F32)],
        compiler_params=_params("parallel", "arbitrary"),
        name="mm_norm_res",
    )(*a_list, *([w] * n_seg), g.reshape(1, n), res)


CAST_BLOCK_BYTES = 4 * 1024 * 1024


def _cast_kernel(x_ref, o_ref):
    o_ref[...] = x_ref[...].astype(o_ref.dtype)


def cast_bf16(w, row0=0, height=None):
    layers, n, k = w.shape
    height = n - row0 if height is None else height
    assert row0 + height <= n
    tr = max(r for r in range(16, height + 1, 16)
             if height % r == 0 and row0 % r == 0 and r * k * 4 <= CAST_BLOCK_BYTES)
    return pl.pallas_call(
        _cast_kernel,
        grid=(layers, height // tr),
        in_specs=[pl.BlockSpec((None, tr, k), lambda l, i: (l, row0 // tr + i, 0))],
        out_specs=pl.BlockSpec((None, tr, k), lambda l, i: (l, i, 0)),
        out_shape=jax.ShapeDtypeStruct((layers, height, k), BF16),
        compiler_params=_params("parallel", "parallel"),
        name="cast_bf16",
    )(w)


IDX_TQ = 128
IDX_TK = 512
IDX_ROWS = 128
IDX_STEPS_PER_CHECK = 4


def _indexer_kernel(qi_ref, aux_ref, kt_ref, mask_ref, key_ref, qh_ref, wb_ref, *, topk, w_scale):
    tq, tk, tr = IDX_TQ, IDX_TK, IDX_ROWS
    nj = tk // 128
    s_len = mask_ref.shape[1]
    qb = pl.program_id(0)
    n_chunks = ((qb + 1) * tq - 1) // tk + 1

    lane = lax.broadcasted_iota(I32, (tq, 128), 1)
    aux = aux_ref[...]
    for h in range(N_IDX_HEADS):
        pair = qi_ref[:, (h // 2) * 128:(h // 2 + 1) * 128]
        keep = (lane < IDX_HEAD_DIM) if h % 2 == 0 else (lane >= IDX_HEAD_DIM)
        qh_ref[h] = jnp.where(keep, pair, jnp.zeros_like(pair))
        wcol = aux[:, IDX_HEAD_DIM + h:IDX_HEAD_DIM + h + 1] * w_scale
        wb_ref[h] = jnp.broadcast_to(wcol, (tq, 128))

    lane_r = lax.broadcasted_iota(I32, (tr, 128), 1)
    row_r = lax.broadcasted_iota(I32, (tr, 128), 0)

    def score_chunk(c, tops):
        col0 = pl.multiple_of(c * tk, tk)
        kt = kt_ref[:, pl.ds(col0, tk)]
        new_tops = []
        for r in range(tq // tr):
            rows = slice(r * tr, (r + 1) * tr)
            accs = [jnp.zeros((tr, 128), F32) for _ in range(nj)]
            d_next = _dot(qh_ref[0, rows], kt)
            for h in range(N_IDX_HEADS):
                d = d_next
                if h + 1 < N_IDX_HEADS:
                    d_next = _dot(qh_ref[h + 1, rows], kt)
                wb = wb_ref[h, rows]
                for j in range(nj):
                    accs[j] = accs[j] + wb * jnp.maximum(d[:, j * 128:(j + 1) * 128], 0.0)
            q_pos = qb * tq + r * tr + row_r
            m1, m2 = tops[r]
            for j in range(nj):
                bits = pltpu.bitcast(accs[j], I32)
                skey = bits ^ ((bits >> 31) & 0x7FFFFFFF)
                skey = jnp.where(col0 + j * 128 + lane_r <= q_pos, skey, INT_MIN)
                key_ref[rows, pl.ds(pl.multiple_of(col0 + j * 128, 128), 128)] = skey
                m2 = jnp.maximum(m2, jnp.minimum(m1, skey))
                m1 = jnp.maximum(m1, skey)
            new_tops.append((m1, m2))
        return tuple(new_tops)

    lowest = jnp.full((tr, 128), INT_MIN, I32)
    tops = lax.fori_loop(0, n_chunks, score_chunk, tuple((lowest, lowest) for _ in range(tq // tr)))
    m1 = jnp.concatenate([t[0] for t in tops], axis=0)
    m2 = jnp.concatenate([t[1] for t in tops], axis=0)
    assert topk <= 2 * 128 and 127 + s_len // 128 < topk
    top = jnp.max(m2, axis=-1, keepdims=True)
    n_top = jnp.sum(jnp.where(m2 == top, 1, 0), axis=-1, keepdims=True)
    below = jnp.max(jnp.where(m2 == top, INT_MIN, m2), axis=-1, keepdims=True)
    hi = _to_u32(jnp.where(n_top >= 2, top, below))
    lo = _to_u32(jnp.min(m2, axis=-1, keepdims=True))
    n_iter = 32 - jnp.max(lax.clz(_to_i32(hi - lo)))

    def count_ge(cand):
        cand_b = jnp.broadcast_to(cand, (tq, 128))

        def body(c, cnt):
            blk = key_ref[:, pl.ds(pl.multiple_of(c * tk, tk), tk)]
            for j in range(nj):
                cnt = cnt + jnp.where(blk[:, j * 128:(j + 1) * 128] >= cand_b, 1, 0)
            return cnt

        cnt = lax.fori_loop(0, n_chunks, body, jnp.zeros((tq, 128), I32))
        return jnp.sum(cnt, axis=-1, keepdims=True)

    def not_settled(state):
        i, _, _, _, settled = state
        return jnp.logical_and(i < n_iter, settled == 0)

    def bisect(state):
        i, lo, hi, n_lo, _ = state
        for _ in range(IDX_STEPS_PER_CHECK):
            width = hi - lo
            mid = lo + (width >> 1) + (width & 1)
            n_mid = count_ge(_to_i32(mid) ^ INT_MIN)
            ok = n_mid >= topk
            lo = jnp.where(ok, mid, lo)
            hi = jnp.where(ok, hi, jnp.where(width > 0, mid - 1, hi))
            n_lo = jnp.where(ok, n_mid, n_lo)
        settled = jnp.min(jnp.where(jnp.logical_or(n_lo == topk, hi == lo), 1, 0))
        return i + IDX_STEPS_PER_CHECK, lo, hi, n_lo, settled

    unknown = jnp.full((tq, 1), s_len + 1, I32)
    _, tau_u, _, _, _ = lax.while_loop(not_settled, bisect, (jnp.int32(0), lo, hi, unknown, jnp.int32(0)))
    tau_u = _to_i32(tau_u)
    tau = jnp.broadcast_to(jnp.maximum(tau_u ^ INT_MIN, INT_MIN + 1), (tq, 128))

    neg = jnp.full((tq, tk), NEG_INF, F32).astype(BF16)

    def write_chunk(c, carry):
        col0 = pl.multiple_of(c * tk, tk)

        @pl.when(c < n_chunks)
        def _():
            blk = key_ref[:, pl.ds(col0, tk)]
            parts = [jnp.where(blk[:, j * 128:(j + 1) * 128] >= tau, 0.0, NEG_INF) for j in range(nj)]
            mask_ref[:, pl.ds(col0, tk)] = jnp.concatenate(parts, axis=-1).astype(BF16)

        @pl.when(c >= n_chunks)
        def _():
            mask_ref[:, pl.ds(col0, tk)] = neg

        return carry

    lax.fori_loop(0, s_len // tk, write_chunk, 0)


def indexer_mask(p_main, p_aux, kt2, topk):
    s_len = p_main.shape[0]
    tq = IDX_TQ
    w_scale = (N_IDX_HEADS ** -0.5) * (IDX_HEAD_DIM ** -0.5)
    return pl.pallas_call(
        functools.partial(_indexer_kernel, topk=topk, w_scale=w_scale),
        grid=(s_len // tq,),
        in_specs=[
            pl.BlockSpec((tq, 1024), lambda i: (i, 3)),
            pl.BlockSpec((tq, 128), lambda i: (i, 8)),
            pl.BlockSpec((128, s_len), lambda i: (0, 0)),
        ],
        out_specs=pl.BlockSpec((tq, s_len), lambda i: (i, 0)),
        out_shape=jax.ShapeDtypeStruct((s_len, s_len), BF16),
        scratch_shapes=[
            pltpu.VMEM((tq, s_len), I32),
            pltpu.VMEM((N_IDX_HEADS, tq, 128), BF16),
            pltpu.VMEM((N_IDX_HEADS, tq, 128), F32),
        ],
        compiler_params=_params("parallel"),
        name="indexer_mask",
    )(p_main, p_aux, kt2)


ATT_T = 512
ATT_RC = 32
ATT_SUB = 128


def _attention_kernel(qb_tab, kb_tab, q_ref, k_ref, v_ref, mask_ref, tz_ref, o_ref,
                      qs_ref, mf_ref, p_ref, m_ref, l_ref, acc_ref, *, n_heads, scale):
    step = pl.program_id(0)
    qb = qb_tab[step]
    kb = kb_tab[step]
    dh, t, rc, sub = ATTN_HEAD_DIM, ATT_T, ATT_RC, ATT_SUB
    nsub = t // sub

    @pl.when(kb == 0)
    def _():
        qs_ref[...] = (q_ref[...].astype(F32) * (scale * LOG2E)).astype(BF16)
        m_ref[...] = jnp.full_like(m_ref, NEG_INF)
        l_ref[...] = jnp.zeros_like(l_ref)
        acc_ref[...] = jnp.zeros_like(acc_ref)

    mf_ref[...] = mask_ref[...].astype(F32)
    prev_flag = jnp.where(kb == qb - 1, 1.0, 0.0).astype(F32)

    def run(diag):
        sls = [slice(h * dh, (h + 1) * dh) for h in range(n_heads)]

        def softmax(h, s):
            alphas = []
            for r in range(t // rc):
                rows = slice(r * rc, (r + 1) * rc)
                blk, r0 = divmod(r * rc, sub)
                sr = s[rows] + mf_ref[rows, :]
                parts = [sr[:, b * sub:(b + 1) * sub] for b in range(nsub)]
                if diag:
                    parts[blk] = parts[blk] + tz_ref[0, h, r0:r0 + rc, :]
                    if blk >= 1:
                        parts[blk - 1] = parts[blk - 1] + tz_ref[1, h, r0:r0 + rc, :]
                elif blk == 0:
                    parts[nsub - 1] = parts[nsub - 1] + prev_flag * tz_ref[1, h, r0:r0 + rc, :]
                if diag:
                    parts = parts[:blk + 1]
                m_prev = m_ref[h, rows]
                m_new = jnp.maximum(m_prev, jnp.max(functools.reduce(jnp.maximum, parts), axis=-1, keepdims=True))
                alpha = jnp.exp2(m_prev - m_new)
                ps = [jnp.exp2(pb - m_new) for pb in parts]
                l_ref[h, rows] = alpha * l_ref[h, rows] + jnp.sum(functools.reduce(jnp.add, ps), axis=-1, keepdims=True)
                m_ref[h, rows] = m_new
                ps = ps + [jnp.zeros_like(ps[0])] * (nsub - len(ps))
                p_ref[h, rows, :] = jnp.concatenate(ps, axis=-1).astype(BF16)
                alphas.append(alpha)
            return alphas

        def weighted_values(h, alphas):
            pv = _dot(p_ref[h], v_ref[:, sls[h]])
            for r in range(t // rc):
                rows = slice(r * rc, (r + 1) * rc)
                acc_ref[h, rows] = alphas[r] * acc_ref[h, rows] + pv[rows]

        ahead = 2
        logits = [_dot_nt(qs_ref[:, sls[h]], k_ref[:, sls[h]]) for h in range(min(ahead, n_heads))]
        pending = None
        for h in range(n_heads):
            if h + ahead < n_heads:
                logits.append(_dot_nt(qs_ref[:, sls[h + ahead]], k_ref[:, sls[h + ahead]]))
            alphas = softmax(h, logits[h])
            if pending is not None:
                weighted_values(*pending)
            pending = (h, alphas)
        weighted_values(*pending)

    @pl.when(kb != qb)
    def _():
        run(False)

    @pl.when(kb == qb)
    def _():
        run(True)
        for h in range(n_heads):
            o_ref[:, h * dh:(h + 1) * dh] = (acc_ref[h] / l_ref[h]).astype(o_ref.dtype)


def masked_attention(p_main, mask, bias_tables, n_heads):
    s_len = p_main.shape[0]
    t = ATT_T
    nb = s_len // t
    pairs = [(q, k) for q in range(nb) for k in range(q + 1)]
    qb_tab = jnp.array([p[0] for p in pairs], I32)
    kb_tab = jnp.array([p[1] for p in pairs], I32)
    width = n_heads * ATTN_HEAD_DIM
    grid_spec = pltpu.PrefetchScalarGridSpec(
        num_scalar_prefetch=2,
        grid=(len(pairs),),
        in_specs=[
            pl.BlockSpec((t, width), lambda s, qt, kt: (qt[s], 0)),
            pl.BlockSpec((t, width), lambda s, qt, kt: (kt[s], 1)),
            pl.BlockSpec((t, width), lambda s, qt, kt: (kt[s], 2)),
            pl.BlockSpec((t, t), lambda s, qt, kt: (qt[s], kt[s])),
            pl.BlockSpec((2, n_heads, ATT_SUB, ATT_SUB), lambda s, qt, kt: (0, 0, 0, 0)),
        ],
        out_specs=pl.BlockSpec((t, width), lambda s, qt, kt: (qt[s], 0)),
        scratch_shapes=[
            pltpu.VMEM((t, width), BF16),
            pltpu.VMEM((t, t), F32),
            pltpu.VMEM((n_heads, t, t), BF16),
            pltpu.VMEM((n_heads, t, 128), F32),
            pltpu.VMEM((n_heads, t, 128), F32),
            pltpu.VMEM((n_heads, t, ATTN_HEAD_DIM), F32),
        ],
    )
    return pl.pallas_call(
        functools.partial(_attention_kernel, n_heads=n_heads, scale=ATTN_HEAD_DIM ** -0.5),
        grid_spec=grid_spec,
        out_shape=jax.ShapeDtypeStruct((s_len, width), BF16),
        compiler_params=_params("arbitrary"),
        name="masked_attention",
    )(qb_tab, kb_tab, p_main, p_main, p_main, mask, bias_tables)


def _t5_bucket(dist):
    max_exact = N_BUCKETS // 2
    d = jnp.maximum(dist, 0)
    log_ratio = jnp.log(jnp.maximum(d, 1).astype(F32) / max_exact) / math.log(MAX_DISTANCE / max_exact)
    large = jnp.minimum(max_exact + (log_ratio * (N_BUCKETS - max_exact)).astype(I32), N_BUCKETS - 1)
    return jnp.where(d < max_exact, d, large)


def _bias_tables(rel_bias):
    assert MAX_DISTANCE <= ATT_SUB
    i = jnp.arange(ATT_SUB, dtype=I32)[:, None]
    j = jnp.arange(ATT_SUB, dtype=I32)[None, :]
    bucket = _t5_bucket(jnp.stack([i - j, i - j + ATT_SUB]))[:, None]
    rel = rel_bias.astype(F32)
    far = rel[N_BUCKETS - 1]
    tables = jnp.zeros((2, rel_bias.shape[1], ATT_SUB, ATT_SUB), F32)
    for b in range(N_BUCKETS):
        tables = jnp.where(bucket == b, (rel[b] - far)[None, :, None, None], tables)
    return tables * LOG2E


POOL_TR = 512
POOL_G = 256
HALO = 16


def _pool_kernel(x_ref, prev_ref, w_ref, sc_ref, o_ref):
    i = pl.program_id(0)
    g = pl.program_id(1)
    tr = POOL_TR
    cur = x_ref[...]
    prev = jnp.where(i > 0, prev_ref[...], 0.0)
    ext = jnp.concatenate([prev, cur], axis=0)
    sums = []
    run = ext
    for shift in (1, 2, 4, 8):
        run = run + pltpu.roll(run, shift, 0)
        sums.append(run[HALO:])
    win_sum = jnp.where(g == 0, sums[0], jnp.where(g == 1, sums[1], jnp.where(g == 2, sums[2], sums[3])))
    win = jnp.left_shift(jnp.int32(2), g)
    t_pos = i * tr + lax.broadcasted_iota(I32, (tr, 1), 0)
    count = jnp.minimum(t_pos + 1, win).astype(F32)
    pooled = win_sum / count - cur
    o_ref[...] = (_dot(pooled.astype(BF16), w_ref[0]) * sc_ref[...]).astype(o_ref.dtype)


def multiscale_pool(p_aux, pool_w, pool_scale):
    s_len = p_aux.shape[0]
    tr, gw = POOL_TR, POOL_G
    ng = len(POOL_WINDOWS)
    return pl.pallas_call(
        _pool_kernel,
        grid=(s_len // tr, ng),
        in_specs=[
            pl.BlockSpec((tr, gw), lambda i, g: (i, g)),
            pl.BlockSpec((HALO, gw), lambda i, g: (jnp.maximum(i * (tr // HALO) - 1, 0), g)),
            pl.BlockSpec((1, gw, gw), lambda i, g: (g, 0, 0)),
            pl.BlockSpec((1, gw), lambda i, g: (0, g)),
        ],
        out_specs=pl.BlockSpec((tr, gw), lambda i, g: (i, g)),
        out_shape=jax.ShapeDtypeStruct((s_len, ng * gw), BF16),
        compiler_params=_params("parallel", "parallel"),
        name="multiscale_pool",
    )(p_aux, p_aux, pool_w, pool_scale.reshape(1, ng * gw))


def _split3(x):
    hi = x.astype(BF16)
    r1 = x - hi.astype(F32)
    mid = r1.astype(BF16)
    lo = (r1 - mid.astype(F32)).astype(BF16)
    return hi, mid, lo


def _dot_exact_lhs(a_bf16, x):
    hi, mid, lo = _split3(x)
    return _dot(a_bf16, lo) + _dot(a_bf16, mid) + _dot(a_bf16, hi)


def _dot_exact_rhs(x, b_bf16):
    hi, mid, lo = _split3(x)
    return _dot(lo, b_bf16) + _dot(mid, b_bf16) + _dot(hi, b_bf16)


def _softplus(x):
    return jnp.maximum(x, 0.0) + jnp.log(1.0 + jnp.exp(-jnp.abs(x)))


PREP_CHUNKS = 2


def _gdn_prep_kernel(x_ref, prev_ref, cw_ref, ba_ref, alog_ref, dtb_ref,
                     u_ref, w_ref, qd_ref, kdt_ref, intra_ref, dl_ref, ext_ref, *, n_vheads):
    c = CHUNK
    dk = GDN_DK
    assert 2 * c == 128 and dk == 128 and GDN_DV == 128
    scale = dk ** -0.5
    n_kheads = n_vheads // 2
    step = pl.program_id(0)
    for sb in range(2 * n_kheads + n_vheads):
        cols = slice(sb * 128, (sb + 1) * 128)
        ext_ref[sb, :HALO, :] = jnp.where(step > 0, prev_ref[:, cols].astype(F32), 0.0)
        ext_ref[sb, HALO:, :] = x_ref[:, cols].astype(F32)
    cw = cw_ref[...]

    def conv_head(sb, ci, out_scale):
        cols = slice(sb * 128, (sb + 1) * 128)
        base = HALO + ci * c
        y = ext_ref[sb, base:base + c, :] * cw[CONV_K - 1:CONV_K, cols]
        for tap in range(CONV_K - 1):
            lo = base - (CONV_K - 1 - tap)
            y = y + ext_ref[sb, lo:lo + c, :] * cw[tap:tap + 1, cols]
        y = y / (1.0 + jnp.exp2(y * (-LOG2E)))
        if out_scale is not None:
            y = y * (lax.rsqrt(jnp.sum(y * y, axis=-1, keepdims=True) + RMS_EPS) * out_scale)
        return y

    row = lax.broadcasted_iota(I32, (c, c), 0)
    col = lax.broadcasted_iota(I32, (c, c), 1)
    lower = jnp.where(row >= col, 1.0, 0.0).astype(BF16)
    upper = jnp.where(row <= col, 1.0, 0.0).astype(BF16)
    row_w = lax.broadcasted_iota(I32, (c, 2 * c), 0)
    lane_w = lax.broadcasted_iota(I32, (c, 2 * c), 1)
    first = lane_w < c
    col_w = jnp.where(first, lane_w, lane_w - c)
    tril_w = row_w >= col_w
    strict_w = row_w > col_w
    eye_w = jnp.where(row_w == col_w, 1.0, 0.0).astype(F32)

    def block_diag(wide16):
        zero = jnp.zeros_like(wide16)
        return jnp.concatenate([jnp.where(first, wide16, zero), jnp.where(first, zero, wide16)], axis=0)

    pairs = range(n_vheads // 2)
    pws, t_invs, rhss, where = [], [], [], []
    for ci in range(PREP_CHUNKS):
        rows = slice(ci * c, (ci + 1) * c)
        ba = ba_ref[rows, :]
        beta_all = jax.nn.sigmoid(ba)
        g_all = -jnp.exp(alog_ref[...]) * _softplus(ba + dtb_ref[...])
        gc_all = _dot_exact_lhs(lower, g_all)
        gct_all = _dot_exact_rhs(g_all.T, upper)
        for p in pairs:
            kh = conv_head(n_kheads + p, ci, 1.0)
            qh = conv_head(p, ci, scale)
            kh_t = kh.T
            kh16 = kh.astype(BF16)
            kk16 = jnp.concatenate([kh16, kh16], axis=0)
            both = _dot_nt(jnp.concatenate([kh16, qh.astype(BF16)], axis=0), kk16)
            gram, qk = both[:c], both[c:]
            hv = (2 * p, 2 * p + 1)
            beta = [jnp.broadcast_to(beta_all[:, h:h + 1], (c, 2 * c)) for h in hv]
            gcol = [jnp.broadcast_to(gc_all[:, n_vheads + h:n_vheads + h + 1], (c, 2 * c)) for h in hv]
            grow = [gct_all[n_vheads + h:n_vheads + h + 1, :] for h in hv]
            grow_w = jnp.concatenate(grow, axis=1)
            gcol_w = jnp.where(first, gcol[0], gcol[1])
            decay_w = jnp.where(tril_w, jnp.exp(jnp.where(tril_w, gcol_w - grow_w, 0.0)), 0.0)
            a_w = jnp.where(strict_w, -(jnp.where(first, beta[0], beta[1]) * gram * decay_w), 0.0)
            intra_ref[ci, p] = jnp.where(tril_w, qk * decay_w, 0.0).astype(BF16)
            rhs = []
            for i, h in enumerate(hv):
                eg = jnp.exp(gcol[i])
                g_last = grow[i][:, c - 1:c]
                sl = slice(h * GDN_DV, (h + 1) * GDN_DV)
                qd_ref[rows, sl] = (qh * eg).astype(BF16)
                kdt_ref[ci, h] = (kh_t * jnp.exp(g_last - grow[i])).astype(BF16)
                dl_ref[ci, h] = jnp.broadcast_to(jnp.exp(g_last), (1, 128))
                vh = conv_head(2 * n_kheads + h, ci, None)
                rhs.append(jnp.concatenate([vh * beta[i], kh * (beta[i] * eg)], axis=-1).astype(BF16))
            pws.append(a_w)
            t_invs.append(eye_w + a_w)
            rhss.append(jnp.concatenate(rhs, axis=0))
            where.append((rows, p))
    n_levels = int(math.log2(c)) - 1
    pws = [_dot(a.astype(BF16), block_diag(a.astype(BF16))) for a in pws]
    for level in range(n_levels):
        new_t, new_pw = [], []
        for t, pw in zip(t_invs, pws):
            pw16 = pw.astype(BF16)
            if level == n_levels - 1:
                new_t.append(t + _dot(t.astype(BF16), block_diag(pw16)))
            else:
                both = _dot(jnp.concatenate([t.astype(BF16), pw16], axis=0), block_diag(pw16))
                new_t.append(t + both[:c])
                new_pw.append(both[c:])
        t_invs, pws = new_t, new_pw
    for (rows, p), t_inv, rhs in zip(where, t_invs, rhss):
        t16 = t_inv.astype(BF16)
        zero = jnp.zeros_like(t16)
        lhs = jnp.concatenate([jnp.where(first, t16, zero), jnp.where(first, zero, t16)], axis=0)
        sol = _dot(lhs, rhs)
        for i, h in enumerate((2 * p, 2 * p + 1)):
            sl = slice(h * GDN_DV, (h + 1) * GDN_DV)
            u_ref[rows, sl] = sol[i * c:(i + 1) * c, :GDN_DV]
            w_ref[rows, sl] = sol[i * c:(i + 1) * c, GDN_DV:].astype(BF16)


def gdn_prepare(p_main, conv_w, ba, a_log, dt_bias, n_kheads, n_vheads):
    s_len = p_main.shape[0]
    nc = PREP_CHUNKS
    c = CHUNK * nc
    n = s_len // c
    v_w = n_vheads * GDN_DV
    conv_width = 2 * n_kheads * GDN_DK + v_w
    assert 2 * n_vheads <= 128 and n_vheads == 2 * n_kheads and c % HALO == 0
    pad = jnp.zeros((128 - 2 * n_vheads,), F32)
    alog_row = jnp.concatenate([jnp.zeros((n_vheads,), F32), a_log, pad]).reshape(1, 128)
    dtb_row = jnp.concatenate([jnp.zeros((n_vheads,), F32), dt_bias, pad]).reshape(1, 128)
    return pl.pallas_call(
        functools.partial(_gdn_prep_kernel, n_vheads=n_vheads),
        grid=(n,),
        in_specs=[
            pl.BlockSpec((c, conv_width), lambda i: (i, 0)),
            pl.BlockSpec((HALO, conv_width), lambda i: (jnp.maximum(i * (c // HALO) - 1, 0), 0)),
            pl.BlockSpec((CONV_K, conv_width), lambda i: (0, 0)),
            pl.BlockSpec((c, 128), lambda i: (i, 0)),
            pl.BlockSpec((1, 128), lambda i: (0, 0)),
            pl.BlockSpec((1, 128), lambda i: (0, 0)),
        ],
        out_specs=[
            pl.BlockSpec((c, v_w), lambda i: (i, 0)),
            pl.BlockSpec((c, v_w), lambda i: (i, 0)),
            pl.BlockSpec((c, v_w), lambda i: (i, 0)),
            pl.BlockSpec((nc, n_vheads, GDN_DK, CHUNK), lambda i: (i, 0, 0, 0)),
            pl.BlockSpec((nc, n_vheads // 2, CHUNK, 2 * CHUNK), lambda i: (i, 0, 0, 0)),
            pl.BlockSpec((nc, n_vheads, 1, 128), lambda i: (i, 0, 0, 0)),
        ],
        out_shape=[
            jax.ShapeDtypeStruct((s_len, v_w), F32),
            jax.ShapeDtypeStruct((s_len, v_w), BF16),
            jax.ShapeDtypeStruct((s_len, v_w), BF16),
            jax.ShapeDtypeStruct((n * nc, n_vheads, GDN_DK, CHUNK), BF16),
            jax.ShapeDtypeStruct((n * nc, n_vheads // 2, CHUNK, 2 * CHUNK), BF16),
            jax.ShapeDtypeStruct((n * nc, n_vheads, 1, 128), F32),
        ],
        scratch_shapes=[pltpu.VMEM((conv_width // 128, HALO + c, 128), F32)],
        compiler_params=_params("parallel"),
        name="gdn_prepare",
    )(p_main, p_main, conv_w, ba, alog_row, dtb_row)


SCAN_HG = 16
SCAN_CHUNKS = 8


def _gdn_scan_kernel(u_ref, w_ref, qd_ref, kdt_ref, intra_ref, dl_ref, z_ref, nw_ref, o_ref, state_ref):
    @pl.when(pl.program_id(1) == 0)
    def _():
        state_ref[...] = jnp.zeros_like(state_ref)

    nw = nw_ref[...]
    heads = range(SCAN_HG)
    sls = [slice(j * GDN_DV, (j + 1) * GDN_DV) for j in heads]
    first = lax.broadcasted_iota(I32, (CHUNK, 2 * CHUNK), 1) < CHUNK
    states = [state_ref[j] for j in heads]
    for ci in range(SCAN_CHUNKS):
        rows = slice(ci * CHUNK, (ci + 1) * CHUNK)
        s16 = [s.astype(BF16) for s in states]
        ws = [_dot(w_ref[rows, sls[j]], s16[j]) for j in heads]
        qs = [_dot(qd_ref[rows, sls[j]], s16[j]) for j in heads]
        v16 = [(u_ref[rows, sls[j]] - ws[j]).astype(BF16) for j in heads]
        upd = [_dot(kdt_ref[ci, j], v16[j]) for j in heads]
        outs = []
        for p in range(SCAN_HG // 2):
            pair = intra_ref[ci, p]
            zero = jnp.zeros_like(pair)
            v_pair = jnp.concatenate([v16[2 * p], v16[2 * p + 1]], axis=0)
            outs.append(qs[2 * p] + _dot(jnp.where(first, pair, zero), v_pair))
            outs.append(qs[2 * p + 1] + _dot(jnp.where(first, zero, pair), v_pair))
        states = [states[j] * dl_ref[ci, j] + upd[j] for j in heads]
        for j in heads:
            z = z_ref[rows, sls[j]].astype(F32)
            o_ref[rows, sls[j]] = (_rms(outs[j], nw) * (z * jax.nn.sigmoid(z))).astype(o_ref.dtype)
    for j in heads:
        state_ref[j] = states[j]


def gdn_scan(u, w, qd, kdt, intra, dl, p_main, norm_w, z_col_block):
    s_len, v_w = u.shape
    nc = SCAN_CHUNKS
    c = CHUNK * nc
    n = s_len // c
    hg = SCAN_HG
    gw = hg * GDN_DV
    return pl.pallas_call(
        _gdn_scan_kernel,
        grid=(v_w // gw, n),
        in_specs=[
            pl.BlockSpec((c, gw), lambda g, i: (i, g)),
            pl.BlockSpec((c, gw), lambda g, i: (i, g)),
            pl.BlockSpec((c, gw), lambda g, i: (i, g)),
            pl.BlockSpec((nc, hg, GDN_DK, CHUNK), lambda g, i: (i, g, 0, 0)),
            pl.BlockSpec((nc, hg // 2, CHUNK, 2 * CHUNK), lambda g, i: (i, g, 0, 0)),
            pl.BlockSpec((nc, hg, 1, 128), lambda g, i: (i, g, 0, 0)),
            pl.BlockSpec((c, gw), lambda g, i: (i, z_col_block + g)),
            pl.BlockSpec((1, GDN_DV), lambda g, i: (0, 0)),
        ],
        out_specs=pl.BlockSpec((c, gw), lambda g, i: (i, g)),
        out_shape=jax.ShapeDtypeStruct((s_len, v_w), BF16),
        scratch_shapes=[pltpu.VMEM((hg, GDN_DK, GDN_DV), F32)],
        compiler_params=_params("parallel", "arbitrary"),
        name="gdn_scan",
    )(u, w, qd, kdt, intra, dl, p_main, norm_w.reshape(1, GDN_DV))


PROJ_TM = 1024
PROJ_TN = 1024
FFN_TM = 512
FFN_TF = 512
OUT_TM = 512
OUT_TK_MAX = 2048


def _out_tk(k):
    return max(c for c in range(128, OUT_TK_MAX + 1, 128) if k % c == 0)


def _even_main_width(d):
    return 3 * (d // 2) + N_IDX_HEADS * IDX_HEAD_DIM


def _odd_main_width(d):
    n_kheads = d // 128
    return 2 * n_kheads * GDN_DK + 2 * (2 * n_kheads) * GDN_DV


def _even_mixer(x, g_pre, g_post, w_in16t, w_rest, w_o16, layer, pool_w, pool_scale, bias_tables):
    s_len, d = x.shape
    attn_w = d // 2
    n_heads = attn_w // ATTN_HEAD_DIM
    idx_w = N_IDX_HEADS * IDX_HEAD_DIM
    pool_width = len(POOL_WINDOWS) * POOL_G
    main_w = _even_main_width(d)
    assert main_w == 3 * attn_w + idx_w == 4096 and pool_width == 1024
    small = IDX_HEAD_DIM + N_IDX_HEADS
    assert w_rest.shape[0] == small + pool_width
    w_aux = jnp.concatenate([w_rest[small:], w_rest[:small], jnp.zeros((128 - small, d), BF16)], axis=0)
    p_main = norm_matmul(x, g_pre, w_in16t, layer, main_w, BF16, PROJ_TM, PROJ_TN)
    p_aux = norm_matmul(x, g_pre, w_aux[None], 0, w_aux.shape[0], F32, PROJ_TM, w_aux.shape[0])
    k_idx_t = p_aux[:, pool_width:pool_width + IDX_HEAD_DIM].astype(BF16).T
    kt2 = jnp.concatenate([k_idx_t, k_idx_t], axis=0)
    mask = indexer_mask(p_main, p_aux, kt2, min(TOPK_MAX, s_len // 4))
    attn = masked_attention(p_main, mask, bias_tables, n_heads)
    pooled = multiscale_pool(p_aux, pool_w.astype(BF16), pool_scale)
    return mm_norm_res([attn, pooled], w_o16, layer, g_post, x, OUT_TM, _out_tk(attn.shape[1]))


def _odd_mixer(x, g_pre, g_post, w_in16t, w_rest, w_o16, layer, conv_w, a_log, dt_bias, norm_w):
    s_len, d = x.shape
    n_kheads = d // 128
    n_vheads = 2 * n_kheads
    qk_w = n_kheads * GDN_DK
    v_w = n_vheads * GDN_DV
    conv_width = 2 * qk_w + v_w
    main_w = _odd_main_width(d)
    assert main_w == conv_width + v_w
    w_ba = jnp.concatenate([w_rest, jnp.zeros((128 - 2 * n_vheads, d), BF16)], axis=0)
    p_main = norm_matmul(x, g_pre, w_in16t, layer, main_w, BF16, PROJ_TM, PROJ_TN)
    ba = norm_matmul(x, g_pre, w_ba[None], 0, 128, F32, PROJ_TM, 128)
    u, w, qd, kdt, intra, dl = gdn_prepare(p_main, conv_w, ba, a_log, dt_bias, n_kheads, n_vheads)
    o = gdn_scan(u, w, qd, kdt, intra, dl, p_main, norm_w, conv_width // (SCAN_HG * GDN_DV))
    return mm_norm_res([o], w_o16, layer, g_post, x, OUT_TM, _out_tk(v_w))


def _ffn(x, g_pre, g_post, w_gate16, w_up16, w_down16, layer):
    return ffn(x, g_pre, g_post, w_gate16, w_up16, w_down16, layer, FFN_TM, FFN_TF)


def kernel(x, rel_bias, even_w_in, even_w_o, pool_w, pool_scale, odd_w_in, conv_w, a_log, dt_bias,
           gdn_norm_w, odd_w_o, ffn_w_gate, ffn_w_up, ffn_w_down, mix_pre_g, mix_post_g, ffn_pre_g, ffn_post_g):
    b, s_len, d = x.shape
    depth = ffn_w_gate.shape[0]
    bias_tables = _bias_tables(rel_bias)
    em, om = _even_main_width(d), _odd_main_width(d)
    even_t, odd_t = jnp.swapaxes(even_w_in, 1, 2), jnp.swapaxes(odd_w_in, 1, 2)
    even_in16 = cast_bf16(even_t)
    even_rest16 = even_in16[:, em:]
    odd_in16, odd_rest16 = cast_bf16(odd_t, 0, om), cast_bf16(odd_t, om)
    even_o16, odd_o16 = cast_bf16(even_w_o), cast_bf16(odd_w_o)
    gate16, up16, down16 = cast_bf16(ffn_w_gate), cast_bf16(ffn_w_up), cast_bf16(ffn_w_down)
    xs = x.reshape(b * s_len, d)
    outs = []
    for bi in range(b):
        h = xs[bi * s_len:(bi + 1) * s_len]
        for layer in range(depth):
            i = layer // 2
            if layer % 2 == 0:
                h = _even_mixer(h, mix_pre_g[layer], mix_post_g[layer], even_in16, even_rest16[i], even_o16, i,
                                pool_w[i], pool_scale[i], bias_tables)
            else:
                h = _odd_mixer(h, mix_pre_g[layer], mix_post_g[layer], odd_in16, odd_rest16[i], odd_o16, i,
                               conv_w[i], a_log[i], dt_bias[i], gdn_norm_w[i])
            h = _ffn(h, ffn_pre_g[layer], ffn_post_g[layer], gate16, up16, down16, layer)
        outs.append(h)
    return jnp.concatenate(outs, axis=0).reshape(b, s_len, d)
```
